```python
import functools
import jax, jax.numpy as jnp
from jax import lax
import numpy as np

D_MODEL = 1024
BATCH = 8
SEQ = 2048
DEPTH = 2
DEC_BATCH = 128
DEC_SEQ = 8
PAST_LEN = 8192
PAGE_SIZE = 128

N_META = 16
W_A = D_MODEL // 4
W_B = D_MODEL // 4
W_C = D_MODEL // 2
D_MIX = W_A + W_B + W_C
H_A = 4
BW_A = W_A // H_A
CONV_W = 4
LRU_C = 8.0
HD_B = 64
H_B = W_B // HD_B
LORA_W = 32
LORA_A = 32
GN_EPS = 64e-5
H_C = 8
V_DIM = W_C // H_C
NOPE_DIM = 64
ROPE_DIM = 32
Q_RANK = D_MODEL // 4
KV_RANK = D_MODEL // 4
ROPE_THETA = 10000.0
Q_BLOCK = 128
ATTN_SCALE = (NOPE_DIM + ROPE_DIM) ** -0.5
DN_ALPHA = (2 * DEPTH) ** 0.25
DN_BETA = (8 * DEPTH) ** -0.25
_SEG = (W_A, W_A, W_B, W_B, Q_RANK, KV_RANK, ROPE_DIM, W_C)
D_IN = sum(_SEG)
SPLIT_IDX = tuple(int(i) for i in np.cumsum(_SEG)[:-1])

F32 = jnp.float32
NEG_INF = -1e30

kernel_name = 'hymba_rglru_rwkv7_mla_step'


def _rms_norm(x, g, eps=1e-6):
    xf = x.astype(F32)
    y = xf * lax.rsqrt(jnp.mean(xf * xf, axis=-1, keepdims=True) + eps)
    return (y * g.astype(F32)).astype(x.dtype)


def _layer_norm(x, g, b, eps=1e-5):
    xf = x.astype(F32)
    mu = jnp.mean(xf, axis=-1, keepdims=True)
    var = jnp.mean(jnp.square(xf - mu), axis=-1, keepdims=True)
    return ((xf - mu) * lax.rsqrt(var + eps) * g.astype(F32) + b.astype(F32)).astype(x.dtype)


def _rope(x, pos):
    half = x.shape[-1] // 2
    inv = ROPE_THETA ** (-2.0 * jnp.arange(half, dtype=F32) / x.shape[-1])
    ang = pos.astype(F32)[:, None] * inv[None, :]
    cos = jnp.cos(ang)[None, :, None, :]
    sin = jnp.sin(ang)[None, :, None, :]
    xf = x.astype(F32)
    x1, x2 = xf[..., :half], xf[..., half:]
    return jnp.concatenate([x1 * cos - x2 * sin, x1 * sin + x2 * cos], axis=-1).astype(x.dtype)


def _lin_combine(left, right):
    a1, b1 = left
    a2, b2 = right
    return a1 * a2, a2 * b1 + b2


def _rglru_branch(u, conv_prev, h_prev, reset, lp):
    bsz, t_len, _ = u.shape
    u_ext = jnp.concatenate([conv_prev.astype(u.dtype), u], axis=1)
    xc = lp['conv_b'] + u_ext[:, 0:t_len] * lp['conv_w'][0]
    for tap in range(1, CONV_W):
        xc = xc + u_ext[:, tap:tap + t_len] * lp['conv_w'][tap]
    conv_new = u_ext[:, t_len:]
    xh = xc.reshape(bsz, t_len, H_A, BW_A)
    gate_r = jax.nn.sigmoid((jnp.einsum('bthi,hij->bthj', xh, lp['lru_wr']).reshape(bsz, t_len, W_A) + lp['lru_br']).astype(F32))
    gate_i = jax.nn.sigmoid((jnp.einsum('bthi,hij->bthj', xh, lp['lru_wi']).reshape(bsz, t_len, W_A) + lp['lru_bi']).astype(F32))
    log_a = -LRU_C * gate_r * jax.nn.softplus(-lp['lru_lambda'].astype(F32))
    rs = reset[None, :, None]
    a = jnp.where(rs, 0.0, jnp.exp(log_a))
    mult = jnp.where(rs, 1.0, jnp.sqrt(-jnp.expm1(2.0 * log_a)))
    b = xc.astype(F32) * gate_i * mult
    b = b.at[:, 0].add(a[:, 0] * h_prev.astype(F32))
    _, h = lax.associative_scan(_lin_combine, (a, b), axis=1)
    return h.astype(u.dtype), conv_new, h[:, -1].astype(u.dtype)


def _rwkv7_branch(u, shift_prev, s_prev, lp):
    bsz, t_len, _ = u.shape
    u_prev = jnp.concatenate([shift_prev[:, None, :].astype(u.dtype), u[:, :-1]], axis=1)
    du = u_prev - u
    mix = lp['rk_mix']
    xr = u + du * mix[0]
    xw = u + du * mix[1]
    xk = u + du * mix[2]
    xv = u + du * mix[3]
    xa = u + du * mix[4]
    r = xr @ lp['rk_wr']
    k = xk @ lp['rk_wk']
    v = xv @ lp['rk_wv']
    w_log = -jax.nn.softplus(-(lp['rk_w0'] + jnp.tanh(xw @ lp['rk_w1']) @ lp['rk_w2']).astype(F32)) - 0.5
    decay = jnp.exp(-jnp.exp(w_log))
    a = jax.nn.sigmoid((lp['rk_a0'] + (xa @ lp['rk_a1']) @ lp['rk_a2']).astype(F32))
    kk = (k.astype(F32) * lp['rk_kk'].astype(F32)).reshape(bsz, t_len, H_B, HD_B)
    kk = kk / jnp.maximum(jnp.sqrt(jnp.sum(kk * kk, axis=-1, keepdims=True)), 1e-12)
    k_mod = k.astype(F32) * (1.0 + (a - 1.0) * lp['rk_ka'].astype(F32))
    hs = lambda t: t.astype(F32).reshape(bsz, t_len, H_B, HD_B)
    r_h, w_h, k_h, v_h, a_h = hs(r), hs(decay), hs(k_mod), hs(v), hs(a)

    def step(s, inp):
        r_t, w_t, k_t, v_t, kk_t, a_t = inp
        sa = jnp.einsum('bhij,bhj->bhi', s, -kk_t)
        s = s * w_t[:, :, None, :] + sa[..., None] * (kk_t * a_t)[:, :, None, :] + v_t[..., None] * k_t[:, :, None, :]
        return s, jnp.einsum('bhij,bhj->bhi', s, r_t)

    xs = tuple(jnp.moveaxis(t, 1, 0) for t in (r_h, w_h, k_h, v_h, kk, a_h))
    s_fin, o = lax.scan(step, s_prev.astype(F32), xs)
    o = jnp.moveaxis(o, 0, 1)
    mu = jnp.mean(o, axis=-1, keepdims=True)
    var = jnp.mean(jnp.square(o - mu), axis=-1, keepdims=True)
    o = (o - mu) * lax.rsqrt(var + GN_EPS) * lp['rk_gn_g'].astype(F32).reshape(H_B, HD_B) + lp['rk_gn_b'].astype(F32).reshape(H_B, HD_B)
    o = o + jnp.sum(r_h * k_h * lp['rk_rk'].astype(F32), axis=-1, keepdims=True) * v_h
    return o.reshape(bsz, t_len, W_B).astype(u.dtype), u[:, -1], s_fin.astype(u.dtype)


def _prompt_attention(q_lat, q_rope, ckv, kr, pos):
    bsz, t_len, nh, rank = q_lat.shape
    nb = -(-t_len // Q_BLOCK)
    pad = nb * Q_BLOCK - t_len
    ql = jnp.pad(q_lat, ((0, 0), (0, pad), (0, 0), (0, 0))).reshape(bsz, nb, Q_BLOCK, nh, rank).transpose(1, 0, 2, 3, 4)
    qr = jnp.pad(q_rope, ((0, 0), (0, pad), (0, 0), (0, 0))).reshape(bsz, nb, Q_BLOCK, nh, ROPE_DIM).transpose(1, 0, 2, 3, 4)
    qpos = jnp.arange(nb * Q_BLOCK).reshape(nb, Q_BLOCK)

    def block(args):
        ql_b, qr_b, qp_b = args
        s = jnp.einsum('bqhr,bkr->bhqk', ql_b, ckv) + jnp.einsum('bqhd,bkd->bhqk', qr_b, kr)
        s = jnp.where(pos[None, :] <= qp_b[:, None], s.astype(F32) * ATTN_SCALE, NEG_INF)
        p = jax.nn.softmax(s, axis=-1).astype(ckv.dtype)
        return jnp.einsum('bhqk,bkr->bqhr', p, ckv)

    o = lax.map(block, (ql, qr, qpos))
    return o.transpose(1, 0, 2, 3, 4).reshape(bsz, nb * Q_BLOCK, nh, rank)[:, :t_len]


def _sample_attention(ckv_past, kr_past, q_lat, q_rope, ckv_new, kr_new):
    tq = q_lat.shape[1]
    s_past = jnp.einsum('bqhr,bkr->bhqk', q_lat, ckv_past) + jnp.einsum('bqhd,bkd->bhqk', q_rope, kr_past)
    s_new = jnp.einsum('bqhr,bkr->bhqk', q_lat, ckv_new) + jnp.einsum('bqhd,bkd->bhqk', q_rope, kr_new)
    causal = jnp.tril(jnp.ones((tq, tq), dtype=bool))
    s_new = jnp.where(causal, s_new.astype(F32) * ATTN_SCALE, NEG_INF)
    s = jnp.concatenate([s_past.astype(F32) * ATTN_SCALE, s_new], axis=-1)
    p = jax.nn.softmax(s, axis=-1).astype(ckv_new.dtype)
    n_past = ckv_past.shape[1]
    return jnp.einsum('bhqk,bkr->bqhr', p[..., :n_past], ckv_past) + jnp.einsum('bhqk,bkr->bqhr', p[..., n_past:], ckv_new)


def _trunk_layer(x, pos, conv_prev, lru_prev, shift_prev, wkv_prev, attend, lp):
    bsz, t_len, _ = x.shape
    z = x @ lp['w_in']
    u_a, g_a, u_b, g_b, c_q, c_kv, k_r, g_c = jnp.split(z, SPLIT_IDX, axis=-1)
    y_a, conv_new, lru_new = _rglru_branch(u_a, conv_prev, lru_prev, pos == 0, lp)
    y_b, shift_new, wkv_new = _rwkv7_branch(u_b, shift_prev, wkv_prev, lp)
    c_q = _rms_norm(c_q, lp['q_norm'])
    ckv = _rms_norm(c_kv, lp['kv_norm'])
    kr = _rope(k_r[:, :, None, :], pos)[:, :, 0]
    q = (c_q @ lp['w_uq']).reshape(bsz, t_len, H_C, NOPE_DIM + ROPE_DIM)
    q_nope = q[..., :NOPE_DIM]
    q_rope = _rope(q[..., NOPE_DIM:], pos)
    q_lat = jnp.einsum('bthd,rhd->bthr', q_nope, lp['w_uk'])
    o_lat = attend(q_lat, q_rope, ckv, kr)
    y_c = jnp.einsum('bthr,rhv->bthv', o_lat, lp['w_uv']).reshape(bsz, t_len, W_C)
    mixed = jnp.concatenate([y_a * jax.nn.silu(g_a), y_b * jax.nn.silu(g_b), y_c * jax.nn.silu(g_c)], axis=-1)
    out = mixed @ lp['w_out']
    x_new = _layer_norm(DN_ALPHA * x + out, lp['ln_g'], lp['ln_b'])
    return x_new, (ckv, kr, conv_new, lru_new, shift_new, wkv_new)


def setup_inputs(seed: int = 0) -> dict:
    key = jax.random.key(seed)
    ks = iter(jax.random.split(key, 64))

    def nrm(shape, scale=1.0):
        return scale * jax.random.normal(next(ks), shape, F32)

    n_pages = PAST_LEN // PAGE_SIZE
    n_used = DEC_BATCH * n_pages
    n_pool = n_used + n_used // 4
    L = DEPTH
    inp = {}
    inp['x_prompt'] = nrm((BATCH, SEQ, D_MODEL))
    inp['x_sample'] = nrm((DEC_BATCH, DEC_SEQ, D_MODEL))
    inp['cache_ckv'] = nrm((L, n_pool, PAGE_SIZE, KV_RANK))
    inp['cache_krope'] = nrm((L, n_pool, PAGE_SIZE, ROPE_DIM))
    inp['state_conv'] = nrm((L, DEC_BATCH, CONV_W - 1, W_A))
    inp['state_lru'] = nrm((L, DEC_BATCH, W_A), 0.5)
    inp['state_shift'] = nrm((L, DEC_BATCH, W_B))
    inp['state_wkv'] = nrm((L, DEC_BATCH, H_B, HD_B, HD_B), 0.3)
    inp['page_table'] = jax.random.permutation(next(ks), n_pool)[:n_used].reshape(DEC_BATCH, n_pages).astype(jnp.int32)
    inp['meta_tokens'] = nrm((N_META, D_MODEL))
    inp['w_in'] = nrm((L, D_MODEL, D_IN), D_MODEL ** -0.5)
    inp['conv_w'] = nrm((L, CONV_W, W_A), CONV_W ** -0.5)
    inp['conv_b'] = nrm((L, W_A), 0.02)
    inp['lru_wr'] = nrm((L, H_A, BW_A, BW_A), BW_A ** -0.5)
    inp['lru_br'] = nrm((L, W_A), 0.02)
    inp['lru_wi'] = nrm((L, H_A, BW_A, BW_A), BW_A ** -0.5)
    inp['lru_bi'] = nrm((L, W_A), 0.02)
    a_init = jax.random.uniform(next(ks), (L, W_A), F32, 0.9, 0.999)
    inp['lru_lambda'] = jnp.log(a_init) - jnp.log1p(-a_init)
    inp['rk_mix'] = jax.random.uniform(next(ks), (L, 5, W_B), F32)
    inp['rk_wr'] = nrm((L, W_B, W_B), W_B ** -0.5)
    inp['rk_wk'] = nrm((L, W_B, W_B), W_B ** -0.5)
    inp['rk_wv'] = nrm((L, W_B, W_B), W_B ** -0.5)
    inp['rk_w0'] = jax.random.uniform(next(ks), (L, W_B), F32, -6.0, 1.0)
    inp['rk_w1'] = nrm((L, W_B, LORA_W), W_B ** -0.5)
    inp['rk_w2'] = nrm((L, LORA_W, W_B), 0.1 * LORA_W ** -0.5)
    inp['rk_a0'] = nrm((L, W_B), 0.1)
    inp['rk_a1'] = nrm((L, W_B, LORA_A), W_B ** -0.5)
    inp['rk_a2'] = nrm((L, LORA_A, W_B), LORA_A ** -0.5)
    inp['rk_kk'] = 0.85 + nrm((L, W_B), 0.02)
    inp['rk_ka'] = 1.0 + nrm((L, W_B), 0.02)
    inp['rk_rk'] = nrm((L, H_B, HD_B), 0.1)
    inp['rk_gn_g'] = 1.0 + nrm((L, W_B), 0.02)
    inp['rk_gn_b'] = nrm((L, W_B), 0.02)
    inp['q_norm'] = 1.0 + nrm((L, Q_RANK), 0.02)
    inp['kv_norm'] = 1.0 + nrm((L, KV_RANK), 0.02)
    inp['w_uq'] = nrm((L, Q_RANK, H_C * (NOPE_DIM + ROPE_DIM)), Q_RANK ** -0.5)
    inp['w_uk'] = nrm((L, KV_RANK, H_C, NOPE_DIM), KV_RANK ** -0.5)
    inp['w_uv'] = nrm((L, KV_RANK, H_C, V_DIM), KV_RANK ** -0.5)
    inp['w_out'] = nrm((L, D_MIX, D_MODEL), DN_BETA * D_MIX ** -0.5)
    inp['ln_g'] = 1.0 + nrm((L, D_MODEL), 0.02)
    inp['ln_b'] = nrm((L, D_MODEL), 0.02)
    return inp


def reference(x_prompt, x_sample, cache_ckv, cache_krope, state_conv, state_lru, state_shift, state_wkv,
              page_table, meta_tokens, w_in, conv_w, conv_b, lru_wr, lru_br, lru_wi, lru_bi, lru_lambda,
              rk_mix, rk_wr, rk_wk, rk_wv, rk_w0, rk_w1, rk_w2, rk_a0, rk_a1, rk_a2, rk_kk, rk_ka, rk_rk,
              rk_gn_g, rk_gn_b, q_norm, kv_norm, w_uq, w_uk, w_uv, w_out, ln_g, ln_b):
    bp = x_prompt.shape[0]
    bs, ts = x_sample.shape[0], x_sample.shape[1]
    n_past = page_table.shape[1] * cache_ckv.shape[2]
    xp = jnp.concatenate([jnp.broadcast_to(meta_tokens[None].astype(x_prompt.dtype), (bp, N_META, D_MODEL)), x_prompt], axis=1)
    pos_p = jnp.arange(xp.shape[1])
    pos_s = n_past + jnp.arange(ts)
    xs = x_sample
    p_states = []
    s_states = []
    for l in range(DEPTH):
        lp = {'w_in': w_in[l], 'conv_w': conv_w[l], 'conv_b': conv_b[l], 'lru_wr': lru_wr[l], 'lru_br': lru_br[l],
              'lru_wi': lru_wi[l], 'lru_bi': lru_bi[l], 'lru_lambda': lru_lambda[l], 'rk_mix': rk_mix[l],
              'rk_wr': rk_wr[l], 'rk_wk': rk_wk[l], 'rk_wv': rk_wv[l], 'rk_w0': rk_w0[l], 'rk_w1': rk_w1[l],
              'rk_w2': rk_w2[l], 'rk_a0': rk_a0[l], 'rk_a1': rk_a1[l], 'rk_a2': rk_a2[l], 'rk_kk': rk_kk[l],
              'rk_ka': rk_ka[l], 'rk_rk': rk_rk[l], 'rk_gn_g': rk_gn_g[l], 'rk_gn_b': rk_gn_b[l],
              'q_norm': q_norm[l], 'kv_norm': kv_norm[l], 'w_uq': w_uq[l], 'w_uk': w_uk[l], 'w_uv': w_uv[l],
              'w_out': w_out[l], 'ln_g': ln_g[l], 'ln_b': ln_b[l]}
        dt = xp.dtype
        xp, st_p = _trunk_layer(
            xp, pos_p,
            jnp.zeros((bp, CONV_W - 1, W_A), dt), jnp.zeros((bp, W_A), dt),
            jnp.zeros((bp, W_B), dt), jnp.zeros((bp, H_B, HD_B, HD_B), dt),
            functools.partial(_prompt_attention, pos=pos_p), lp)
        p_states.append(st_p)
        ckv_past = cache_ckv[l, page_table].reshape(bs, n_past, KV_RANK)
        kr_past = cache_krope[l, page_table].reshape(bs, n_past, ROPE_DIM)
        xs, st_s = _trunk_layer(
            xs, pos_s, state_conv[l], state_lru[l], state_shift[l], state_wkv[l],
            functools.partial(_sample_attention, ckv_past, kr_past), lp)
        s_states.append(st_s)
    y_prompt = xp[:, N_META:]
    y_sample = xs
    p_ckv, p_krope, p_conv, p_lru, p_shift, p_wkv = [jnp.stack([st[i] for st in p_states]) for i in range(6)]
    s_ckv, s_krope, s_conv, s_lru, s_shift, s_wkv = [jnp.stack([st[i] for st in s_states]) for i in range(6)]
    return (y_prompt, y_sample, p_ckv, p_krope, p_conv, p_lru, p_shift, p_wkv,
            s_ckv, s_krope, s_conv, s_lru, s_shift, s_wkv)
```

```python
import functools

import numpy as np
import jax
import jax.numpy as jnp
from jax import lax
from jax.experimental import pallas as pl
from jax.experimental.pallas import tpu as pltpu

F32 = jnp.float32
BF16 = jnp.bfloat16
HIGHEST = lax.Precision.HIGHEST

D_MODEL = 1024
N_META = 16
W_A = 256
W_B = 256
W_C = 512
H_A = 4
CONV_W = 4
LRU_C = 8.0
HD_B = 64
H_B = 4
GN_EPS = 64e-5
H_C = 8
V_DIM = 64
NOPE_DIM = 64
ROPE_DIM = 32
ROPE_HALF = 16
Q_RANK = 256
KV_RANK = 256
ROPE_THETA = 10000.0
ATTN_SCALE = (NOPE_DIM + ROPE_DIM) ** -0.5
NEG_INF = -1e30
DEPTH = 2
DN_ALPHA = (2 * DEPTH) ** 0.25

Z_COLS = 2304
CHUNK = 64
ATT_TILE = 256
PAGES_PER_STEP = 8

VMEM_LIMIT = 48 * 1024 * 1024


def _cparams(*sem):
    return pltpu.CompilerParams(dimension_semantics=sem, vmem_limit_bytes=VMEM_LIMIT)


def _dot(a, b):
    return jnp.dot(a, b, preferred_element_type=F32)


def _dot_nt(a, b):
    return lax.dot_general(a, b, (((1,), (1,)), ((), ())), preferred_element_type=F32)


def _dot_tn(a, b):
    return lax.dot_general(a, b, (((0,), (0,)), ((), ())), preferred_element_type=F32)


def _dot_hi(a, b):
    return jnp.dot(a, b, preferred_element_type=F32, precision=HIGHEST)


def _softplus(x):
    return jnp.maximum(x, 0.0) + jnp.log1p(jnp.exp(-jnp.abs(x)))


def _silu(x):
    return x * jax.nn.sigmoid(x)


def _mm_kernel(x_ref, w_ref, o_ref):
    o_ref[...] = _dot(x_ref[...].astype(BF16), w_ref[...])


def _in_proj(x, w, tm):
    n, k = x.shape
    m = w.shape[1]
    return pl.pallas_call(
        _mm_kernel,
        grid=(n // tm,),
        in_specs=[pl.BlockSpec((tm, k), lambda i: (i, 0)), pl.BlockSpec((k, m), lambda i: (0, 0))],
        out_specs=pl.BlockSpec((tm, m), lambda i: (i, 0)),
        out_shape=jax.ShapeDtypeStruct((n, m), F32),
        compiler_params=_cparams("parallel"),
        name="in_proj",
    )(x, w)


def _rglru_kernel(*refs, seq_len, pos0, has_state):
    if has_state:
        (u_ref, g_ref, f_ref, hp_ref, cw_ref, cb_ref, wr_ref, br_ref, wi_ref, bi_ref, lam_ref,
         y_ref, h_ref) = refs
    else:
        (u_ref, g_ref, cw_ref, cb_ref, wr_ref, br_ref, wi_ref, bi_ref, lam_ref,
         y_ref, h_ref, a_scr, b_scr) = refs
    u = u_ref[...]
    rows = u.shape[0]
    row = lax.broadcasted_iota(jnp.int32, u.shape, 0)
    t = row if rows == seq_len else row & (seq_len - 1)

    cw = cw_ref[...]
    xc = cb_ref[...] + u * cw[CONV_W - 1:CONV_W]
    for k in range(1, CONV_W):
        sh = pltpu.roll(u, k, 0)
        if has_state:
            f = f_ref[...]
            prev = f if k == CONV_W - 1 else pltpu.roll(f, rows - (CONV_W - 1 - k), 0)
        else:
            prev = 0.0
        sh = jnp.where(t < k, prev, sh)
        xc = xc + sh * cw[CONV_W - 1 - k:CONV_W - k]

    xcb = xc.astype(BF16)
    gate_r = jax.nn.sigmoid(_dot(xcb, wr_ref[...]) + br_ref[...])
    gate_i = jax.nn.sigmoid(_dot(xcb, wi_ref[...]) + bi_ref[...])
    log_a = (-LRU_C) * gate_r * _softplus(-lam_ref[...])
    th = jnp.tanh(log_a)
    neg_expm1 = -2.0 * th / (1.0 - th)
    reset = (t + pos0) == 0
    a = jnp.where(reset, 0.0, jnp.exp(log_a))
    mult = jnp.where(reset, 1.0, jnp.sqrt(neg_expm1))
    bv = xc * gate_i * mult

    t8 = row & 7
    for s in (1, 2, 4):
        a_s = jnp.where(t8 >= s, pltpu.roll(a, s, 0), 1.0)
        b_s = jnp.where(t8 >= s, pltpu.roll(bv, s, 0), 0.0)
        bv = a * b_s + bv
        a = a * a_s

    if has_state:
        h = bv + a * hp_ref[...]
        h_ref[...] = h
    else:
        a_scr[...] = a
        b_scr[...] = bv

        def body(i, carry):
            r0 = pl.multiple_of(i * 8, 8)
            hh = b_scr[pl.ds(r0, 8), :] + a_scr[pl.ds(r0, 8), :] * carry
            b_scr[pl.ds(r0, 8), :] = hh
            return jnp.broadcast_to(hh[7:8, :], hh.shape)

        carry = lax.fori_loop(0, rows // 8, body, jnp.zeros((8, u.shape[1]), F32))
        h = b_scr[...]
        h_ref[...] = carry[0:1, :]
    y_ref[...] = (h * _silu(g_ref[...])).astype(BF16)


def _rglru(z, lw, *, nseq, seq_len, pos0, conv_ext=None, h_ext=None):
    n = z.shape[0]
    wspecs = [pl.BlockSpec(w.shape, lambda i: (0,) * w.ndim) for w in lw]
    if conv_ext is None:
        rows = seq_len
        in_specs = [pl.BlockSpec((rows, W_A), lambda i: (i, 0)),
                    pl.BlockSpec((rows, W_A), lambda i: (i, 1))] + wspecs
        out_specs = [pl.BlockSpec((rows, W_A), lambda i: (i, 0)),
                     pl.BlockSpec((None, 1, W_A), lambda i: (i, 0, 0))]
        out_shape = [jax.ShapeDtypeStruct((n, W_A), BF16), jax.ShapeDtypeStruct((nseq, 1, W_A), F32)]
        scratch = [pltpu.VMEM((rows, W_A), F32), pltpu.VMEM((rows, W_A), F32)]
        args = (z, z) + tuple(lw)
        grid = (nseq,)
    else:
        rows = n
        full = pl.BlockSpec((rows, W_A), lambda i: (0, 0))
        in_specs = [full, pl.BlockSpec((rows, W_A), lambda i: (0, 1)), full, full] + wspecs
        out_specs = [full, full]
        out_shape = [jax.ShapeDtypeStruct((n, W_A), BF16), jax.ShapeDtypeStruct((n, W_A), F32)]
        scratch = []
        args = (z, z, conv_ext, h_ext) + tuple(lw)
        grid = (1,)
    return pl.pallas_call(
        functools.partial(_rglru_kernel, seq_len=seq_len, pos0=pos0, has_state=conv_ext is not None),
        grid=grid, in_specs=in_specs, out_specs=out_specs, out_shape=out_shape,
        scratch_shapes=scratch, compiler_params=_cparams("parallel"), name="rglru",
    )(*args)


def _rwkv_prep_kernel(u_ref, p_ref, mix_ref, wr_ref, wk_ref, wv_ref, w0_ref, w1_ref, w2_ref,
                      a0_ref, a1_ref, a2_ref, kkw_ref, ka_ref, ones_ref,
                      r_ref, lw_ref, k_ref, v_ref, kk_ref, a_ref, *, seq_len, tiles_per_seq):
    u = u_ref[...]
    row = lax.broadcasted_iota(jnp.int32, u.shape, 0)
    rolled = pltpu.roll(u, 1, 0)
    if tiles_per_seq is None:
        u_prev = jnp.where((row & (seq_len - 1)) == 0, p_ref[...], rolled)
    else:
        first = (pl.program_id(0) % tiles_per_seq) == 0
        prow = jnp.where(first, 0.0, p_ref[7:8, :])
        u_prev = jnp.where(row == 0, prow, rolled)
    du = u_prev - u
    mix = mix_ref[...]
    xr = (u + du * mix[0:1]).astype(BF16)
    xw = (u + du * mix[1:2]).astype(BF16)
    xk = (u + du * mix[2:3]).astype(BF16)
    xv = (u + du * mix[3:4]).astype(BF16)
    xa = (u + du * mix[4:5]).astype(BF16)
    r = _dot(xr, wr_ref[...])
    k = _dot(xk, wk_ref[...])
    v = _dot(xv, wv_ref[...])
    lora_w = _dot(jnp.tanh(_dot(xw, w1_ref[...])).astype(BF16), w2_ref[...])
    w_log = -_softplus(-(w0_ref[...] + lora_w)) - 0.5
    a = jax.nn.sigmoid(a0_ref[...] + _dot(_dot(xa, a1_ref[...]).astype(BF16), a2_ref[...]))
    kk = k * kkw_ref[...]
    ssq = _dot_hi(kk * kk, ones_ref[...])
    kk = kk / jnp.maximum(jnp.sqrt(ssq), 1e-12)
    r_ref[...] = r
    lw_ref[...] = -jnp.exp(w_log)
    k_ref[...] = k * (1.0 + (a - 1.0) * ka_ref[...])
    v_ref[...] = v
    kk_ref[...] = kk
    a_ref[...] = a


def _rwkv_prep(z, pw, *, seq_len, tm, prev_ext=None):
    n = z.shape[0]
    wspecs = [pl.BlockSpec(w.shape, lambda i: (0,) * w.ndim) for w in pw]
    if prev_ext is None:
        tiles = seq_len // tm
        p_arg = z
        p_spec = pl.BlockSpec((8, W_B), lambda i: (jnp.maximum(i * (tm // 8) - 1, 0), 2))
    else:
        tiles = None
        p_arg = prev_ext
        p_spec = pl.BlockSpec((tm, W_B), lambda i: (i, 0))
    out = jax.ShapeDtypeStruct((n, W_B), F32)
    ospec = pl.BlockSpec((tm, W_B), lambda i: (i, 0))
    return pl.pallas_call(
        functools.partial(_rwkv_prep_kernel, seq_len=seq_len, tiles_per_seq=tiles),
        grid=(n // tm,),
        in_specs=[pl.BlockSpec((tm, W_B), lambda i: (i, 2)), p_spec] + wspecs,
        out_specs=[ospec] * 6, out_shape=[out] * 6,
        compiler_params=_cparams("parallel"), name="rwkv_prep",
    )(z, p_arg, *pw)


def _wkv_kernel(r_ref, lw_ref, k_ref, v_ref, kk_ref, a_ref, g_ref, s0_ref, hm_ref, ones_ref,
                rk_ref, gng_ref, gnb_ref, y_ref, s_ref, *, t_valid, nseq, sub_len):
    c = pl.program_id(1)
    width = W_B
    nh = H_B

    @pl.when(c == 0)
    def _():
        s_ref[...] = s0_ref[...]

    row = lax.broadcasted_iota(jnp.int32, (CHUNK, width), 0)
    valid = (c * CHUNK + row) < t_valid
    zero = jnp.zeros((CHUNK, width), F32)
    r = jnp.where(valid, r_ref[...], zero)
    lw = jnp.where(valid, lw_ref[...], zero)
    k = jnp.where(valid, k_ref[...], zero)
    v = jnp.where(valid, v_ref[...], zero)
    kk = jnp.where(valid, kk_ref[...], zero)
    a = jnp.where(valid, a_ref[...], zero)

    qi = lax.broadcasted_iota(jnp.int32, (CHUNK, CHUNK), 0)
    si = lax.broadcasted_iota(jnp.int32, (CHUNK, CHUNK), 1)
    same = (qi // sub_len) == (si // sub_len)
    cum = _dot_hi(jnp.where(same & (qi >= si), 1.0, 0.0), lw)
    tot = _dot_hi(jnp.where(same, 1.0, 0.0), lw)
    e_neg = jnp.exp(-cum)
    e_tail = jnp.exp(tot - cum)
    rg = r * jnp.exp(cum)
    ag = -kk * jnp.exp(cum - lw)
    kka = kk * a
    bg = kka * e_neg
    kg = k * e_neg
    bt = kka * e_tail
    kt = k * e_tail

    hm = hm_ref[...]

    def stack(x):
        return jnp.concatenate([x * hm[h:h + 1] for h in range(nh)], axis=0)

    def unstack(x):
        out = x[0:CHUNK]
        for h in range(1, nh):
            out = out + x[h * CHUNK:(h + 1) * CHUNK]
        return out

    a_st = stack(ag).astype(BF16)
    r_st = stack(rg)
    b_st = stack(bg).astype(BF16)
    k_st = stack(kg).astype(BF16)
    v_st = stack(v)

    n_st = nh * CHUNK
    rq = lax.broadcasted_iota(jnp.int32, (n_st, n_st), 0) % CHUNK
    cs = lax.broadcasted_iota(jnp.int32, (n_st, n_st), 1) % CHUNK
    same_st = (rq // sub_len) == (cs // sub_len)
    strict = same_st & (rq > cs)
    incl = same_st & (rq >= cs)

    l_ab = jnp.where(strict, _dot_nt(a_st, b_st), 0.0)
    l_ak = jnp.where(strict, _dot_nt(a_st, k_st), 0.0)
    p_rb = jnp.where(incl, _dot_nt(r_st.astype(BF16), b_st), 0.0)
    p_rk = jnp.where(incl, _dot_nt(r_st.astype(BF16), k_st), 0.0)

    eye = jnp.where(lax.broadcasted_iota(jnp.int32, (n_st, n_st), 0)
                    == lax.broadcasted_iota(jnp.int32, (n_st, n_st), 1), 1.0, 0.0)
    tinv = eye + l_ab
    lp = l_ab
    levels = int(np.log2(sub_len))
    for _ in range(levels - 1):
        lp = _dot_hi(lp, lp)
        tinv = tinv + _dot_hi(lp, tinv)

    x0_parts, r0_parts = [], []
    for s in range(nseq):
        st = s_ref[s].astype(BF16)
        sl = slice(s * sub_len, (s + 1) * sub_len)
        rows_s = jnp.concatenate([ag[sl], rg[sl]], axis=0).astype(BF16)
        xs = _dot_nt(rows_s, st)
        x0_parts.append(xs[0:sub_len])
        r0_parts.append(xs[sub_len:2 * sub_len])
    x0 = x0_parts[0] if nseq == 1 else jnp.concatenate(x0_parts, axis=0)
    r0 = r0_parts[0] if nseq == 1 else jnp.concatenate(r0_parts, axis=0)

    x_st = stack(x0) + _dot(l_ak.astype(BF16), v_st.astype(BF16))
    u_st = _dot_hi(tinv, x_st)
    o_st = stack(r0) + _dot(p_rb.astype(BF16), u_st.astype(BF16)) + _dot(p_rk.astype(BF16), v_st.astype(BF16))
    u_all = unstack(u_st)
    o = unstack(o_st)

    bd = (lax.broadcasted_iota(jnp.int32, (width, width), 0) // HD_B
          == lax.broadcasted_iota(jnp.int32, (width, width), 1) // HD_B)
    for s in range(nseq):
        sl = slice(s * sub_len, (s + 1) * sub_len)
        upd = (_dot_tn(u_all[sl].astype(BF16), bt[sl].astype(BF16))
               + _dot_tn(v[sl].astype(BF16), kt[sl].astype(BF16)))
        gam = jnp.exp(tot[s * sub_len:s * sub_len + 1])
        s_ref[s] = s_ref[s] * gam + jnp.where(bd, upd, 0.0)

    ones = ones_ref[...]
    mu = _dot_hi(o, ones) * (1.0 / HD_B)
    d = o - mu
    var = _dot_hi(d * d, ones) * (1.0 / HD_B)
    o = d * lax.rsqrt(var + GN_EPS) * gng_ref[...] + gnb_ref[...]
    o = o + _dot_hi(r * k * rk_ref[...], ones) * v
    y_ref[...] = (o * _silu(g_ref[...])).astype(BF16)


def _wkv(z3, prep, s0, cw, *, t_valid, nseq, sub_len):
    nb, t_len, _ = z3.shape
    nc = -(-t_len // CHUNK)
    seq = pl.BlockSpec((None, CHUNK, W_B), lambda b, c: (b, c, 0))
    wspecs = [pl.BlockSpec(w.shape, lambda b, c: (0,) * w.ndim) for w in cw]
    sspec = pl.BlockSpec((nseq, W_B, W_B), lambda b, c: (b, 0, 0))
    return pl.pallas_call(
        functools.partial(_wkv_kernel, t_valid=t_valid, nseq=nseq, sub_len=sub_len),
        grid=(nb, nc),
        in_specs=[seq] * 6 + [pl.BlockSpec((None, CHUNK, W_B), lambda b, c: (b, c, 3)), sspec] + wspecs,
        out_specs=[seq, sspec],
        out_shape=[jax.ShapeDtypeStruct((nb, t_len, W_B), BF16),
                   jax.ShapeDtypeStruct(s0.shape, F32)],
        compiler_params=_cparams("parallel", "arbitrary"), name="wkv",
    )(*prep, z3, s0, *cw)


def _mla_prep_kernel(cq_ref, ckv_ref, za_ref, zb_ref, cos_ref, sin_ref, qn_ref, kvn_ref, wr1_ref, wr2_ref,
                     ckv_out, krt_out, kp_out, cqn_out, of_out, *, t_valid):
    tm = cq_ref.shape[0]
    row = lax.broadcasted_iota(jnp.int32, (tm, 1), 0)
    valid = (pl.program_id(1) * tm + row) < t_valid
    cq = cq_ref[...]
    cqn = cq * lax.rsqrt(jnp.mean(cq * cq, axis=-1, keepdims=True) + 1e-6) * qn_ref[...]
    cv = ckv_ref[...]
    ckv = cv * lax.rsqrt(jnp.mean(cv * cv, axis=-1, keepdims=True) + 1e-6) * kvn_ref[...]
    cos = cos_ref[...]
    sin = sin_ref[...]
    za = za_ref[...]
    zb = zb_ref[...]
    kr1 = za * cos - zb * sin
    kr2 = za * sin + zb * cos
    cqb = cqn.astype(BF16)
    q1 = _dot(cqb, wr1_ref[...])
    q2 = _dot(cqb, wr2_ref[...])
    o1 = q1 * cos - q2 * sin
    o2 = q1 * sin + q2 * cos
    half = cos.shape[1]
    ckv_out[...] = ckv
    krt_out[:, 0:half] = kr1
    krt_out[:, half:] = kr2
    kp_out[:, 0:KV_RANK] = jnp.where(valid, ckv, 0.0).astype(BF16)
    kp_out[:, KV_RANK:KV_RANK + half] = jnp.where(valid, kr1, 0.0).astype(BF16)
    kp_out[:, KV_RANK + half:] = jnp.where(valid, kr2, 0.0).astype(BF16)
    cqn_out[...] = cqb
    of_out[:, 0:half] = (o1 * ATTN_SCALE).astype(BF16)
    of_out[:, half:] = (o2 * ATTN_SCALE).astype(BF16)


def _mla_prep(z3, cos, sin, mw, *, tm):
    nb, t_len, _ = z3.shape
    nt = -(-t_len // tm)
    tp = nt * tm
    wspecs = [pl.BlockSpec(w.shape, lambda b, j: (0,) * w.ndim) for w in mw]
    blk = lambda width, col: pl.BlockSpec((None, tm, width), lambda b, j: (b, j, col))
    tab = pl.BlockSpec((tm, 128), lambda b, j: (j, 0))
    return pl.pallas_call(
        functools.partial(_mla_prep_kernel, t_valid=t_len),
        grid=(nb, nt),
        in_specs=[blk(256, 4), blk(256, 5), blk(128, 16), blk(128, 17), tab, tab] + wspecs,
        out_specs=[blk(256, 0), blk(256, 0), blk(512, 0), blk(256, 0), blk(256, 0)],
        out_shape=[jax.ShapeDtypeStruct((nb, t_len, KV_RANK), F32),
                   jax.ShapeDtypeStruct((nb, t_len, 256), F32),
                   jax.ShapeDtypeStruct((nb, tp, 512), BF16),
                   jax.ShapeDtypeStruct((nb, tp, Q_RANK), BF16),
                   jax.ShapeDtypeStruct((nb, tp, 256), BF16)],
        compiler_params=_cparams("parallel", "parallel"), name="mla_prep",
    )(z3, z3, z3, z3, cos, sin, *mw)


def _fold_kernel(wq_ref, wk_ref, o_ref):
    o_ref[...] = _dot_hi(wq_ref[...], wk_ref[...]).astype(BF16)


def _fold_q(wq_nope, wuk_t):
    return pl.pallas_call(
        _fold_kernel,
        grid=(H_C,),
        in_specs=[pl.BlockSpec((None, Q_RANK, NOPE_DIM), lambda h: (h, 0, 0)),
                  pl.BlockSpec((None, NOPE_DIM, KV_RANK), lambda h: (h, 0, 0))],
        out_specs=pl.BlockSpec((Q_RANK, KV_RANK), lambda h: (0, h)),
        out_shape=jax.ShapeDtypeStruct((Q_RANK, H_C * KV_RANK), BF16),
        compiler_params=_cparams("parallel"), name="fold_q",
    )(wq_nope, wuk_t)


def _build_q(cqn, of, wf, hm, tq):
    q_all = _dot(cqn, wf) * ATTN_SCALE
    lat = [q_all[:, h * KV_RANK:(h + 1) * KV_RANK].astype(BF16) for h in range(H_C)]
    rope = [of * hm[h:h + 1].astype(BF16) for h in range(H_C)]
    return jnp.concatenate(lat, axis=0), jnp.concatenate(rope, axis=0)


def _project_out(acc, l, wuv_ref, tq):
    o = acc / l
    y = _dot(o[0:tq].astype(BF16), wuv_ref[0])
    for h in range(1, H_C):
        y = y + _dot(o[h * tq:(h + 1) * tq].astype(BF16), wuv_ref[h])
    return y


def _attn_prompt_kernel(cqn_ref, of_ref, kp_ref, wf_ref, hm_ref, wuv_ref, y_ref,
                        ql_scr, qr_scr, m_scr, l_scr, acc_scr):
    tq = cqn_ref.shape[0]
    tk = kp_ref.shape[0]
    qi = pl.program_id(1)
    ki = pl.program_id(2)

    @pl.when(ki == 0)
    def _():
        lat, rope = _build_q(cqn_ref[...], of_ref[...], wf_ref[...], hm_ref[...], tq)
        ql_scr[...] = lat
        qr_scr[...] = rope
        m_scr[...] = jnp.full(m_scr.shape, NEG_INF, F32)
        l_scr[...] = jnp.zeros(l_scr.shape, F32)
        acc_scr[...] = jnp.zeros(acc_scr.shape, F32)

    @pl.when(ki <= qi)
    def _():
        kp = kp_ref[...]
        ckv = kp[:, 0:KV_RANK]
        s = _dot_nt(ql_scr[...], ckv) + _dot_nt(qr_scr[...], kp[:, KV_RANK:])
        qpos = qi * tq + lax.broadcasted_iota(jnp.int32, s.shape, 0) % tq
        kpos = ki * tk + lax.broadcasted_iota(jnp.int32, s.shape, 1)
        s = jnp.where(kpos <= qpos, s, NEG_INF)
        m_old = m_scr[...]
        m_new = jnp.maximum(m_old, jnp.max(s, axis=-1, keepdims=True))
        alpha = jnp.exp(m_old - m_new)
        p = jnp.exp(s - m_new)
        l_scr[...] = alpha * l_scr[...] + jnp.sum(p, axis=-1, keepdims=True)
        acc_scr[...] = alpha * acc_scr[...] + _dot(p.astype(BF16), ckv)
        m_scr[...] = m_new

    @pl.when(ki == qi)
    def _():
        y_ref[...] = _project_out(acc_scr[...], l_scr[...], wuv_ref, tq)


def _attn_prompt(cqn, of, kp, wf, hm, wuv, t_len):
    nb, tp, _ = kp.shape
    tq = ATT_TILE
    nt = tp // tq
    rows = H_C * tq
    const = lambda w: pl.BlockSpec(w.shape, lambda b, i, j: (0,) * w.ndim)
    return pl.pallas_call(
        _attn_prompt_kernel,
        grid=(nb, nt, nt),
        in_specs=[pl.BlockSpec((None, tq, Q_RANK), lambda b, i, j: (b, i, 0)),
                  pl.BlockSpec((None, tq, 256), lambda b, i, j: (b, i, 0)),
                  pl.BlockSpec((None, tq, 512), lambda b, i, j: (b, jnp.minimum(i, j), 0)),
                  const(wf), const(hm), const(wuv)],
        out_specs=pl.BlockSpec((None, tq, W_C), lambda b, i, j: (b, i, 0)),
        out_shape=jax.ShapeDtypeStruct((nb, t_len, W_C), F32),
        scratch_shapes=[pltpu.VMEM((rows, KV_RANK), BF16), pltpu.VMEM((rows, 256), BF16),
                        pltpu.VMEM((rows, 1), F32), pltpu.VMEM((rows, 1), F32),
                        pltpu.VMEM((rows, KV_RANK), F32)],
        compiler_params=_cparams("parallel", "parallel", "arbitrary"), name="attn_prompt",
    )(cqn, of, kp, wf, hm, wuv)


def _attn_sample_kernel(pt_ref, *refs, n_pages):
    ckv_pages = refs[0:n_pages]
    kr_pages = refs[n_pages:2 * n_pages]
    (cqn_ref, of_ref, cnew_ref, knew_ref, wf_ref, hm_ref, tile_ref, wuv_ref, y_ref,
     ql_scr, qr_scr, m_scr, l_scr, acc_scr) = refs[2 * n_pages:]
    del pt_ref
    tq = cqn_ref.shape[0]
    g = pl.program_id(1)

    @pl.when(g == 0)
    def _():
        lat, rope = _build_q(cqn_ref[...], of_ref[...], wf_ref[...], hm_ref[...], tq)
        qr = _dot(rope, tile_ref[...]).astype(BF16)
        ql_scr[...] = lat
        qr_scr[...] = qr
        cnew = cnew_ref[...].astype(BF16)
        s = _dot_nt(lat, cnew) + _dot_nt(qr, knew_ref[...].astype(BF16))
        tpos = lax.broadcasted_iota(jnp.int32, s.shape, 0) % tq
        s = jnp.where(lax.broadcasted_iota(jnp.int32, s.shape, 1) <= tpos, s, NEG_INF)
        m = jnp.max(s, axis=-1, keepdims=True)
        p = jnp.exp(s - m)
        m_scr[...] = m
        l_scr[...] = jnp.sum(p, axis=-1, keepdims=True)
        acc_scr[...] = _dot(p.astype(BF16), cnew)

    ql = ql_scr[...]
    qr = qr_scr[...]
    pages = [r[...].astype(BF16) for r in ckv_pages]
    s = jnp.concatenate(
        [_dot_nt(ql, pages[i]) + _dot_nt(qr, kr_pages[i][...].astype(BF16)) for i in range(n_pages)], axis=1)
    m_old = m_scr[...]
    m_new = jnp.maximum(m_old, jnp.max(s, axis=-1, keepdims=True))
    alpha = jnp.exp(m_old - m_new)
    p = jnp.exp(s - m_new)
    l_scr[...] = alpha * l_scr[...] + jnp.sum(p, axis=-1, keepdims=True)
    p = p.astype(BF16)
    page = pages[0].shape[0]
    pv = _dot(p[:, 0:page], pages[0])
    for i in range(1, n_pages):
        pv = pv + _dot(p[:, i * page:(i + 1) * page], pages[i])
    acc_scr[...] = alpha * acc_scr[...] + pv
    m_scr[...] = m_new

    @pl.when(g == pl.num_programs(1) - 1)
    def _():
        y_ref[...] = _project_out(acc_scr[...], l_scr[...], wuv_ref, tq)


def _attn_sample(page_table, cache_ckv, cache_krope, layer, cqn, of, cnew, knew, wf, hm, tile_t, wuv):
    nb, tq, _ = cqn.shape
    n_pages_total = page_table.shape[1]
    page = cache_ckv.shape[2]
    npg = PAGES_PER_STEP
    steps = n_pages_total // npg
    rows = H_C * tq
    pt_flat = page_table.reshape(-1)

    def page_spec(width, i):
        return pl.BlockSpec((None, None, page, width),
                            lambda b, g, pt: (layer, pt[b * n_pages_total + g * npg + i], 0, 0))

    per_b = lambda width: pl.BlockSpec((None, tq, width), lambda b, g, pt: (b, 0, 0))
    const = lambda w: pl.BlockSpec(w.shape, lambda b, g, pt: (0,) * w.ndim)
    grid_spec = pltpu.PrefetchScalarGridSpec(
        num_scalar_prefetch=1,
        grid=(nb, steps),
        in_specs=[page_spec(KV_RANK, i) for i in range(npg)] + [page_spec(ROPE_DIM, i) for i in range(npg)]
        + [per_b(Q_RANK), per_b(256), per_b(KV_RANK), per_b(ROPE_DIM), const(wf), const(hm), const(tile_t),
           const(wuv)],
        out_specs=pl.BlockSpec((None, tq, W_C), lambda b, g, pt: (b, 0, 0)),
        scratch_shapes=[pltpu.VMEM((rows, KV_RANK), BF16), pltpu.VMEM((rows, ROPE_DIM), BF16),
                        pltpu.VMEM((rows, 1), F32), pltpu.VMEM((rows, 1), F32),
                        pltpu.VMEM((rows, KV_RANK), F32)],
    )
    return pl.pallas_call(
        functools.partial(_attn_sample_kernel, n_pages=npg),
        grid_spec=grid_spec,
        out_shape=jax.ShapeDtypeStruct((nb, tq, W_C), F32),
        compiler_params=_cparams("parallel", "arbitrary"), name="attn_sample",
    )(pt_flat, *([cache_ckv] * npg), *([cache_krope] * npg), cqn, of, cnew, knew, wf, hm, tile_t, wuv)


def _out_kernel(ya_ref, yb_ref, yc_ref, gc_ref, x_ref, wa_ref, wb_ref, wc_ref, g_ref, b_ref, o_ref):
    yc = (yc_ref[...] * _silu(gc_ref[...])).astype(BF16)
    out = _dot(ya_ref[...], wa_ref[...]) + _dot(yb_ref[...], wb_ref[...]) + _dot(yc, wc_ref[...])
    h = DN_ALPHA * x_ref[...] + out
    mu = jnp.mean(h, axis=-1, keepdims=True)
    d = h - mu
    var = jnp.mean(d * d, axis=-1, keepdims=True)
    o_ref[...] = d * lax.rsqrt(var + 1e-5) * g_ref[...] + b_ref[...]


def _out_proj(ya, yb, yc, z, x, ow, tm):
    n = x.shape[0]
    wspecs = [pl.BlockSpec(w.shape, lambda i: (0,) * w.ndim) for w in ow]
    rowblk = lambda width, col=0: pl.BlockSpec((tm, width), lambda i: (i, col))
    return pl.pallas_call(
        _out_kernel,
        grid=(n // tm,),
        in_specs=[rowblk(W_A), rowblk(W_B), rowblk(W_C), rowblk(W_C, 3), rowblk(D_MODEL)] + wspecs,
        out_specs=rowblk(D_MODEL),
        out_shape=jax.ShapeDtypeStruct((n, D_MODEL), F32),
        compiler_params=_cparams("parallel"), name="out_proj",
    )(ya, yb, yc, z, x, *ow)


def _block_diag(blocks):
    nblk, bw, _ = blocks.shape
    out = jnp.zeros((nblk * bw, nblk * bw), blocks.dtype)
    for h in range(nblk):
        out = out.at[h * bw:(h + 1) * bw, h * bw:(h + 1) * bw].set(blocks[h])
    return out


def _pad_to(x, shape):
    return jnp.pad(x, [(0, s - d) for d, s in zip(x.shape, shape)])


def _rope_tables(pos):
    inv = ROPE_THETA ** (-2.0 * jnp.arange(ROPE_HALF, dtype=F32) / ROPE_DIM)
    ang = pos.astype(F32)[:, None] * inv[None, :]
    return jnp.tile(jnp.cos(ang), (1, H_C)), jnp.tile(jnp.sin(ang), (1, H_C))


def _static_tables():
    lane = np.arange(256)
    hm = np.stack([((lane % 128) // ROPE_HALF == h) for h in range(H_C)]).astype(np.float32)
    hmb = np.stack([(lane // HD_B == h) for h in range(H_B)]).astype(np.float32)
    ones_bd = (lane[:, None] // HD_B == lane[None, :] // HD_B).astype(np.float32)
    tile_t = np.zeros((256, ROPE_DIM), np.float32)
    for ln in lane:
        tile_t[ln, (ln // 128) * ROPE_HALF + ln % ROPE_HALF] = 1.0
    return jnp.asarray(hm), jnp.asarray(hmb), jnp.asarray(ones_bd), jnp.asarray(tile_t, BF16)


def _state_to_bd(s):
    nb = s.shape[0]
    out = jnp.zeros((nb, W_B, W_B), F32)
    for h in range(H_B):
        out = out.at[:, h * HD_B:(h + 1) * HD_B, h * HD_B:(h + 1) * HD_B].set(s[:, h])
    return out


def _state_from_bd(s):
    return jnp.stack([s[:, h * HD_B:(h + 1) * HD_B, h * HD_B:(h + 1) * HD_B] for h in range(H_B)], axis=1)


def kernel(x_prompt, x_sample, cache_ckv, cache_krope, state_conv, state_lru, state_shift, state_wkv,
           page_table, meta_tokens, w_in, conv_w, conv_b, lru_wr, lru_br, lru_wi, lru_bi, lru_lambda,
           rk_mix, rk_wr, rk_wk, rk_wv, rk_w0, rk_w1, rk_w2, rk_a0, rk_a1, rk_a2, rk_kk, rk_ka, rk_rk,
           rk_gn_g, rk_gn_b, q_norm, kv_norm, w_uq, w_uk, w_uv, w_out, ln_g, ln_b):
    bp, seq, _ = x_prompt.shape
    bs, ts, _ = x_sample.shape
    tp_len = N_META + seq
    n_past = page_table.shape[1] * cache_ckv.shape[2]
    depth = w_in.shape[0]
    assert ts == 8 and CHUNK % ts == 0 and bs % (CHUNK // ts) == 0
    assert tp_len % 24 == 0

    hm, hmb, ones_bd, tile_t = _static_tables()
    row2 = lambda v: v.reshape(1, -1)

    xp = jnp.concatenate(
        [jnp.broadcast_to(meta_tokens[None].astype(x_prompt.dtype), (bp, N_META, D_MODEL)), x_prompt],
        axis=1).reshape(bp * tp_len, D_MODEL)
    xs = x_sample.reshape(bs * ts, D_MODEL)

    tm_p = tp_len // 3
    tm_s = bs * ts
    tp_pad = -(-tp_len // ATT_TILE) * ATT_TILE
    cos_p, sin_p = _rope_tables(jnp.arange(tp_pad))
    cos_s, sin_s = _rope_tables(n_past + jnp.arange(bs * ts) % ts)

    p_states, s_states = [], []
    for l in range(depth):
        wi = w_in[l]
        kr_cols = wi[:, 1536:1568]
        w_in_r = jnp.concatenate(
            [wi[:, 0:1536], wi[:, 1568:2080],
             jnp.tile(kr_cols[:, 0:ROPE_HALF], (1, H_C)), jnp.tile(kr_cols[:, ROPE_HALF:], (1, H_C))],
            axis=1).astype(BF16)
        lru_w = (conv_w[l], row2(conv_b[l]), _block_diag(lru_wr[l]).astype(BF16), row2(lru_br[l]),
                 _block_diag(lru_wi[l]).astype(BF16), row2(lru_bi[l]), row2(lru_lambda[l]))
        prep_w = (_pad_to(rk_mix[l], (8, W_B)), rk_wr[l].astype(BF16), rk_wk[l].astype(BF16),
                  rk_wv[l].astype(BF16), row2(rk_w0[l]), _pad_to(rk_w1[l], (W_B, 128)).astype(BF16),
                  _pad_to(rk_w2[l], (128, W_B)).astype(BF16), row2(rk_a0[l]),
                  _pad_to(rk_a1[l], (W_B, 128)).astype(BF16), _pad_to(rk_a2[l], (128, W_B)).astype(BF16),
                  row2(rk_kk[l]), row2(rk_ka[l]), ones_bd)
        wkv_w = (hmb, ones_bd, row2(rk_rk[l]), row2(rk_gn_g[l]), row2(rk_gn_b[l]))
        wq = w_uq[l].reshape(Q_RANK, H_C, NOPE_DIM + ROPE_DIM)
        wq_nope = jnp.transpose(wq[:, :, 0:NOPE_DIM], (1, 0, 2))
        wr1 = wq[:, :, NOPE_DIM:NOPE_DIM + ROPE_HALF].reshape(Q_RANK, 128).astype(BF16)
        wr2 = wq[:, :, NOPE_DIM + ROPE_HALF:].reshape(Q_RANK, 128).astype(BF16)
        wuk_t = jnp.transpose(w_uk[l], (1, 2, 0))
        wf = _fold_q(wq_nope, wuk_t)
        mla_w = (row2(q_norm[l]), row2(kv_norm[l]), wr1, wr2)
        wuv = jnp.zeros((H_C, KV_RANK, W_C), F32)
        for h in range(H_C):
            wuv = wuv.at[h, :, h * V_DIM:(h + 1) * V_DIM].set(w_uv[l][:, h, :])
        wuv = wuv.astype(BF16)
        wo = w_out[l].astype(BF16)
        out_w = (wo[0:W_A], wo[W_A:W_A + W_B], wo[W_A + W_B:], row2(ln_g[l]), row2(ln_b[l]))

        z = _in_proj(xp, w_in_r, tm_p)
        z3 = z.reshape(bp, tp_len, Z_COLS)
        ya, lru_new = _rglru(z, lru_w, nseq=bp, seq_len=tp_len, pos0=0)
        prep = _rwkv_prep(z, prep_w, seq_len=tp_len, tm=tm_p)
        prep3 = tuple(a.reshape(bp, tp_len, W_B) for a in prep)
        yb, wkv_bd = _wkv(z3, prep3, jnp.zeros((bp, W_B, W_B), F32), wkv_w,
                          t_valid=tp_len, nseq=1, sub_len=CHUNK)
        ckv, krt, kp, cqn, of = _mla_prep(z3, cos_p, sin_p, mla_w, tm=ATT_TILE)
        yc = _attn_prompt(cqn, of, kp, wf, hm, wuv, tp_len)
        x_new = _out_proj(ya, yb.reshape(bp * tp_len, W_B), yc.reshape(bp * tp_len, W_C), z, xp, out_w, tm_p)
        kr = jnp.concatenate([krt[..., 0:ROPE_HALF], krt[..., 128:128 + ROPE_HALF]], axis=-1)
        p_states.append((ckv, kr, z3[:, tp_len - (CONV_W - 1):, 0:W_A], lru_new.reshape(bp, W_A),
                         z3[:, tp_len - 1, 2 * W_A:2 * W_A + W_B], _state_from_bd(wkv_bd)))
        xp = x_new

        z = _in_proj(xs, w_in_r, tm_s)
        z3 = z.reshape(bs, ts, Z_COLS)
        conv_ext = _pad_to(state_conv[l], (bs, ts, W_A)).reshape(bs * ts, W_A)
        h_ext = jnp.repeat(state_lru[l], ts, axis=0)
        ya, h_all = _rglru(z, lru_w, nseq=bs, seq_len=ts, pos0=n_past, conv_ext=conv_ext, h_ext=h_ext)
        prep = _rwkv_prep(z, prep_w, seq_len=ts, tm=tm_s, prev_ext=jnp.repeat(state_shift[l], ts, axis=0))
        per_chunk = CHUNK // ts
        groups = bs // per_chunk
        prep3 = tuple(a.reshape(groups, CHUNK, W_B) for a in prep)
        yb, wkv_bd = _wkv(z.reshape(groups, CHUNK, Z_COLS), prep3, _state_to_bd(state_wkv[l]), wkv_w,
                          t_valid=CHUNK, nseq=per_chunk, sub_len=ts)
        ckv, krt, _, cqn, of = _mla_prep(z.reshape(1, bs * ts, Z_COLS), cos_s, sin_s, mla_w, tm=ATT_TILE)
        ckv = ckv.reshape(bs, ts, KV_RANK)
        krt = krt.reshape(bs, ts, 256)
        kr = jnp.concatenate([krt[..., 0:ROPE_HALF], krt[..., 128:128 + ROPE_HALF]], axis=-1)
        yc = _attn_sample(page_table, cache_ckv, cache_krope, l, cqn.reshape(bs, ts, Q_RANK),
                          of.reshape(bs, ts, 256), ckv, kr, wf, hm, tile_t, wuv)
        x_new = _out_proj(ya, yb.reshape(bs * ts, W_B), yc.reshape(bs * ts, W_C), z, xs, out_w, tm_s)
        s_states.append((ckv, kr, z3[:, ts - (CONV_W - 1):, 0:W_A], h_all.reshape(bs, ts, W_A)[:, ts - 1],
                         z3[:, ts - 1, 2 * W_A:2 * W_A + W_B], _state_from_bd(wkv_bd)))
        xs = x_new

    y_prompt = xp.reshape(bp, tp_len, D_MODEL)[:, N_META:]
    y_sample = xs.reshape(bs, ts, D_MODEL)
    p_out = [jnp.stack([st[i] for st in p_states]) for i in range(6)]
    s_out = [jnp.stack([st[i] for st in s_states]) for i in range(6)]
    return (y_prompt, y_sample, *p_out, *s_out)
```

```python
import functools
import itertools

import numpy as np
import jax
import jax.numpy as jnp
from jax import lax
from jax.experimental import pallas as pl
from jax.experimental.pallas import tpu as pltpu

F32 = jnp.float32
BF16 = jnp.bfloat16
HIGHEST = lax.Precision.HIGHEST

D_MODEL = 1024
N_META = 16
W_A = 256
W_B = 256
W_C = 512
CONV_W = 4
LRU_C = 8.0
HD_B = 64
H_B = 4
GN_EPS = 64e-5
H_C = 8
V_DIM = 64
NOPE_DIM = 64
ROPE_DIM = 32
ROPE_HALF = 16
Q_RANK = 256
KV_RANK = 256
ROPE_THETA = 10000.0
ATTN_SCALE = (NOPE_DIM + ROPE_DIM) ** -0.5
LOG2_E = 1.4426950408889634
Q_SCALE = ATTN_SCALE * LOG2_E
NEG_INF = -1e30
DEPTH = 2
DN_ALPHA = (2 * DEPTH) ** 0.25

LANES = 128
Z_COLS = 2304
CHUNK = 64
WKV_BATCH = 4
ATT_TILE = 256
ATT_SKEW = 2

VMEM_LIMIT = 48 * 1024 * 1024


def _cparams(*sem):
    return pltpu.CompilerParams(dimension_semantics=sem, vmem_limit_bytes=VMEM_LIMIT)


def _dot(a, b):
    return jnp.dot(a, b, preferred_element_type=F32)


def _dot_nt(a, b):
    return lax.dot_general(a, b, (((1,), (1,)), ((), ())), preferred_element_type=F32)


def _dot_tn(a, b):
    return lax.dot_general(a, b, (((0,), (0,)), ((), ())), preferred_element_type=F32)


def _dot_hi(a, b):
    return jnp.dot(a, b, preferred_element_type=F32, precision=HIGHEST)


def _split(x):
    hi = x.astype(BF16)
    return hi, (x - hi.astype(F32)).astype(BF16)


def _dot3(a, b):
    a_hi, a_lo = _split(a)
    b_hi, b_lo = _split(b)
    return _dot(a_hi, b_hi) + (_dot(a_hi, b_lo) + _dot(a_lo, b_hi))


def _softplus(x):
    return jnp.maximum(x, 0.0) + jnp.log1p(jnp.exp(-jnp.abs(x)))


def _silu(x):
    return x * jax.nn.sigmoid(x)


def _lane_tiles(x):
    return [x[:, j * LANES:(j + 1) * LANES] for j in range(x.shape[1] // LANES)]


def _mm_kernel(x_ref, w_ref, o_ref):
    o_ref[...] = _dot(x_ref[...].astype(BF16), w_ref[...])


def _in_proj(x, w, tm):
    n, k = x.shape
    m = w.shape[1]
    return pl.pallas_call(
        _mm_kernel,
        grid=(n // tm,),
        in_specs=[pl.BlockSpec((tm, k), lambda i: (i, 0)), pl.BlockSpec((k, m), lambda i: (0, 0))],
        out_specs=pl.BlockSpec((tm, m), lambda i: (i, 0)),
        out_shape=jax.ShapeDtypeStruct((n, m), F32),
        compiler_params=_cparams("parallel"),
        name="in_proj",
    )(x, w)


def _rglru_kernel(*refs, seq_len, pos0, has_state):
    if has_state:
        (u_ref, g_ref, f_ref, hp_ref, cw_ref, cb_ref, wr_ref, br_ref, wi_ref, bi_ref, lam_ref,
         y_ref, h_ref) = refs
    else:
        (u_ref, g_ref, cw_ref, cb_ref, wr_ref, br_ref, wi_ref, bi_ref, lam_ref,
         y_ref, h_ref, a_scr, b_scr) = refs
    u = u_ref[...]
    rows = u.shape[0]
    row = lax.broadcasted_iota(jnp.int32, u.shape, 0)
    t = row if rows == seq_len else row & (seq_len - 1)

    cw = cw_ref[...]
    xc = cb_ref[...] + u * cw[CONV_W - 1:CONV_W]
    for k in range(1, CONV_W):
        sh = pltpu.roll(u, k, 0)
        if has_state:
            f = f_ref[...]
            prev = f if k == CONV_W - 1 else pltpu.roll(f, rows - (CONV_W - 1 - k), 0)
        else:
            prev = 0.0
        sh = jnp.where(t < k, prev, sh)
        xc = xc + sh * cw[CONV_W - 1 - k:CONV_W - k]

    xcb = xc.astype(BF16)
    gate_r = jax.nn.sigmoid(_dot(xcb, wr_ref[...]) + br_ref[...])
    gate_i = jax.nn.sigmoid(_dot(xcb, wi_ref[...]) + bi_ref[...])
    log_a = (-LRU_C) * gate_r * _softplus(-lam_ref[...])
    th = jnp.tanh(log_a)
    neg_expm1 = -2.0 * th / (1.0 - th)
    reset = (t + pos0) == 0
    a = jnp.where(reset, 0.0, jnp.exp(log_a))
    mult = jnp.where(reset, 1.0, jnp.sqrt(neg_expm1))
    bv = xc * gate_i * mult

    t8 = row & 7
    for s in (1, 2, 4):
        a_s = jnp.where(t8 >= s, pltpu.roll(a, s, 0), 1.0)
        b_s = jnp.where(t8 >= s, pltpu.roll(bv, s, 0), 0.0)
        bv = a * b_s + bv
        a = a * a_s

    if has_state:
        h = bv + a * hp_ref[...]
        h_ref[...] = h
    else:
        a_scr[...] = a
        b_scr[...] = bv

        def body(i, carry):
            r0 = pl.multiple_of(i * 8, 8)
            hh = b_scr[pl.ds(r0, 8), :] + a_scr[pl.ds(r0, 8), :] * carry
            b_scr[pl.ds(r0, 8), :] = hh
            return jnp.broadcast_to(hh[7:8, :], hh.shape)

        carry = lax.fori_loop(0, rows // 8, body, jnp.zeros((8, u.shape[1]), F32))
        h = b_scr[...]
        h_ref[...] = carry[0:1, :]
    y_ref[...] = (h * _silu(g_ref[...])).astype(BF16)


def _rglru(z, lw, *, nseq, seq_len, pos0, conv_ext=None, h_ext=None):
    n = z.shape[0]
    wspecs = [pl.BlockSpec(w.shape, lambda i: (0,) * w.ndim) for w in lw]
    if conv_ext is None:
        rows = seq_len
        in_specs = [pl.BlockSpec((rows, W_A), lambda i: (i, 0)),
                    pl.BlockSpec((rows, W_A), lambda i: (i, 1))] + wspecs
        out_specs = [pl.BlockSpec((rows, W_A), lambda i: (i, 0)),
                     pl.BlockSpec((None, 1, W_A), lambda i: (i, 0, 0))]
        out_shape = [jax.ShapeDtypeStruct((n, W_A), BF16), jax.ShapeDtypeStruct((nseq, 1, W_A), F32)]
        scratch = [pltpu.VMEM((rows, W_A), F32), pltpu.VMEM((rows, W_A), F32)]
        args = (z, z) + tuple(lw)
        grid = (nseq,)
    else:
        rows = n
        full = pl.BlockSpec((rows, W_A), lambda i: (0, 0))
        in_specs = [full, pl.BlockSpec((rows, W_A), lambda i: (0, 1)), full, full] + wspecs
        out_specs = [full, full]
        out_shape = [jax.ShapeDtypeStruct((n, W_A), BF16), jax.ShapeDtypeStruct((n, W_A), F32)]
        scratch = []
        args = (z, z, conv_ext, h_ext) + tuple(lw)
        grid = (1,)
    return pl.pallas_call(
        functools.partial(_rglru_kernel, seq_len=seq_len, pos0=pos0, has_state=conv_ext is not None),
        grid=grid, in_specs=in_specs, out_specs=out_specs, out_shape=out_shape,
        scratch_shapes=scratch, compiler_params=_cparams("parallel"), name="rglru",
    )(*args)


def _rglru_weights(p):
    row2 = lambda v: v.reshape(1, -1)
    return (p['conv_w'], row2(p['conv_b']), _block_diag(p['lru_wr']).astype(BF16), row2(p['lru_br']),
            _block_diag(p['lru_wi']).astype(BF16), row2(p['lru_bi']), row2(p['lru_lambda']))


def _rglru_branch_prompt(z3, lru_w):
    nb, t_len, _ = z3.shape
    y, h_last = _rglru(z3.reshape(nb * t_len, Z_COLS), lru_w, nseq=nb, seq_len=t_len, pos0=0)
    return y, h_last.reshape(nb, W_A)


def _rglru_branch_sample(z3, conv_prev, h_prev, pos0, lru_w):
    nb, ts, _ = z3.shape
    conv_ext = _pad_to(conv_prev, (nb, ts, W_A)).reshape(nb * ts, W_A)
    h_ext = jnp.repeat(h_prev, ts, axis=0)
    y, h_all = _rglru(z3.reshape(nb * ts, Z_COLS), lru_w, nseq=nb, seq_len=ts, pos0=pos0,
                      conv_ext=conv_ext, h_ext=h_ext)
    return y, h_all.reshape(nb, ts, W_A)[:, ts - 1]


def _rwkv_prep_kernel(u_ref, p_ref, mix_ref, wr_ref, wk_ref, wv_ref, w0_ref, w1_ref, w2_ref,
                      a0_ref, a1_ref, a2_ref, kkw_ref, ka_ref, ones_ref,
                      r_ref, lw_ref, k_ref, v_ref, kk_ref, a_ref, *, seq_len, tiles_per_seq):
    u = u_ref[...]
    row = lax.broadcasted_iota(jnp.int32, u.shape, 0)
    rolled = pltpu.roll(u, 1, 0)
    if tiles_per_seq is None:
        u_prev = jnp.where((row & (seq_len - 1)) == 0, p_ref[...], rolled)
    else:
        first = (pl.program_id(0) % tiles_per_seq) == 0
        prow = jnp.where(first, 0.0, p_ref[7:8, :])
        u_prev = jnp.where(row == 0, prow, rolled)
    du = u_prev - u
    mix = mix_ref[...]
    xr = (u + du * mix[0:1]).astype(BF16)
    xw = (u + du * mix[1:2]).astype(BF16)
    xk = (u + du * mix[2:3]).astype(BF16)
    xv = (u + du * mix[3:4]).astype(BF16)
    xa = (u + du * mix[4:5]).astype(BF16)
    r = _dot(xr, wr_ref[...])
    k = _dot(xk, wk_ref[...])
    v = _dot(xv, wv_ref[...])
    lora_w = _dot(jnp.tanh(_dot(xw, w1_ref[...])).astype(BF16), w2_ref[...])
    w_log = -_softplus(-(w0_ref[...] + lora_w)) - 0.5
    a = jax.nn.sigmoid(a0_ref[...] + _dot(_dot(xa, a1_ref[...]).astype(BF16), a2_ref[...]))
    kk = k * kkw_ref[...]
    ssq = _dot3(kk * kk, ones_ref[...])
    kk = kk / jnp.maximum(jnp.sqrt(ssq), 1e-12)
    r_ref[...] = r
    lw_ref[...] = -jnp.exp(w_log)
    k_ref[...] = k * (1.0 + (a - 1.0) * ka_ref[...])
    v_ref[...] = v
    kk_ref[...] = kk
    a_ref[...] = a


def _rwkv_prep(z, pw, *, seq_len, tm, prev_ext=None):
    n = z.shape[0]
    wspecs = [pl.BlockSpec(w.shape, lambda i: (0,) * w.ndim) for w in pw]
    if prev_ext is None:
        tiles = seq_len // tm
        p_arg = z
        p_spec = pl.BlockSpec((8, W_B), lambda i: (jnp.maximum(i * (tm // 8) - 1, 0), 2))
    else:
        tiles = None
        p_arg = prev_ext
        p_spec = pl.BlockSpec((tm, W_B), lambda i: (i, 0))
    out = jax.ShapeDtypeStruct((n, W_B), F32)
    ospec = pl.BlockSpec((tm, W_B), lambda i: (i, 0))
    return pl.pallas_call(
        functools.partial(_rwkv_prep_kernel, seq_len=seq_len, tiles_per_seq=tiles),
        grid=(n // tm,),
        in_specs=[pl.BlockSpec((tm, W_B), lambda i: (i, 2)), p_spec] + wspecs,
        out_specs=[ospec] * 6, out_shape=[out] * 6,
        compiler_params=_cparams("parallel"), name="rwkv_prep",
    )(z, p_arg, *pw)


def _wkv_kernel(r_ref, lw_ref, k_ref, v_ref, kk_ref, a_ref, g_ref, s0_ref, hm_ref, ones_ref,
                rk_ref, gng_ref, gnb_ref, y_ref, s_ref, *, t_valid, nbatch, nseq, sub_len):
    c = pl.program_id(1)

    @pl.when(c == 0)
    def _():
        s_ref[...] = s0_ref[...]

    chains = [_wkv_chunk(c, j, r_ref, lw_ref, k_ref, v_ref, kk_ref, a_ref, g_ref, hm_ref, ones_ref,
                         rk_ref, gng_ref, gnb_ref, y_ref, s_ref, t_valid=t_valid, nseq=nseq, sub_len=sub_len)
              for j in range(nbatch)]
    for _ in itertools.zip_longest(*chains):
        pass


def _wkv_chunk(c, j, r_ref, lw_ref, k_ref, v_ref, kk_ref, a_ref, g_ref, hm_ref, ones_ref,
               rk_ref, gng_ref, gnb_ref, y_ref, s_ref, *, t_valid, nseq, sub_len):
    width = W_B
    nh = H_B
    s_base = j * nseq
    row = lax.broadcasted_iota(jnp.int32, (CHUNK, width), 0)
    valid = (c * CHUNK + row) < t_valid
    zero = jnp.zeros((CHUNK, width), F32)
    r = jnp.where(valid, r_ref[j], zero)
    lw = jnp.where(valid, lw_ref[j], zero)
    k = jnp.where(valid, k_ref[j], zero)
    v = jnp.where(valid, v_ref[j], zero)
    kk = jnp.where(valid, kk_ref[j], zero)
    a = jnp.where(valid, a_ref[j], zero)

    qi = lax.broadcasted_iota(jnp.int32, (CHUNK, CHUNK), 0)
    si = lax.broadcasted_iota(jnp.int32, (CHUNK, CHUNK), 1)
    same = (qi // sub_len) == (si // sub_len)
    cum = _dot_hi(jnp.where(same & (qi >= si), 1.0, 0.0), lw)
    tot = _dot_hi(jnp.where(same, 1.0, 0.0), lw)
    yield
    e_neg = jnp.exp(-cum)
    e_tail = jnp.exp(tot - cum)
    rg = r * jnp.exp(cum)
    ag = -kk * jnp.exp(cum - lw)
    kka = kk * a
    bg = kka * e_neg
    kg = k * e_neg
    bt = kka * e_tail
    kt = k * e_tail

    hm = hm_ref[...]

    def stack(x):
        return jnp.concatenate([x * hm[h:h + 1] for h in range(nh)], axis=0)

    def unstack(x):
        out = x[0:CHUNK]
        for h in range(1, nh):
            out = out + x[h * CHUNK:(h + 1) * CHUNK]
        return out

    a_st = stack(ag).astype(BF16)
    r_st = stack(rg)
    b_st = stack(bg).astype(BF16)
    k_st = stack(kg).astype(BF16)
    v_st = stack(v)

    n_st = nh * CHUNK
    rq = lax.broadcasted_iota(jnp.int32, (n_st, n_st), 0) % CHUNK
    cs = lax.broadcasted_iota(jnp.int32, (n_st, n_st), 1) % CHUNK
    same_st = (rq // sub_len) == (cs // sub_len)
    strict = same_st & (rq > cs)
    incl = same_st & (rq >= cs)

    l_ab = jnp.where(strict, _dot_nt(a_st, b_st), 0.0)
    l_ak = jnp.where(strict, _dot_nt(a_st, k_st), 0.0)
    yield
    p_rb = jnp.where(incl, _dot_nt(r_st.astype(BF16), b_st), 0.0)
    p_rk = jnp.where(incl, _dot_nt(r_st.astype(BF16), k_st), 0.0)
    yield

    eye = jnp.where(lax.broadcasted_iota(jnp.int32, (n_st, n_st), 0)
                    == lax.broadcasted_iota(jnp.int32, (n_st, n_st), 1), 1.0, 0.0)
    tinv = eye + l_ab
    lp = l_ab
    levels = int(np.log2(sub_len))
    for _ in range(levels - 1):
        lpb = lp.astype(BF16)
        lp = _dot(lpb, lpb)
        yield
        tinv = tinv + _dot(lp.astype(BF16), tinv.astype(BF16))
        yield

    x0_parts, r0_parts = [], []
    for s in range(nseq):
        st = s_ref[s_base + s].astype(BF16)
        sl = slice(s * sub_len, (s + 1) * sub_len)
        rows_s = jnp.concatenate([ag[sl], rg[sl]], axis=0).astype(BF16)
        xs = _dot_nt(rows_s, st)
        x0_parts.append(xs[0:sub_len])
        r0_parts.append(xs[sub_len:2 * sub_len])
    x0 = x0_parts[0] if nseq == 1 else jnp.concatenate(x0_parts, axis=0)
    r0 = r0_parts[0] if nseq == 1 else jnp.concatenate(r0_parts, axis=0)
    yield

    x_st = stack(x0) + _dot(l_ak.astype(BF16), v_st.astype(BF16))
    yield
    u_st = _dot(tinv.astype(BF16), x_st.astype(BF16))
    yield
    o_st = stack(r0) + _dot(p_rb.astype(BF16), u_st.astype(BF16)) + _dot(p_rk.astype(BF16), v_st.astype(BF16))
    u_all = unstack(u_st)
    o = unstack(o_st)
    yield

    bd = (lax.broadcasted_iota(jnp.int32, (width, width), 0) // HD_B
          == lax.broadcasted_iota(jnp.int32, (width, width), 1) // HD_B)
    for s in range(nseq):
        sl = slice(s * sub_len, (s + 1) * sub_len)
        upd = (_dot_tn(u_all[sl].astype(BF16), bt[sl].astype(BF16))
               + _dot_tn(v[sl].astype(BF16), kt[sl].astype(BF16)))
        gam = jnp.exp(tot[s * sub_len:s * sub_len + 1])
        s_ref[s_base + s] = s_ref[s_base + s] * gam + jnp.where(bd, upd, 0.0)

    ones = ones_ref[...]
    mu = _dot3(o, ones) * (1.0 / HD_B)
    d = o - mu
    var = _dot3(d * d, ones) * (1.0 / HD_B)
    o = d * lax.rsqrt(var + GN_EPS) * gng_ref[...] + gnb_ref[...]
    o = o + _dot3(r * k * rk_ref[...], ones) * v
    y_ref[j] = (o * _silu(g_ref[j])).astype(BF16)


def _wkv(z3, prep, s0, cw, *, t_valid, nbatch, nseq, sub_len):
    nb, t_len, _ = z3.shape
    nc = -(-t_len // CHUNK)
    seq = pl.BlockSpec((nbatch, CHUNK, W_B), lambda b, c: (b, c, 0))
    wspecs = [pl.BlockSpec(w.shape, lambda b, c: (0,) * w.ndim) for w in cw]
    sspec = pl.BlockSpec((nbatch * nseq, W_B, W_B), lambda b, c: (b, 0, 0))
    return pl.pallas_call(
        functools.partial(_wkv_kernel, t_valid=t_valid, nbatch=nbatch, nseq=nseq, sub_len=sub_len),
        grid=(nb // nbatch, nc),
        in_specs=[seq] * 6 + [pl.BlockSpec((nbatch, CHUNK, W_B), lambda b, c: (b, c, 3)), sspec] + wspecs,
        out_specs=[seq, sspec],
        out_shape=[jax.ShapeDtypeStruct((nb, t_len, W_B), BF16),
                   jax.ShapeDtypeStruct(s0.shape, F32)],
        compiler_params=_cparams("parallel", "arbitrary"), name="wkv",
    )(*prep, z3, s0, *cw)


def _rwkv_weights(p, hmb, ones_bd):
    row2 = lambda v: v.reshape(1, -1)
    prep_w = (_pad_to(p['rk_mix'], (8, W_B)), p['rk_wr'].astype(BF16), p['rk_wk'].astype(BF16),
              p['rk_wv'].astype(BF16), row2(p['rk_w0']), _pad_to(p['rk_w1'], (W_B, LANES)).astype(BF16),
              _pad_to(p['rk_w2'], (LANES, W_B)).astype(BF16), row2(p['rk_a0']),
              _pad_to(p['rk_a1'], (W_B, LANES)).astype(BF16), _pad_to(p['rk_a2'], (LANES, W_B)).astype(BF16),
              row2(p['rk_kk']), row2(p['rk_ka']), ones_bd)
    wkv_w = (hmb, ones_bd, row2(p['rk_rk']), row2(p['rk_gn_g']), row2(p['rk_gn_b']))
    return prep_w, wkv_w


def _state_to_bd(s):
    nb = s.shape[0]
    eye = jnp.eye(H_B, dtype=s.dtype)
    return jnp.einsum('bhij,hg->bhigj', s, eye).reshape(nb, W_B, W_B)


def _state_from_bd(s):
    return jnp.stack([s[:, h * HD_B:(h + 1) * HD_B, h * HD_B:(h + 1) * HD_B] for h in range(H_B)], axis=1)


def _rwkv_branch_prompt(z3, prep_w, wkv_w):
    nb, t_len, _ = z3.shape
    prep = _rwkv_prep(z3.reshape(nb * t_len, Z_COLS), prep_w, seq_len=t_len, tm=t_len // 3)
    prep3 = tuple(a.reshape(nb, t_len, W_B) for a in prep)
    y, s_bd = _wkv(z3, prep3, jnp.zeros((nb, W_B, W_B), F32), wkv_w, t_valid=t_len,
                   nbatch=min(nb, WKV_BATCH), nseq=1, sub_len=CHUNK)
    return y, _state_from_bd(s_bd)


def _rwkv_branch_sample(z3, shift_prev, s_prev, prep_w, wkv_w):
    nb, ts, _ = z3.shape
    z = z3.reshape(nb * ts, Z_COLS)
    prep = _rwkv_prep(z, prep_w, seq_len=ts, tm=nb * ts, prev_ext=jnp.repeat(shift_prev, ts, axis=0))
    per_chunk = CHUNK // ts
    groups = nb // per_chunk
    prep3 = tuple(a.reshape(groups, CHUNK, W_B) for a in prep)
    y, s_bd = _wkv(z.reshape(groups, CHUNK, Z_COLS), prep3, _state_to_bd(s_prev), wkv_w,
                   t_valid=CHUNK, nbatch=1, nseq=per_chunk, sub_len=ts)
    return y.reshape(nb, ts, W_B), _state_from_bd(s_bd)


def _mla_prep_kernel(cq_ref, ckv_ref, za_ref, zb_ref, cos_ref, sin_ref, qn_ref, kvn_ref, wr1_ref, wr2_ref,
                     ckv_out, krt_out, kp_out, cqn_out, of_out, *, t_valid):
    tm = cq_ref.shape[0]
    row = lax.broadcasted_iota(jnp.int32, (tm, 1), 0)
    valid = (pl.program_id(1) * tm + row) < t_valid
    cq = cq_ref[...]
    cqn = cq * lax.rsqrt(jnp.mean(cq * cq, axis=-1, keepdims=True) + 1e-6) * qn_ref[...]
    cv = ckv_ref[...]
    ckv = cv * lax.rsqrt(jnp.mean(cv * cv, axis=-1, keepdims=True) + 1e-6) * kvn_ref[...]
    cos = cos_ref[...]
    sin = sin_ref[...]
    za = za_ref[...]
    zb = zb_ref[...]
    kr1 = za * cos - zb * sin
    kr2 = za * sin + zb * cos
    cqb = cqn.astype(BF16)
    q1 = _dot(cqb, wr1_ref[...])
    q2 = _dot(cqb, wr2_ref[...])
    o1 = q1 * cos - q2 * sin
    o2 = q1 * sin + q2 * cos
    ckv_out[...] = ckv
    krt_out[:, 0:LANES] = kr1
    krt_out[:, LANES:] = kr2
    kp_out[:, 0:KV_RANK] = jnp.where(valid, ckv, 0.0).astype(BF16)
    kp_out[:, KV_RANK:KV_RANK + LANES] = jnp.where(valid, kr1, 0.0).astype(BF16)
    kp_out[:, KV_RANK + LANES:] = jnp.where(valid, kr2, 0.0).astype(BF16)
    cqn_out[...] = cqb
    of_out[:, 0:LANES] = (o1 * Q_SCALE).astype(BF16)
    of_out[:, LANES:] = (o2 * Q_SCALE).astype(BF16)


def _mla_prep(z3, cos, sin, mw, *, tm):
    nb, t_len, _ = z3.shape
    nt = -(-t_len // tm)
    tp = nt * tm
    wspecs = [pl.BlockSpec(w.shape, lambda b, j: (0,) * w.ndim) for w in mw]
    blk = lambda width, col: pl.BlockSpec((None, tm, width), lambda b, j: (b, j, col))
    tab = pl.BlockSpec((tm, LANES), lambda b, j: (j, 0))
    return pl.pallas_call(
        functools.partial(_mla_prep_kernel, t_valid=t_len),
        grid=(nb, nt),
        in_specs=[blk(256, 4), blk(256, 5), blk(LANES, 16), blk(LANES, 17), tab, tab] + wspecs,
        out_specs=[blk(256, 0), blk(256, 0), blk(512, 0), blk(256, 0), blk(256, 0)],
        out_shape=[jax.ShapeDtypeStruct((nb, t_len, KV_RANK), F32),
                   jax.ShapeDtypeStruct((nb, t_len, 256), F32),
                   jax.ShapeDtypeStruct((nb, tp, 512), BF16),
                   jax.ShapeDtypeStruct((nb, tp, Q_RANK), BF16),
                   jax.ShapeDtypeStruct((nb, tp, 256), BF16)],
        compiler_params=_cparams("parallel", "parallel"), name="mla_prep",
    )(z3, z3, z3, z3, cos, sin, *mw)


def _fold_kernel(wq_ref, wk_ref, o_ref):
    o_ref[...] = _dot_hi(wq_ref[...], wk_ref[...]).astype(BF16)


def _fold_q(wq_nope, wuk_t):
    return pl.pallas_call(
        _fold_kernel,
        grid=(H_C,),
        in_specs=[pl.BlockSpec((None, Q_RANK, NOPE_DIM), lambda h: (h, 0, 0)),
                  pl.BlockSpec((None, NOPE_DIM, KV_RANK), lambda h: (h, 0, 0))],
        out_specs=pl.BlockSpec((Q_RANK, KV_RANK), lambda h: (0, h)),
        out_shape=jax.ShapeDtypeStruct((Q_RANK, H_C * KV_RANK), BF16),
        compiler_params=_cparams("parallel"), name="fold_q",
    )(wq_nope, wuk_t)


def _mla_weights(p):
    row2 = lambda v: v.reshape(1, -1)
    wq = p['w_uq'].reshape(Q_RANK, H_C, NOPE_DIM + ROPE_DIM)
    wq_nope = jnp.transpose(wq[:, :, 0:NOPE_DIM], (1, 0, 2))
    wr1 = wq[:, :, NOPE_DIM:NOPE_DIM + ROPE_HALF].reshape(Q_RANK, LANES).astype(BF16)
    wr2 = wq[:, :, NOPE_DIM + ROPE_HALF:].reshape(Q_RANK, LANES).astype(BF16)
    wuk_t = jnp.transpose(p['w_uk'], (1, 2, 0))
    wf = _fold_q(wq_nope, wuk_t)
    mla_w = (row2(p['q_norm']), row2(p['kv_norm']), wr1, wr2)
    eye = jnp.eye(H_C, dtype=F32)
    wuv = jnp.einsum('rhv,hg->hrgv', p['w_uv'], eye).reshape(H_C, KV_RANK, W_C).astype(BF16)
    return mla_w, wf, wuv


def _build_q(cqn, of, wf, hm, tq):
    q_all = _dot(cqn, wf) * Q_SCALE
    lat = [q_all[:, h * KV_RANK:(h + 1) * KV_RANK].astype(BF16) for h in range(H_C)]
    rope = [of * hm[h:h + 1].astype(BF16) for h in range(H_C)]
    return jnp.concatenate(lat, axis=0), jnp.concatenate(rope, axis=0)


def _project_out(o, wuv_ref, tq):
    y = _dot(o[0:tq].astype(BF16), wuv_ref[0])
    for h in range(1, H_C):
        y = y + _dot(o[h * tq:(h + 1) * tq].astype(BF16), wuv_ref[h])
    return y


def _attn_prompt_kernel(cqn_ref, of_ref, kp_ref, wf_ref, hm_ref, wuv_ref, y_ref,
                        ql_scr, qr_scr, m_scr, acc_scr):
    tq = cqn_ref.shape[0]
    tk = kp_ref.shape[0]
    qi = pl.program_id(1)
    ki = pl.program_id(2)

    @pl.when(ki == 0)
    def _():
        lat, rope = _build_q(cqn_ref[...], of_ref[...], wf_ref[...], hm_ref[...], tq)
        ql_scr[...] = lat
        qr_scr[...] = rope
        m_scr[...] = jnp.full(m_scr.shape, NEG_INF, F32)
        acc_scr[...] = jnp.zeros(acc_scr.shape, F32)

    def step(masked):
        kp = kp_ref[...]
        ckv = kp[:, 0:KV_RANK]
        krope = kp[:, KV_RANK:]
        ones = jnp.ones((tk, LANES), BF16)
        def scores(h):
            rows = pl.ds(h * tq, tq)
            return _dot_nt(ql_scr[rows, :], ckv) + _dot_nt(qr_scr[rows, :], krope)

        ahead = [scores(h) for h in range(ATT_SKEW)]
        for h in range(H_C):
            rows = pl.ds(h * tq, tq)
            s = ahead.pop(0)
            if h + ATT_SKEW < H_C:
                ahead.append(scores(h + ATT_SKEW))
            if masked:
                qpos = lax.broadcasted_iota(jnp.int32, s.shape, 0)
                kpos = lax.broadcasted_iota(jnp.int32, s.shape, 1)
                s = jnp.where(kpos <= qpos, s, NEG_INF)
            m_old = m_scr[rows, :]
            m_new = jnp.maximum(m_old, jnp.max(s, axis=-1, keepdims=True))
            alpha = jnp.exp2(m_old - m_new)
            p = jnp.concatenate([jnp.exp2(st - m_new) for st in _lane_tiles(s)], axis=1).astype(BF16)
            pv = jnp.concatenate([_dot(p, ckv), _dot(p, ones)], axis=1)
            acc = acc_scr[rows, :]
            acc_scr[rows, :] = jnp.concatenate([at * alpha for at in _lane_tiles(acc)], axis=1) + pv
            m_scr[rows, :] = m_new

    @pl.when(ki < qi)
    def _():
        step(False)

    @pl.when(ki == qi)
    def _():
        step(True)
        acc = acc_scr[...]
        l = acc[:, KV_RANK:KV_RANK + LANES]
        o = jnp.concatenate([at / l for at in _lane_tiles(acc[:, 0:KV_RANK])], axis=1)
        y_ref[...] = _project_out(o, wuv_ref, tq)


def _attn_prompt(cqn, of, kp, wf, hm, wuv, t_len):
    nb, tp, _ = kp.shape
    tq = ATT_TILE
    nt = tp // tq
    rows = H_C * tq
    const = lambda w: pl.BlockSpec(w.shape, lambda b, i, j: (0,) * w.ndim)
    return pl.pallas_call(
        _attn_prompt_kernel,
        grid=(nb, nt, nt),
        in_specs=[pl.BlockSpec((None, tq, Q_RANK), lambda b, i, j: (b, i, 0)),
                  pl.BlockSpec((None, tq, 256), lambda b, i, j: (b, i, 0)),
                  pl.BlockSpec((None, tq, 512), lambda b, i, j: (b, jnp.minimum(i, j), 0)),
                  const(wf), const(hm), const(wuv)],
        out_specs=pl.BlockSpec((None, tq, W_C), lambda b, i, j: (b, i, 0)),
        out_shape=jax.ShapeDtypeStruct((nb, t_len, W_C), F32),
        scratch_shapes=[pltpu.VMEM((rows, KV_RANK), BF16), pltpu.VMEM((rows, 256), BF16),
                        pltpu.VMEM((rows, LANES), F32), pltpu.VMEM((rows, KV_RANK + LANES), F32)],
        compiler_params=_cparams("parallel", "parallel", "arbitrary"), name="attn_prompt",
    )(cqn, of, kp, wf, hm, wuv)


def _kr_leaf(krt):
    return jnp.concatenate([krt[..., 0:ROPE_HALF], krt[..., LANES:LANES + ROPE_HALF]], axis=-1)


def _mla_branch_prompt(z3, mla_w, wf, wuv, hm):
    nb, t_len, _ = z3.shape
    tp = -(-t_len // ATT_TILE) * ATT_TILE
    cos, sin = _rope_tables(jnp.arange(tp))
    ckv, krt, kp, cqn, of = _mla_prep(z3, cos, sin, mla_w, tm=ATT_TILE)
    yc = _attn_prompt(cqn, of, kp, wf, hm, wuv, t_len)
    return yc, ckv, _kr_leaf(krt)


def _attn_sample_kernel(pt_ref, *refs, n_pages):
    ckv_pages = refs[0:n_pages]
    kr_pages = refs[n_pages:2 * n_pages]
    (cqn_ref, of_ref, cnew_ref, knew_ref, wf_ref, hm_ref, tile_ref, wuv_ref, y_ref,
     k_scr, kr_scr) = refs[2 * n_pages:]
    del pt_ref
    tq = cqn_ref.shape[0]
    page = ckv_pages[0].shape[0]

    for i in range(n_pages):
        k_scr[i * page:(i + 1) * page, :] = ckv_pages[i][...].astype(BF16)
        kr_scr[:, i * page:(i + 1) * page] = kr_pages[i][...].astype(BF16)

    lat, rope = _build_q(cqn_ref[...], of_ref[...], wf_ref[...], hm_ref[...], tq)
    qr = _dot(rope, tile_ref[...]).astype(BF16)
    kpast = k_scr[...]
    s = _dot_nt(lat, kpast) + _dot(qr, kr_scr[...])
    cnew = cnew_ref[...].astype(BF16)
    s_new = _dot_nt(lat, cnew) + _dot_nt(qr, knew_ref[...].astype(BF16))
    tpos = lax.broadcasted_iota(jnp.int32, s_new.shape, 0) % tq
    s_new = jnp.where(lax.broadcasted_iota(jnp.int32, s_new.shape, 1) <= tpos, s_new, NEG_INF)
    m = jnp.maximum(jnp.max(s, axis=-1, keepdims=True), jnp.max(s_new, axis=-1, keepdims=True))
    p = jnp.exp2(s - m)
    p_new = jnp.exp2(s_new - m)
    l = jnp.sum(p, axis=-1, keepdims=True) + jnp.sum(p_new, axis=-1, keepdims=True)
    o = (_dot(p.astype(BF16), kpast) + _dot(p_new.astype(BF16), cnew)) / l
    y_ref[...] = _project_out(o, wuv_ref, tq)


def _attn_sample(page_table, cache_ckv, cache_krope_t, layer, cqn, of, cnew, knew, wf, hm, tile_t, wuv):
    nb, tq, _ = cqn.shape
    n_pages = page_table.shape[1]
    page = cache_ckv.shape[2]
    pt_flat = page_table.reshape(-1)

    def ckv_spec(i):
        return pl.BlockSpec((None, None, page, KV_RANK), lambda b, pt: (layer, pt[b * n_pages + i], 0, 0))

    def kr_spec(i):
        return pl.BlockSpec((None, None, ROPE_DIM, page), lambda b, pt: (layer, pt[b * n_pages + i], 0, 0))

    per_b = lambda width: pl.BlockSpec((None, tq, width), lambda b, pt: (b, 0, 0))
    const = lambda w: pl.BlockSpec(w.shape, lambda b, pt: (0,) * w.ndim)
    grid_spec = pltpu.PrefetchScalarGridSpec(
        num_scalar_prefetch=1,
        grid=(nb,),
        in_specs=[ckv_spec(i) for i in range(n_pages)] + [kr_spec(i) for i in range(n_pages)]
        + [per_b(Q_RANK), per_b(256), per_b(KV_RANK), per_b(ROPE_DIM), const(wf), const(hm), const(tile_t),
           const(wuv)],
        out_specs=pl.BlockSpec((None, tq, W_C), lambda b, pt: (b, 0, 0)),
        scratch_shapes=[pltpu.VMEM((n_pages * page, KV_RANK), BF16),
                        pltpu.VMEM((ROPE_DIM, n_pages * page), BF16)],
    )
    return pl.pallas_call(
        functools.partial(_attn_sample_kernel, n_pages=n_pages),
        grid_spec=grid_spec,
        out_shape=jax.ShapeDtypeStruct((nb, tq, W_C), F32),
        compiler_params=_cparams("parallel"), name="attn_sample",
    )(pt_flat, *([cache_ckv] * n_pages), *([cache_krope_t] * n_pages), cqn, of, cnew, knew, wf, hm, tile_t, wuv)


def _mla_branch_sample(z3, page_table, cache_ckv, cache_krope_t, layer, n_past, mla_w, wf, wuv, hm, tile_t):
    nb, ts, _ = z3.shape
    cos, sin = _rope_tables(n_past + jnp.arange(nb * ts) % ts)
    ckv, krt, _, cqn, of = _mla_prep(z3.reshape(1, nb * ts, Z_COLS), cos, sin, mla_w, tm=ATT_TILE)
    ckv = ckv.reshape(nb, ts, KV_RANK)
    kr = _kr_leaf(krt.reshape(nb, ts, 256))
    yc = _attn_sample(page_table, cache_ckv, cache_krope_t, layer, cqn.reshape(nb, ts, Q_RANK),
                      of.reshape(nb, ts, 256), ckv, kr, wf, hm, tile_t, wuv)
    return yc, ckv, kr


def _out_kernel(ya_ref, yb_ref, yc_ref, gc_ref, x_ref, wa_ref, wb_ref, wc_ref, g_ref, b_ref, o_ref):
    yc = (yc_ref[...] * _silu(gc_ref[...])).astype(BF16)
    out = _dot(ya_ref[...], wa_ref[...]) + _dot(yb_ref[...], wb_ref[...]) + _dot(yc, wc_ref[...])
    h = DN_ALPHA * x_ref[...] + out
    mu = jnp.mean(h, axis=-1, keepdims=True)
    d = h - mu
    var = jnp.mean(d * d, axis=-1, keepdims=True)
    o_ref[...] = d * lax.rsqrt(var + 1e-5) * g_ref[...] + b_ref[...]


def _out_proj(ya, yb, yc, z, x, ow, tm):
    n = x.shape[0]
    wspecs = [pl.BlockSpec(w.shape, lambda i: (0,) * w.ndim) for w in ow]
    rowblk = lambda width, col=0: pl.BlockSpec((tm, width), lambda i: (i, col))
    return pl.pallas_call(
        _out_kernel,
        grid=(n // tm,),
        in_specs=[rowblk(W_A), rowblk(W_B), rowblk(W_C), rowblk(W_C, 3), rowblk(D_MODEL)] + wspecs,
        out_specs=rowblk(D_MODEL),
        out_shape=jax.ShapeDtypeStruct((n, D_MODEL), F32),
        compiler_params=_cparams("parallel"), name="out_proj",
    )(ya, yb, yc, z, x, *ow)


def _block_diag(blocks):
    nblk, bw, _ = blocks.shape
    eye = jnp.eye(nblk, dtype=blocks.dtype)
    return jnp.einsum('hij,hg->higj', blocks, eye).reshape(nblk * bw, nblk * bw)


def _pad_to(x, shape):
    return jnp.pad(x, [(0, s - d) for d, s in zip(x.shape, shape)])


def _rope_tables(pos):
    inv = ROPE_THETA ** (-2.0 * jnp.arange(ROPE_HALF, dtype=F32) / ROPE_DIM)
    ang = pos.astype(F32)[:, None] * inv[None, :]
    return jnp.tile(jnp.cos(ang), (1, H_C)), jnp.tile(jnp.sin(ang), (1, H_C))


def _static_tables():
    lane = np.arange(256)
    hm = np.stack([((lane % LANES) // ROPE_HALF == h) for h in range(H_C)]).astype(np.float32)
    hmb = np.stack([(lane // HD_B == h) for h in range(H_B)]).astype(np.float32)
    ones_bd = (lane[:, None] // HD_B == lane[None, :] // HD_B).astype(np.float32)
    tile_t = np.zeros((256, ROPE_DIM), np.float32)
    for ln in lane:
        tile_t[ln, (ln // LANES) * ROPE_HALF + ln % ROPE_HALF] = 1.0
    return jnp.asarray(hm), jnp.asarray(hmb), jnp.asarray(ones_bd), jnp.asarray(tile_t, BF16)


_PER_LAYER = ('w_in', 'conv_w', 'conv_b', 'lru_wr', 'lru_br', 'lru_wi', 'lru_bi', 'lru_lambda', 'rk_mix',
              'rk_wr', 'rk_wk', 'rk_wv', 'rk_w0', 'rk_w1', 'rk_w2', 'rk_a0', 'rk_a1', 'rk_a2', 'rk_kk',
              'rk_ka', 'rk_rk', 'rk_gn_g', 'rk_gn_b', 'q_norm', 'kv_norm', 'w_uq', 'w_uk', 'w_uv', 'w_out',
              'ln_g', 'ln_b')


def kernel(x_prompt, x_sample, cache_ckv, cache_krope, state_conv, state_lru, state_shift, state_wkv,
           page_table, meta_tokens, w_in, conv_w, conv_b, lru_wr, lru_br, lru_wi, lru_bi, lru_lambda,
           rk_mix, rk_wr, rk_wk, rk_wv, rk_w0, rk_w1, rk_w2, rk_a0, rk_a1, rk_a2, rk_kk, rk_ka, rk_rk,
           rk_gn_g, rk_gn_b, q_norm, kv_norm, w_uq, w_uk, w_uv, w_out, ln_g, ln_b):
    stacked = dict(zip(_PER_LAYER, (w_in, conv_w, conv_b, lru_wr, lru_br, lru_wi, lru_bi, lru_lambda, rk_mix,
                                    rk_wr, rk_wk, rk_wv, rk_w0, rk_w1, rk_w2, rk_a0, rk_a1, rk_a2, rk_kk,
                                    rk_ka, rk_rk, rk_gn_g, rk_gn_b, q_norm, kv_norm, w_uq, w_uk, w_uv, w_out,
                                    ln_g, ln_b)))
    bp, seq, _ = x_prompt.shape
    bs, ts, _ = x_sample.shape
    tp_len = N_META + seq
    n_past = page_table.shape[1] * cache_ckv.shape[2]
    depth = w_in.shape[0]
    assert ts == 8 and CHUNK % ts == 0 and bs % (CHUNK // ts) == 0
    assert tp_len % 24 == 0

    hm, hmb, ones_bd, tile_t = _static_tables()
    row2 = lambda v: v.reshape(1, -1)
    cache_krope_t = jnp.swapaxes(cache_krope, 2, 3)

    xp = jnp.concatenate(
        [jnp.broadcast_to(meta_tokens[None].astype(x_prompt.dtype), (bp, N_META, D_MODEL)), x_prompt],
        axis=1).reshape(bp * tp_len, D_MODEL)
    xs = x_sample.reshape(bs * ts, D_MODEL)
    tm_p = tp_len // 3
    tm_s = bs * ts

    p_states, s_states = [], []
    for l in range(depth):
        p = {name: arr[l] for name, arr in stacked.items()}
        wi = p['w_in']
        kr_cols = wi[:, 1536:1568]
        w_in_r = jnp.concatenate(
            [wi[:, 0:1536], wi[:, 1568:2080],
             jnp.tile(kr_cols[:, 0:ROPE_HALF], (1, H_C)), jnp.tile(kr_cols[:, ROPE_HALF:], (1, H_C))],
            axis=1).astype(BF16)
        lru_w = _rglru_weights(p)
        prep_w, wkv_w = _rwkv_weights(p, hmb, ones_bd)
        mla_w, wf, wuv = _mla_weights(p)
        wo = p['w_out'].astype(BF16)
        out_w = (wo[0:W_A], wo[W_A:W_A + W_B], wo[W_A + W_B:], row2(p['ln_g']), row2(p['ln_b']))

        z = _in_proj(xp, w_in_r, tm_p)
        z3 = z.reshape(bp, tp_len, Z_COLS)
        ya, lru_new = _rglru_branch_prompt(z3, lru_w)
        yb, wkv_new = _rwkv_branch_prompt(z3, prep_w, wkv_w)
        yc, ckv, kr = _mla_branch_prompt(z3, mla_w, wf, wuv, hm)
        x_new = _out_proj(ya, yb.reshape(bp * tp_len, W_B), yc.reshape(bp * tp_len, W_C), z, xp, out_w, tm_p)
        p_states.append((ckv, kr, z3[:, tp_len - (CONV_W - 1):, 0:W_A], lru_new,
                         z3[:, tp_len - 1, 2 * W_A:2 * W_A + W_B], wkv_new))
        xp = x_new

        z = _in_proj(xs, w_in_r, tm_s)
        z3 = z.reshape(bs, ts, Z_COLS)
        ya, lru_new = _rglru_branch_sample(z3, state_conv[l], state_lru[l], n_past, lru_w)
        yb, wkv_new = _rwkv_branch_sample(z3, state_shift[l], state_wkv[l], prep_w, wkv_w)
        yc, ckv, kr = _mla_branch_sample(z3, page_table, cache_ckv, cache_krope_t, l, n_past,
                                         mla_w, wf, wuv, hm, tile_t)
        x_new = _out_proj(ya, yb.reshape(bs * ts, W_B), yc.reshape(bs * ts, W_C), z, xs, out_w, tm_s)
        s_states.append((ckv, kr, z3[:, ts - (CONV_W - 1):, 0:W_A], lru_new,
                         z3[:, ts - 1, 2 * W_A:2 * W_A + W_B], wkv_new))
        xs = x_new

    y_prompt = xp.reshape(bp, tp_len, D_MODEL)[:, N_META:]
    y_sample = xs.reshape(bs, ts, D_MODEL)
    p_out = [jnp.stack([st[i] for st in p_states]) for i in range(6)]
    s_out = [jnp.stack([st[i] for st in s_states]) for i in range(6)]
    return (y_prompt, y_sample, *p_out, *s_out)
```

```python
import functools
import itertools

import numpy as np
import jax
import jax.numpy as jnp
from jax import lax
from jax.experimental import pallas as pl
from jax.experimental.pallas import tpu as pltpu

F32 = jnp.float32
BF16 = jnp.bfloat16
HIGHEST = lax.Precision.HIGHEST

D_MODEL = 1024
N_META = 16
W_A = 256
W_B = 256
W_C = 512
CONV_W = 4
LRU_C = 8.0
HD_B = 64
H_B = 4
GN_EPS = 64e-5
H_C = 8
V_DIM = 64
NOPE_DIM = 64
ROPE_DIM = 32
ROPE_HALF = 16
Q_RANK = 256
KV_RANK = 256
ROPE_THETA = 10000.0
ATTN_SCALE = (NOPE_DIM + ROPE_DIM) ** -0.5
LOG2_E = 1.4426950408889634
Q_SCALE = ATTN_SCALE * LOG2_E
NEG_INF = -1e30
DEPTH = 2
DN_ALPHA = (2 * DEPTH) ** 0.25

LANES = 128
Z_COLS = 2304
CHUNK = 64
WKV_BATCH = 4
ATT_TQ = 256
ATT_TK = 512
ATT_SKEW = 2
SAMPLE_PAGES_PER_CHUNK = 8

VMEM_LIMIT = 48 * 1024 * 1024


def _cparams(*sem):
    return pltpu.CompilerParams(dimension_semantics=sem, vmem_limit_bytes=VMEM_LIMIT)


def _dot(a, b):
    return jnp.dot(a, b, preferred_element_type=F32)


def _dot_nt(a, b):
    return lax.dot_general(a, b, (((1,), (1,)), ((), ())), preferred_element_type=F32)


def _dot_tn(a, b):
    return lax.dot_general(a, b, (((0,), (0,)), ((), ())), preferred_element_type=F32)


def _dot_hi(a, b):
    return jnp.dot(a, b, preferred_element_type=F32, precision=HIGHEST)


def _split(x):
    hi = x.astype(BF16)
    return hi, (x - hi.astype(F32)).astype(BF16)


def _dot3(a, b):
    a_hi, a_lo = _split(a)
    b_hi, b_lo = _split(b)
    return _dot(a_hi, b_hi) + (_dot(a_hi, b_lo) + _dot(a_lo, b_hi))


def _softplus(x):
    return jnp.maximum(x, 0.0) + jnp.log1p(jnp.exp(-jnp.abs(x)))


def _silu(x):
    return x * jax.nn.sigmoid(x)


def _lane_tiles(x):
    return [x[:, j * LANES:(j + 1) * LANES] for j in range(x.shape[1] // LANES)]


def _mm_kernel(x_ref, w_ref, o_ref):
    o_ref[...] = _dot(x_ref[...].astype(BF16), w_ref[...])


def _in_proj(x, w, tm):
    n, k = x.shape
    m = w.shape[1]
    return pl.pallas_call(
        _mm_kernel,
        grid=(n // tm,),
        in_specs=[pl.BlockSpec((tm, k), lambda i: (i, 0)), pl.BlockSpec((k, m), lambda i: (0, 0))],
        out_specs=pl.BlockSpec((tm, m), lambda i: (i, 0)),
        out_shape=jax.ShapeDtypeStruct((n, m), F32),
        compiler_params=_cparams("parallel"),
        name="in_proj",
    )(x, w)


def _rglru_kernel(*refs, seq_len, pos0, has_state):
    if has_state:
        (u_ref, g_ref, f_ref, hp_ref, cw_ref, cb_ref, wr_ref, br_ref, wi_ref, bi_ref, lam_ref,
         y_ref, h_ref) = refs
    else:
        (u_ref, g_ref, cw_ref, cb_ref, wr_ref, br_ref, wi_ref, bi_ref, lam_ref,
         y_ref, h_ref, a_scr, b_scr) = refs
    u = u_ref[...]
    rows = u.shape[0]
    row = lax.broadcasted_iota(jnp.int32, u.shape, 0)
    t = row if rows == seq_len else row & (seq_len - 1)

    cw = cw_ref[...]
    xc = cb_ref[...] + u * cw[CONV_W - 1:CONV_W]
    for k in range(1, CONV_W):
        sh = pltpu.roll(u, k, 0)
        if has_state:
            f = f_ref[...]
            prev = f if k == CONV_W - 1 else pltpu.roll(f, rows - (CONV_W - 1 - k), 0)
        else:
            prev = 0.0
        sh = jnp.where(t < k, prev, sh)
        xc = xc + sh * cw[CONV_W - 1 - k:CONV_W - k]

    xcb = xc.astype(BF16)
    gate_r = jax.nn.sigmoid(_dot(xcb, wr_ref[...]) + br_ref[...])
    gate_i = jax.nn.sigmoid(_dot(xcb, wi_ref[...]) + bi_ref[...])
    log_a = (-LRU_C) * gate_r * _softplus(-lam_ref[...])
    th = jnp.tanh(log_a)
    neg_expm1 = -2.0 * th / (1.0 - th)
    reset = (t + pos0) == 0
    a = jnp.where(reset, 0.0, jnp.exp(log_a))
    mult = jnp.where(reset, 1.0, jnp.sqrt(neg_expm1))
    bv = xc * gate_i * mult

    t8 = row & 7
    for s in (1, 2, 4):
        a_s = jnp.where(t8 >= s, pltpu.roll(a, s, 0), 1.0)
        b_s = jnp.where(t8 >= s, pltpu.roll(bv, s, 0), 0.0)
        bv = a * b_s + bv
        a = a * a_s

    if has_state:
        h = bv + a * hp_ref[...]
        h_ref[...] = h
    else:
        a_scr[...] = a
        b_scr[...] = bv

        def body(i, carry):
            r0 = pl.multiple_of(i * 8, 8)
            hh = b_scr[pl.ds(r0, 8), :] + a_scr[pl.ds(r0, 8), :] * carry
            b_scr[pl.ds(r0, 8), :] = hh
            return jnp.broadcast_to(hh[7:8, :], hh.shape)

        carry = lax.fori_loop(0, rows // 8, body, jnp.zeros((8, u.shape[1]), F32))
        h = b_scr[...]
        h_ref[...] = carry[0:1, :]
    y_ref[...] = (h * _silu(g_ref[...])).astype(BF16)


def _rglru(z, lw, *, nseq, seq_len, pos0, conv_ext=None, h_ext=None):
    n = z.shape[0]
    wspecs = [pl.BlockSpec(w.shape, lambda i: (0,) * w.ndim) for w in lw]
    if conv_ext is None:
        rows = seq_len
        in_specs = [pl.BlockSpec((rows, W_A), lambda i: (i, 0)),
                    pl.BlockSpec((rows, W_A), lambda i: (i, 1))] + wspecs
        out_specs = [pl.BlockSpec((rows, W_A), lambda i: (i, 0)),
                     pl.BlockSpec((None, 1, W_A), lambda i: (i, 0, 0))]
        out_shape = [jax.ShapeDtypeStruct((n, W_A), BF16), jax.ShapeDtypeStruct((nseq, 1, W_A), F32)]
        scratch = [pltpu.VMEM((rows, W_A), F32), pltpu.VMEM((rows, W_A), F32)]
        args = (z, z) + tuple(lw)
        grid = (nseq,)
    else:
        rows = n
        full = pl.BlockSpec((rows, W_A), lambda i: (0, 0))
        in_specs = [full, pl.BlockSpec((rows, W_A), lambda i: (0, 1)), full, full] + wspecs
        out_specs = [full, full]
        out_shape = [jax.ShapeDtypeStruct((n, W_A), BF16), jax.ShapeDtypeStruct((n, W_A), F32)]
        scratch = []
        args = (z, z, conv_ext, h_ext) + tuple(lw)
        grid = (1,)
    return pl.pallas_call(
        functools.partial(_rglru_kernel, seq_len=seq_len, pos0=pos0, has_state=conv_ext is not None),
        grid=grid, in_specs=in_specs, out_specs=out_specs, out_shape=out_shape,
        scratch_shapes=scratch, compiler_params=_cparams("parallel"), name="rglru",
    )(*args)


def _rglru_weights(p):
    row2 = lambda v: v.reshape(1, -1)
    return (p['conv_w'], row2(p['conv_b']), _block_diag(p['lru_wr']).astype(BF16), row2(p['lru_br']),
            _block_diag(p['lru_wi']).astype(BF16), row2(p['lru_bi']), row2(p['lru_lambda']))


def _rglru_branch_prompt(z3, lru_w):
    nb, t_len, _ = z3.shape
    y, h_last = _rglru(z3.reshape(nb * t_len, Z_COLS), lru_w, nseq=nb, seq_len=t_len, pos0=0)
    return y, h_last.reshape(nb, W_A)


def _rglru_branch_sample(z3, conv_prev, h_prev, pos0, lru_w):
    nb, ts, _ = z3.shape
    conv_ext = _pad_to(conv_prev, (nb, ts, W_A)).reshape(nb * ts, W_A)
    h_ext = jnp.repeat(h_prev, ts, axis=0)
    y, h_all = _rglru(z3.reshape(nb * ts, Z_COLS), lru_w, nseq=nb, seq_len=ts, pos0=pos0,
                      conv_ext=conv_ext, h_ext=h_ext)
    return y, h_all.reshape(nb, ts, W_A)[:, ts - 1]


def _rwkv_prep_kernel(u_ref, p_ref, mix_ref, wr_ref, wk_ref, wv_ref, w0_ref, w1_ref, w2_ref,
                      a0_ref, a1_ref, a2_ref, kkw_ref, ka_ref, ones_ref,
                      r_ref, lw_ref, k_ref, v_ref, kk_ref, a_ref, *, seq_len, tiles_per_seq):
    u = u_ref[...]
    row = lax.broadcasted_iota(jnp.int32, u.shape, 0)
    rolled = pltpu.roll(u, 1, 0)
    if tiles_per_seq is None:
        u_prev = jnp.where((row & (seq_len - 1)) == 0, p_ref[...], rolled)
    else:
        first = (pl.program_id(0) % tiles_per_seq) == 0
        prow = jnp.where(first, 0.0, p_ref[7:8, :])
        u_prev = jnp.where(row == 0, prow, rolled)
    du = u_prev - u
    mix = mix_ref[...]
    xr = (u + du * mix[0:1]).astype(BF16)
    xw = (u + du * mix[1:2]).astype(BF16)
    xk = (u + du * mix[2:3]).astype(BF16)
    xv = (u + du * mix[3:4]).astype(BF16)
    xa = (u + du * mix[4:5]).astype(BF16)
    r = _dot(xr, wr_ref[...])
    k = _dot(xk, wk_ref[...])
    v = _dot(xv, wv_ref[...])
    lora_w = _dot(jnp.tanh(_dot(xw, w1_ref[...])).astype(BF16), w2_ref[...])
    w_log = -_softplus(-(w0_ref[...] + lora_w)) - 0.5
    a = jax.nn.sigmoid(a0_ref[...] + _dot(_dot(xa, a1_ref[...]).astype(BF16), a2_ref[...]))
    kk = k * kkw_ref[...]
    ssq = _dot3(kk * kk, ones_ref[...])
    kk = kk / jnp.maximum(jnp.sqrt(ssq), 1e-12)
    r_ref[...] = r
    lw_ref[...] = -jnp.exp(w_log)
    k_ref[...] = k * (1.0 + (a - 1.0) * ka_ref[...])
    v_ref[...] = v
    kk_ref[...] = kk
    a_ref[...] = a


def _rwkv_prep(z, pw, *, seq_len, tm, prev_ext=None):
    n = z.shape[0]
    wspecs = [pl.BlockSpec(w.shape, lambda i: (0,) * w.ndim) for w in pw]
    if prev_ext is None:
        tiles = seq_len // tm
        p_arg = z
        p_spec = pl.BlockSpec((8, W_B), lambda i: (jnp.maximum(i * (tm // 8) - 1, 0), 2))
    else:
        tiles = None
        p_arg = prev_ext
        p_spec = pl.BlockSpec((tm, W_B), lambda i: (i, 0))
    out = jax.ShapeDtypeStruct((n, W_B), F32)
    ospec = pl.BlockSpec((tm, W_B), lambda i: (i, 0))
    return pl.pallas_call(
        functools.partial(_rwkv_prep_kernel, seq_len=seq_len, tiles_per_seq=tiles),
        grid=(n // tm,),
        in_specs=[pl.BlockSpec((tm, W_B), lambda i: (i, 2)), p_spec] + wspecs,
        out_specs=[ospec] * 6, out_shape=[out] * 6,
        compiler_params=_cparams("parallel"), name="rwkv_prep",
    )(z, p_arg, *pw)


def _wkv_kernel(r_ref, lw_ref, k_ref, v_ref, kk_ref, a_ref, g_ref, s0_ref, hm_ref, ones_ref,
                rk_ref, gng_ref, gnb_ref, y_ref, s_ref, *, t_valid, nbatch, nseq, sub_len):
    c = pl.program_id(1)

    @pl.when(c == 0)
    def _():
        s_ref[...] = s0_ref[...]

    chains = [_wkv_chunk(c, j, r_ref, lw_ref, k_ref, v_ref, kk_ref, a_ref, g_ref, hm_ref, ones_ref,
                         rk_ref, gng_ref, gnb_ref, y_ref, s_ref, t_valid=t_valid, nseq=nseq, sub_len=sub_len)
              for j in range(nbatch)]
    for _ in itertools.zip_longest(*chains):
        pass


def _wkv_chunk(c, j, r_ref, lw_ref, k_ref, v_ref, kk_ref, a_ref, g_ref, hm_ref, ones_ref,
               rk_ref, gng_ref, gnb_ref, y_ref, s_ref, *, t_valid, nseq, sub_len):
    width = W_B
    nh = H_B
    s_base = j * nseq
    row = lax.broadcasted_iota(jnp.int32, (CHUNK, width), 0)
    valid = (c * CHUNK + row) < t_valid
    zero = jnp.zeros((CHUNK, width), F32)
    r = jnp.where(valid, r_ref[j], zero)
    lw = jnp.where(valid, lw_ref[j], zero)
    k = jnp.where(valid, k_ref[j], zero)
    v = jnp.where(valid, v_ref[j], zero)
    kk = jnp.where(valid, kk_ref[j], zero)
    a = jnp.where(valid, a_ref[j], zero)

    qi = lax.broadcasted_iota(jnp.int32, (CHUNK, CHUNK), 0)
    si = lax.broadcasted_iota(jnp.int32, (CHUNK, CHUNK), 1)
    same = (qi // sub_len) == (si // sub_len)
    cum = _dot_hi(jnp.where(same & (qi >= si), 1.0, 0.0), lw)
    tot = _dot_hi(jnp.where(same, 1.0, 0.0), lw)
    yield
    e_neg = jnp.exp(-cum)
    e_tail = jnp.exp(tot - cum)
    rg = r * jnp.exp(cum)
    ag = -kk * jnp.exp(cum - lw)
    kka = kk * a
    bg = kka * e_neg
    kg = k * e_neg
    bt = kka * e_tail
    kt = k * e_tail

    hm = hm_ref[...]

    def stack(x):
        return jnp.concatenate([x * hm[h:h + 1] for h in range(nh)], axis=0)

    def unstack(x):
        out = x[0:CHUNK]
        for h in range(1, nh):
            out = out + x[h * CHUNK:(h + 1) * CHUNK]
        return out

    a_st = stack(ag).astype(BF16)
    r_st = stack(rg)
    b_st = stack(bg).astype(BF16)
    k_st = stack(kg).astype(BF16)
    v_st = stack(v)

    n_st = nh * CHUNK
    rq = lax.broadcasted_iota(jnp.int32, (n_st, n_st), 0) % CHUNK
    cs = lax.broadcasted_iota(jnp.int32, (n_st, n_st), 1) % CHUNK
    same_st = (rq // sub_len) == (cs // sub_len)
    strict = same_st & (rq > cs)
    incl = same_st & (rq >= cs)

    l_ab = jnp.where(strict, _dot_nt(a_st, b_st), 0.0)
    l_ak = jnp.where(strict, _dot_nt(a_st, k_st), 0.0)
    yield
    p_rb = jnp.where(incl, _dot_nt(r_st.astype(BF16), b_st), 0.0)
    p_rk = jnp.where(incl, _dot_nt(r_st.astype(BF16), k_st), 0.0)
    yield

    eye = jnp.where(lax.broadcasted_iota(jnp.int32, (n_st, n_st), 0)
                    == lax.broadcasted_iota(jnp.int32, (n_st, n_st), 1), 1.0, 0.0)
    tinv = eye + l_ab
    lp = l_ab
    levels = int(np.log2(sub_len))
    for _ in range(levels - 1):
        lpb = lp.astype(BF16)
        lp = _dot(lpb, lpb)
        yield
        tinv = tinv + _dot(lp.astype(BF16), tinv.astype(BF16))
        yield

    x0_parts, r0_parts = [], []
    for s in range(nseq):
        st = s_ref[s_base + s].astype(BF16)
        sl = slice(s * sub_len, (s + 1) * sub_len)
        rows_s = jnp.concatenate([ag[sl], rg[sl]], axis=0).astype(BF16)
        xs = _dot_nt(rows_s, st)
        x0_parts.append(xs[0:sub_len])
        r0_parts.append(xs[sub_len:2 * sub_len])
    x0 = x0_parts[0] if nseq == 1 else jnp.concatenate(x0_parts, axis=0)
    r0 = r0_parts[0] if nseq == 1 else jnp.concatenate(r0_parts, axis=0)
    yield

    x_st = stack(x0) + _dot(l_ak.astype(BF16), v_st.astype(BF16))
    yield
    u_st = _dot(tinv.astype(BF16), x_st.astype(BF16))
    yield
    o_st = stack(r0) + _dot(p_rb.astype(BF16), u_st.astype(BF16)) + _dot(p_rk.astype(BF16), v_st.astype(BF16))
    u_all = unstack(u_st)
    o = unstack(o_st)
    yield

    bd = (lax.broadcasted_iota(jnp.int32, (width, width), 0) // HD_B
          == lax.broadcasted_iota(jnp.int32, (width, width), 1) // HD_B)
    for s in range(nseq):
        sl = slice(s * sub_len, (s + 1) * sub_len)
        upd = (_dot_tn(u_all[sl].astype(BF16), bt[sl].astype(BF16))
               + _dot_tn(v[sl].astype(BF16), kt[sl].astype(BF16)))
        gam = jnp.exp(tot[s * sub_len:s * sub_len + 1])
        s_ref[s_base + s] = s_ref[s_base + s] * gam + jnp.where(bd, upd, 0.0)

    ones = ones_ref[...]
    mu = _dot3(o, ones) * (1.0 / HD_B)
    d = o - mu
    var = _dot3(d * d, ones) * (1.0 / HD_B)
    o = d * lax.rsqrt(var + GN_EPS) * gng_ref[...] + gnb_ref[...]
    o = o + _dot3(r * k * rk_ref[...], ones) * v
    y_ref[j] = (o * _silu(g_ref[j])).astype(BF16)


def _wkv(z3, prep, s0, cw, *, t_valid, nbatch, nseq, sub_len):
    nb, t_len, _ = z3.shape
    nc = -(-t_len // CHUNK)
    seq = pl.BlockSpec((nbatch, CHUNK, W_B), lambda b, c: (b, c, 0))
    wspecs = [pl.BlockSpec(w.shape, lambda b, c: (0,) * w.ndim) for w in cw]
    sspec = pl.BlockSpec((nbatch * nseq, W_B, W_B), lambda b, c: (b, 0, 0))
    return pl.pallas_call(
        functools.partial(_wkv_kernel, t_valid=t_valid, nbatch=nbatch, nseq=nseq, sub_len=sub_len),
        grid=(nb // nbatch, nc),
        in_specs=[seq] * 6 + [pl.BlockSpec((nbatch, CHUNK, W_B), lambda b, c: (b, c, 3)), sspec] + wspecs,
        out_specs=[seq, sspec],
        out_shape=[jax.ShapeDtypeStruct((nb, t_len, W_B), BF16),
                   jax.ShapeDtypeStruct(s0.shape, F32)],
        compiler_params=_cparams("parallel", "arbitrary"), name="wkv",
    )(*prep, z3, s0, *cw)


def _rwkv_weights(p, hmb, ones_bd):
    row2 = lambda v: v.reshape(1, -1)
    prep_w = (_pad_to(p['rk_mix'], (8, W_B)), p['rk_wr'].astype(BF16), p['rk_wk'].astype(BF16),
              p['rk_wv'].astype(BF16), row2(p['rk_w0']), _pad_to(p['rk_w1'], (W_B, LANES)).astype(BF16),
              _pad_to(p['rk_w2'], (LANES, W_B)).astype(BF16), row2(p['rk_a0']),
              _pad_to(p['rk_a1'], (W_B, LANES)).astype(BF16), _pad_to(p['rk_a2'], (LANES, W_B)).astype(BF16),
              row2(p['rk_kk']), row2(p['rk_ka']), ones_bd)
    wkv_w = (hmb, ones_bd, row2(p['rk_rk']), row2(p['rk_gn_g']), row2(p['rk_gn_b']))
    return prep_w, wkv_w


def _state_to_bd(s):
    nb = s.shape[0]
    eye = jnp.eye(H_B, dtype=s.dtype)
    return jnp.einsum('bhij,hg->bhigj', s, eye).reshape(nb, W_B, W_B)


def _state_from_bd(s):
    return jnp.stack([s[:, h * HD_B:(h + 1) * HD_B, h * HD_B:(h + 1) * HD_B] for h in range(H_B)], axis=1)


def _rwkv_branch_prompt(z3, prep_w, wkv_w):
    nb, t_len, _ = z3.shape
    prep = _rwkv_prep(z3.reshape(nb * t_len, Z_COLS), prep_w, seq_len=t_len, tm=t_len // 3)
    prep3 = tuple(a.reshape(nb, t_len, W_B) for a in prep)
    y, s_bd = _wkv(z3, prep3, jnp.zeros((nb, W_B, W_B), F32), wkv_w, t_valid=t_len,
                   nbatch=min(nb, WKV_BATCH), nseq=1, sub_len=CHUNK)
    return y, _state_from_bd(s_bd)


def _rwkv_branch_sample(z3, shift_prev, s_prev, prep_w, wkv_w):
    nb, ts, _ = z3.shape
    z = z3.reshape(nb * ts, Z_COLS)
    prep = _rwkv_prep(z, prep_w, seq_len=ts, tm=nb * ts, prev_ext=jnp.repeat(shift_prev, ts, axis=0))
    per_chunk = CHUNK // ts
    groups = nb // per_chunk
    prep3 = tuple(a.reshape(groups, CHUNK, W_B) for a in prep)
    y, s_bd = _wkv(z.reshape(groups, CHUNK, Z_COLS), prep3, _state_to_bd(s_prev), wkv_w,
                   t_valid=CHUNK, nbatch=1, nseq=per_chunk, sub_len=ts)
    return y.reshape(nb, ts, W_B), _state_from_bd(s_bd)


def _mla_prep_kernel(cq_ref, ckv_ref, za_ref, zb_ref, cos_ref, sin_ref, qn_ref, kvn_ref, wr1_ref, wr2_ref,
                     ckv_out, krt_out, kp_out, cqn_out, of_out, *, t_valid):
    tm = cq_ref.shape[0]
    row = lax.broadcasted_iota(jnp.int32, (tm, 1), 0)
    valid = (pl.program_id(1) * tm + row) < t_valid
    cq = cq_ref[...]
    cqn = cq * lax.rsqrt(jnp.mean(cq * cq, axis=-1, keepdims=True) + 1e-6) * qn_ref[...]
    cv = ckv_ref[...]
    ckv = cv * lax.rsqrt(jnp.mean(cv * cv, axis=-1, keepdims=True) + 1e-6) * kvn_ref[...]
    cos = cos_ref[...]
    sin = sin_ref[...]
    za = za_ref[...]
    zb = zb_ref[...]
    kr1 = za * cos - zb * sin
    kr2 = za * sin + zb * cos
    cqb = cqn.astype(BF16)
    q1 = _dot(cqb, wr1_ref[...])
    q2 = _dot(cqb, wr2_ref[...])
    o1 = q1 * cos - q2 * sin
    o2 = q1 * sin + q2 * cos
    ckv_out[...] = ckv
    krt_out[:, 0:LANES] = kr1
    krt_out[:, LANES:] = kr2
    kp_out[:, 0:KV_RANK] = jnp.where(valid, ckv, 0.0).astype(BF16)
    kp_out[:, KV_RANK:KV_RANK + LANES] = jnp.where(valid, kr1, 0.0).astype(BF16)
    kp_out[:, KV_RANK + LANES:] = jnp.where(valid, kr2, 0.0).astype(BF16)
    cqn_out[...] = jnp.where(valid, cqn, 0.0).astype(BF16)
    of_out[:, 0:LANES] = jnp.where(valid, o1 * Q_SCALE, 0.0).astype(BF16)
    of_out[:, LANES:] = jnp.where(valid, o2 * Q_SCALE, 0.0).astype(BF16)


def _mla_prep(z3, cos, sin, mw, *, tm, tp):
    nb, t_len, _ = z3.shape
    nt = tp // tm
    last = -(-t_len // tm) - 1
    wspecs = [pl.BlockSpec(w.shape, lambda b, j: (0,) * w.ndim) for w in mw]
    real = lambda width, col: pl.BlockSpec((None, tm, width), lambda b, j: (b, jnp.minimum(j, last), col))
    blk = lambda width: pl.BlockSpec((None, tm, width), lambda b, j: (b, j, 0))
    tab = pl.BlockSpec((tm, LANES), lambda b, j: (jnp.minimum(j, last), 0))
    return pl.pallas_call(
        functools.partial(_mla_prep_kernel, t_valid=t_len),
        grid=(nb, nt),
        in_specs=[real(256, 4), real(256, 5), real(LANES, 16), real(LANES, 17), tab, tab] + wspecs,
        out_specs=[real(256, 0), real(256, 0), blk(512), blk(256), blk(256)],
        out_shape=[jax.ShapeDtypeStruct((nb, t_len, KV_RANK), F32),
                   jax.ShapeDtypeStruct((nb, t_len, 256), F32),
                   jax.ShapeDtypeStruct((nb, tp, 512), BF16),
                   jax.ShapeDtypeStruct((nb, tp, Q_RANK), BF16),
                   jax.ShapeDtypeStruct((nb, tp, 256), BF16)],
        compiler_params=_cparams("parallel", "parallel"), name="mla_prep",
    )(z3, z3, z3, z3, cos, sin, *mw)


def _fold_kernel(wq_ref, wk_ref, o_ref):
    o_ref[...] = _dot_hi(wq_ref[...], wk_ref[...]).astype(BF16)


def _fold_q(wq_nope, wuk_t):
    return pl.pallas_call(
        _fold_kernel,
        grid=(H_C,),
        in_specs=[pl.BlockSpec((None, Q_RANK, NOPE_DIM), lambda h: (h, 0, 0)),
                  pl.BlockSpec((None, NOPE_DIM, KV_RANK), lambda h: (h, 0, 0))],
        out_specs=pl.BlockSpec((Q_RANK, KV_RANK), lambda h: (0, h)),
        out_shape=jax.ShapeDtypeStruct((Q_RANK, H_C * KV_RANK), BF16),
        compiler_params=_cparams("parallel"), name="fold_q",
    )(wq_nope, wuk_t)


def _mla_weights(p):
    row2 = lambda v: v.reshape(1, -1)
    wq = p['w_uq'].reshape(Q_RANK, H_C, NOPE_DIM + ROPE_DIM)
    wq_nope = jnp.transpose(wq[:, :, 0:NOPE_DIM], (1, 0, 2))
    wr1 = wq[:, :, NOPE_DIM:NOPE_DIM + ROPE_HALF].reshape(Q_RANK, LANES).astype(BF16)
    wr2 = wq[:, :, NOPE_DIM + ROPE_HALF:].reshape(Q_RANK, LANES).astype(BF16)
    wuk_t = jnp.transpose(p['w_uk'], (1, 2, 0))
    wf = _fold_q(wq_nope, wuk_t)
    mla_w = (row2(p['q_norm']), row2(p['kv_norm']), wr1, wr2)
    eye = jnp.eye(H_C, dtype=F32)
    wuv = jnp.einsum('rhv,hg->hrgv', p['w_uv'], eye).reshape(H_C, KV_RANK, W_C).astype(BF16)
    return mla_w, wf, wuv


def _build_q(cqn, of, wf, hm, tq):
    q_all = _dot(cqn, wf) * Q_SCALE
    lat = [q_all[:, h * KV_RANK:(h + 1) * KV_RANK].astype(BF16) for h in range(H_C)]
    rope = [of * hm[h:h + 1].astype(BF16) for h in range(H_C)]
    return jnp.concatenate(lat, axis=0), jnp.concatenate(rope, axis=0)


def _project_out(o, wuv_ref, tq):
    y = _dot(o[0:tq].astype(BF16), wuv_ref[0])
    for h in range(1, H_C):
        y = y + _dot(o[h * tq:(h + 1) * tq].astype(BF16), wuv_ref[h])
    return y


def _attn_prompt_kernel(qt_ref, kt_ref, cqn_ref, of_ref, kp_ref, wf_ref, hm_ref, wuv_ref, y_ref,
                        ql_scr, qr_scr, m_scr, acc_scr):
    tq = cqn_ref.shape[0]
    tk = kp_ref.shape[0]
    t = pl.program_id(1)
    qi = qt_ref[t]
    ki = kt_ref[t]
    last = (qi * tq + tq - 1) // tk

    @pl.when(ki == 0)
    def _():
        lat, rope = _build_q(cqn_ref[...], of_ref[...], wf_ref[...], hm_ref[...], tq)
        ql_scr[...] = lat
        qr_scr[...] = rope
        m_scr[...] = jnp.full(m_scr.shape, NEG_INF, F32)
        acc_scr[...] = jnp.zeros(acc_scr.shape, F32)

    def step(masked):
        kp = kp_ref[...]
        ckv = kp[:, 0:KV_RANK]
        krope = kp[:, KV_RANK:]
        ones = jnp.ones((tk, LANES), BF16)
        def scores(h):
            rows = pl.ds(h * tq, tq)
            return _dot_nt(ql_scr[rows, :], ckv) + _dot_nt(qr_scr[rows, :], krope)

        ahead = [scores(h) for h in range(ATT_SKEW)]
        for h in range(H_C):
            rows = pl.ds(h * tq, tq)
            s = ahead.pop(0)
            if h + ATT_SKEW < H_C:
                ahead.append(scores(h + ATT_SKEW))
            if masked:
                qpos = qi * tq + lax.broadcasted_iota(jnp.int32, s.shape, 0)
                kpos = ki * tk + lax.broadcasted_iota(jnp.int32, s.shape, 1)
                s = jnp.where(kpos <= qpos, s, NEG_INF)
            m_old = m_scr[rows, :]
            m_new = jnp.maximum(m_old, jnp.max(s, axis=-1, keepdims=True))
            alpha = jnp.exp2(m_old - m_new)
            p = jnp.concatenate([jnp.exp2(st - m_new) for st in _lane_tiles(s)], axis=1).astype(BF16)
            pv = jnp.concatenate([_dot(p, ckv), _dot(p, ones)], axis=1)
            acc = acc_scr[rows, :]
            acc_scr[rows, :] = jnp.concatenate([at * alpha for at in _lane_tiles(acc)], axis=1) + pv
            m_scr[rows, :] = m_new

    @pl.when(ki < last)
    def _():
        step(False)

    @pl.when(ki == last)
    def _():
        step(True)
        acc = acc_scr[...]
        l = acc[:, KV_RANK:KV_RANK + LANES]
        o = jnp.concatenate([at / l for at in _lane_tiles(acc[:, 0:KV_RANK])], axis=1)
        y_ref[...] = _project_out(o, wuv_ref, tq)


def _attn_prompt(cqn, of, kp, wf, hm, wuv, t_len):
    nb, tp, _ = kp.shape
    tq, tk = ATT_TQ, ATT_TK
    nq = -(-t_len // tq)
    pairs = [(i, j) for i in range(nq) for j in range((i * tq + tq - 1) // tk + 1)]
    q_tab = jnp.asarray([p[0] for p in pairs], jnp.int32)
    k_tab = jnp.asarray([p[1] for p in pairs], jnp.int32)
    rows = H_C * tq
    const = lambda w: pl.BlockSpec(w.shape, lambda b, t, qt, kt: (0,) * w.ndim)
    grid_spec = pltpu.PrefetchScalarGridSpec(
        num_scalar_prefetch=2,
        grid=(nb, len(pairs)),
        in_specs=[pl.BlockSpec((None, tq, Q_RANK), lambda b, t, qt, kt: (b, qt[t], 0)),
                  pl.BlockSpec((None, tq, 256), lambda b, t, qt, kt: (b, qt[t], 0)),
                  pl.BlockSpec((None, tk, 512), lambda b, t, qt, kt: (b, kt[t], 0)),
                  const(wf), const(hm), const(wuv)],
        out_specs=pl.BlockSpec((None, tq, W_C), lambda b, t, qt, kt: (b, qt[t], 0)),
        scratch_shapes=[pltpu.VMEM((rows, KV_RANK), BF16), pltpu.VMEM((rows, 256), BF16),
                        pltpu.VMEM((rows, LANES), F32), pltpu.VMEM((rows, KV_RANK + LANES), F32)],
    )
    return pl.pallas_call(
        _attn_prompt_kernel,
        grid_spec=grid_spec,
        out_shape=jax.ShapeDtypeStruct((nb, t_len, W_C), F32),
        compiler_params=_cparams("parallel", "arbitrary"), name="attn_prompt",
    )(q_tab, k_tab, cqn, of, kp, wf, hm, wuv)


def _kr_leaf(krt):
    return jnp.concatenate([krt[..., 0:ROPE_HALF], krt[..., LANES:LANES + ROPE_HALF]], axis=-1)


def _mla_branch_prompt(z3, mla_w, wf, wuv, hm):
    nb, t_len, _ = z3.shape
    tp = -(-t_len // ATT_TK) * ATT_TK
    cos, sin = _rope_tables(jnp.arange(tp))
    ckv, krt, kp, cqn, of = _mla_prep(z3, cos, sin, mla_w, tm=ATT_TQ, tp=tp)
    yc = _attn_prompt(cqn, of, kp, wf, hm, wuv, t_len)
    return yc, ckv, _kr_leaf(krt)


def _page_copies(pt_ref, ckv_hbm, kr_hbm, kbuf, krbuf, sem, seq, slot, *, layer, n_pages):
    out = []
    for pg in range(n_pages):
        pid = pt_ref[seq * n_pages + pg]
        out.append(pltpu.make_async_copy(ckv_hbm.at[layer, pid], kbuf.at[slot, pg], sem.at[slot, 0]))
        out.append(pltpu.make_async_copy(kr_hbm.at[layer, pid], krbuf.at[slot, pg], sem.at[slot, 1]))
    return out


def _attn_sample_kernel(pt_ref, ckv_hbm, kr_hbm, cqn_ref, of_ref, cnew_ref, knew_ref, wf_ref, hm_ref,
                        tile_ref, wuv_ref, y_ref, kbuf, krbuf, k_scr, kr_scr, sem, *, layer, n_pages):
    tq = cqn_ref.shape[1]
    page = kbuf.shape[2]
    span = SAMPLE_PAGES_PER_CHUNK * page
    n_chunks = n_pages // SAMPLE_PAGES_PER_CHUNK
    step = pl.program_id(0)
    last_step = pl.num_programs(0) - 1
    copies = functools.partial(_page_copies, pt_ref, ckv_hbm, kr_hbm, kbuf, krbuf, sem,
                               layer=layer, n_pages=n_pages)

    def attend(j, slot):
        lat, rope = _build_q(cqn_ref[j], of_ref[j], wf_ref[...], hm_ref[...], tq)
        qr = _dot(rope, tile_ref[...]).astype(BF16)
        cnew = cnew_ref[j].astype(BF16)
        s_new = _dot_nt(lat, cnew) + _dot_nt(qr, knew_ref[j].astype(BF16))
        tpos = lax.broadcasted_iota(jnp.int32, s_new.shape, 0) % tq
        s_new = jnp.where(lax.broadcasted_iota(jnp.int32, s_new.shape, 1) <= tpos, s_new, NEG_INF)
        m = jnp.max(s_new, axis=-1, keepdims=True)
        p_new = jnp.exp2(s_new - m)
        l = jnp.sum(p_new, axis=-1, keepdims=True)
        acc = _dot(p_new.astype(BF16), cnew)

        def scores(c):
            for i in range(c * SAMPLE_PAGES_PER_CHUNK, (c + 1) * SAMPLE_PAGES_PER_CHUNK):
                k_scr[i * page:(i + 1) * page, :] = kbuf[slot, i].astype(BF16)
                kr_scr[:, i * page:(i + 1) * page] = krbuf[slot, i].astype(BF16)
            cols = slice(c * span, (c + 1) * span)
            return _dot_nt(lat, k_scr[cols, :]) + _dot(qr, kr_scr[:, cols])

        ahead = [scores(c) for c in range(ATT_SKEW)]
        for c in range(n_chunks):
            s = ahead.pop(0)
            if c + ATT_SKEW < n_chunks:
                ahead.append(scores(c + ATT_SKEW))
            m_new = jnp.maximum(m, jnp.max(s, axis=-1, keepdims=True))
            alpha = jnp.exp2(m - m_new)
            p = jnp.exp2(s - m_new)
            l = alpha * l + jnp.sum(p, axis=-1, keepdims=True)
            acc = alpha * acc + _dot(p.astype(BF16), k_scr[c * span:(c + 1) * span, :])
            m = m_new
        y_ref[j] = _project_out(acc / l, wuv_ref, tq)

    seq0 = 2 * step

    @pl.when(step == 0)
    def _():
        for cp in copies(seq0, 0):
            cp.start()

    for cp in copies(seq0 + 1, 1):
        cp.start()
    for cp in copies(seq0, 0):
        cp.wait()
    attend(0, 0)
    seq_next = jnp.minimum(seq0 + 2, 2 * last_step)
    for cp in copies(seq_next, 0):
        cp.start()
    for cp in copies(seq0 + 1, 1):
        cp.wait()
    attend(1, 1)

    @pl.when(step == last_step)
    def _():
        for cp in copies(seq_next, 0):
            cp.wait()


def _attn_sample(page_table, cache_ckv, cache_krope_t, layer, cqn, of, cnew, knew, wf, hm, tile_t, wuv):
    nb, tq, _ = cqn.shape
    n_pages = page_table.shape[1]
    page = cache_ckv.shape[2]
    assert nb % 2 == 0
    per_step = lambda width: pl.BlockSpec((2, tq, width), lambda i, pt: (i, 0, 0))
    const = lambda w: pl.BlockSpec(w.shape, lambda i, pt: (0,) * w.ndim)
    hbm = pl.BlockSpec(memory_space=pl.ANY)
    grid_spec = pltpu.PrefetchScalarGridSpec(
        num_scalar_prefetch=1,
        grid=(nb // 2,),
        in_specs=[hbm, hbm, per_step(Q_RANK), per_step(256), per_step(KV_RANK), per_step(ROPE_DIM),
                  const(wf), const(hm), const(tile_t), const(wuv)],
        out_specs=per_step(W_C),
        scratch_shapes=[pltpu.VMEM((2, n_pages, page, KV_RANK), F32),
                        pltpu.VMEM((2, n_pages, ROPE_DIM, page), F32),
                        pltpu.VMEM((n_pages * page, KV_RANK), BF16),
                        pltpu.VMEM((ROPE_DIM, n_pages * page), BF16),
                        pltpu.SemaphoreType.DMA((2, 2))],
    )
    return pl.pallas_call(
        functools.partial(_attn_sample_kernel, layer=layer, n_pages=n_pages),
        grid_spec=grid_spec,
        out_shape=jax.ShapeDtypeStruct((nb, tq, W_C), F32),
        compiler_params=_cparams("arbitrary"), name="attn_sample",
    )(page_table.reshape(-1), cache_ckv, cache_krope_t, cqn, of, cnew, knew, wf, hm, tile_t, wuv)


def _mla_branch_sample(z3, page_table, cache_ckv, cache_krope_t, layer, n_past, mla_w, wf, wuv, hm, tile_t):
    nb, ts, _ = z3.shape
    cos, sin = _rope_tables(n_past + jnp.arange(nb * ts) % ts)
    ckv, krt, _, cqn, of = _mla_prep(z3.reshape(1, nb * ts, Z_COLS), cos, sin, mla_w, tm=ATT_TQ, tp=nb * ts)
    ckv = ckv.reshape(nb, ts, KV_RANK)
    kr = _kr_leaf(krt.reshape(nb, ts, 256))
    yc = _attn_sample(page_table, cache_ckv, cache_krope_t, layer, cqn.reshape(nb, ts, Q_RANK),
                      of.reshape(nb, ts, 256), ckv, kr, wf, hm, tile_t, wuv)
    return yc, ckv, kr


def _out_kernel(ya_ref, yb_ref, yc_ref, gc_ref, x_ref, wa_ref, wb_ref, wc_ref, g_ref, b_ref, o_ref):
    yc = (yc_ref[...] * _silu(gc_ref[...])).astype(BF16)
    out = _dot(ya_ref[...], wa_ref[...]) + _dot(yb_ref[...], wb_ref[...]) + _dot(yc, wc_ref[...])
    h = DN_ALPHA * x_ref[...] + out
    mu = jnp.mean(h, axis=-1, keepdims=True)
    d = h - mu
    var = jnp.mean(d * d, axis=-1, keepdims=True)
    o_ref[...] = d * lax.rsqrt(var + 1e-5) * g_ref[...] + b_ref[...]


def _out_proj(ya, yb, yc, z, x, ow, tm):
    n = x.shape[0]
    wspecs = [pl.BlockSpec(w.shape, lambda i: (0,) * w.ndim) for w in ow]
    rowblk = lambda width, col=0: pl.BlockSpec((tm, width), lambda i: (i, col))
    return pl.pallas_call(
        _out_kernel,
        grid=(n // tm,),
        in_specs=[rowblk(W_A), rowblk(W_B), rowblk(W_C), rowblk(W_C, 3), rowblk(D_MODEL)] + wspecs,
        out_specs=rowblk(D_MODEL),
        out_shape=jax.ShapeDtypeStruct((n, D_MODEL), F32),
        compiler_params=_cparams("parallel"), name="out_proj",
    )(ya, yb, yc, z, x, *ow)


def _block_diag(blocks):
    nblk, bw, _ = blocks.shape
    eye = jnp.eye(nblk, dtype=blocks.dtype)
    return jnp.einsum('hij,hg->higj', blocks, eye).reshape(nblk * bw, nblk * bw)


def _pad_to(x, shape):
    return jnp.pad(x, [(0, s - d) for d, s in zip(x.shape, shape)])


def _rope_tables(pos):
    inv = ROPE_THETA ** (-2.0 * jnp.arange(ROPE_HALF, dtype=F32) / ROPE_DIM)
    ang = pos.astype(F32)[:, None] * inv[None, :]
    return jnp.tile(jnp.cos(ang), (1, H_C)), jnp.tile(jnp.sin(ang), (1, H_C))


def _static_tables():
    lane = np.arange(256)
    hm = np.stack([((lane % LANES) // ROPE_HALF == h) for h in range(H_C)]).astype(np.float32)
    hmb = np.stack([(lane // HD_B == h) for h in range(H_B)]).astype(np.float32)
    ones_bd = (lane[:, None] // HD_B == lane[None, :] // HD_B).astype(np.float32)
    tile_t = np.zeros((256, ROPE_DIM), np.float32)
    for ln in lane:
        tile_t[ln, (ln // LANES) * ROPE_HALF + ln % ROPE_HALF] = 1.0
    return jnp.asarray(hm), jnp.asarray(hmb), jnp.asarray(ones_bd), jnp.asarray(tile_t, BF16)


_PER_LAYER = ('w_in', 'conv_w', 'conv_b', 'lru_wr', 'lru_br', 'lru_wi', 'lru_bi', 'lru_lambda', 'rk_mix',
              'rk_wr', 'rk_wk', 'rk_wv', 'rk_w0', 'rk_w1', 'rk_w2', 'rk_a0', 'rk_a1', 'rk_a2', 'rk_kk',
              'rk_ka', 'rk_rk', 'rk_gn_g', 'rk_gn_b', 'q_norm', 'kv_norm', 'w_uq', 'w_uk', 'w_uv', 'w_out',
              'ln_g', 'ln_b')


def kernel(x_prompt, x_sample, cache_ckv, cache_krope, state_conv, state_lru, state_shift, state_wkv,
           page_table, meta_tokens, w_in, conv_w, conv_b, lru_wr, lru_br, lru_wi, lru_bi, lru_lambda,
           rk_mix, rk_wr, rk_wk, rk_wv, rk_w0, rk_w1, rk_w2, rk_a0, rk_a1, rk_a2, rk_kk, rk_ka, rk_rk,
           rk_gn_g, rk_gn_b, q_norm, kv_norm, w_uq, w_uk, w_uv, w_out, ln_g, ln_b):
    stacked = dict(zip(_PER_LAYER, (w_in, conv_w, conv_b, lru_wr, lru_br, lru_wi, lru_bi, lru_lambda, rk_mix,
                                    rk_wr, rk_wk, rk_wv, rk_w0, rk_w1, rk_w2, rk_a0, rk_a1, rk_a2, rk_kk,
                                    rk_ka, rk_rk, rk_gn_g, rk_gn_b, q_norm, kv_norm, w_uq, w_uk, w_uv, w_out,
                                    ln_g, ln_b)))
    bp, seq, _ = x_prompt.shape
    bs, ts, _ = x_sample.shape
    tp_len = N_META + seq
    n_past = page_table.shape[1] * cache_ckv.shape[2]
    depth = w_in.shape[0]
    assert ts == 8 and CHUNK % ts == 0 and bs % (CHUNK // ts) == 0
    assert tp_len % 24 == 0

    hm, hmb, ones_bd, tile_t = _static_tables()
    row2 = lambda v: v.reshape(1, -1)
    cache_krope_t = jnp.swapaxes(cache_krope, 2, 3)

    xp = jnp.concatenate(
        [jnp.broadcast_to(meta_tokens[None].astype(x_prompt.dtype), (bp, N_META, D_MODEL)), x_prompt],
        axis=1).reshape(bp * tp_len, D_MODEL)
    xs = x_sample.reshape(bs * ts, D_MODEL)
    tm_p = tp_len // 3
    tm_s = bs * ts

    p_states, s_states = [], []
    for l in range(depth):
        p = {name: arr[l] for name, arr in stacked.items()}
        wi = p['w_in']
        kr_cols = wi[:, 1536:1568]
        w_in_r = jnp.concatenate(
            [wi[:, 0:1536], wi[:, 1568:2080],
             jnp.tile(kr_cols[:, 0:ROPE_HALF], (1, H_C)), jnp.tile(kr_cols[:, ROPE_HALF:], (1, H_C))],
            axis=1).astype(BF16)
        lru_w = _rglru_weights(p)
        prep_w, wkv_w = _rwkv_weights(p, hmb, ones_bd)
        mla_w, wf, wuv = _mla_weights(p)
        wo = p['w_out'].astype(BF16)
        out_w = (wo[0:W_A], wo[W_A:W_A + W_B], wo[W_A + W_B:], row2(p['ln_g']), row2(p['ln_b']))

        z = _in_proj(xp, w_in_r, tm_p)
        z3 = z.reshape(bp, tp_len, Z_COLS)
        ya, lru_new = _rglru_branch_prompt(z3, lru_w)
        yb, wkv_new = _rwkv_branch_prompt(z3, prep_w, wkv_w)
        yc, ckv, kr = _mla_branch_prompt(z3, mla_w, wf, wuv, hm)
        x_new = _out_proj(ya, yb.reshape(bp * tp_len, W_B), yc.reshape(bp * tp_len, W_C), z, xp, out_w, tm_p)
        p_states.append((ckv, kr, z3[:, tp_len - (CONV_W - 1):, 0:W_A], lru_new,
                         z3[:, tp_len - 1, 2 * W_A:2 * W_A + W_B], wkv_new))
        xp = x_new

        z = _in_proj(xs, w_in_r, tm_s)
        z3 = z.reshape(bs, ts, Z_COLS)
        ya, lru_new = _rglru_branch_sample(z3, state_conv[l], state_lru[l], n_past, lru_w)
        yb, wkv_new = _rwkv_branch_sample(z3, state_shift[l], state_wkv[l], prep_w, wkv_w)
        yc, ckv, kr = _mla_branch_sample(z3, page_table, cache_ckv, cache_krope_t, l, n_past,
                                         mla_w, wf, wuv, hm, tile_t)
        x_new = _out_proj(ya, yb.reshape(bs * ts, W_B), yc.reshape(bs * ts, W_C), z, xs, out_w, tm_s)
        s_states.append((ckv, kr, z3[:, ts - (CONV_W - 1):, 0:W_A], lru_new,
                         z3[:, ts - 1, 2 * W_A:2 * W_A + W_B], wkv_new))
        xs = x_new

    y_prompt = xp.reshape(bp, tp_len, D_MODEL)[:, N_META:]
    y_sample = xs.reshape(bs, ts, D_MODEL)
    p_out = [jnp.stack([st[i] for st in p_states]) for i in range(6)]
    s_out = [jnp.stack([st[i] for st in s_states]) for i in range(6)]
    return (y_prompt, y_sample, *p_out, *s_out)
```

```python
import functools
import itertools

import numpy as np
import jax
import jax.numpy as jnp
from jax import lax
from jax.experimental import pallas as pl
from jax.experimental.pallas import tpu as pltpu

F32 = jnp.float32
BF16 = jnp.bfloat16
HIGHEST = lax.Precision.HIGHEST

D_MODEL = 1024
N_META = 16
W_A = 256
W_B = 256
W_C = 512
CONV_W = 4
LRU_C = 8.0
HD_B = 64
H_B = 4
GN_EPS = 64e-5
H_C = 8
V_DIM = 64
NOPE_DIM = 64
ROPE_DIM = 32
ROPE_HALF = 16
Q_RANK = 256
KV_RANK = 256
ROPE_THETA = 10000.0
ATTN_SCALE = (NOPE_DIM + ROPE_DIM) ** -0.5
LOG2_E = 1.4426950408889634
Q_SCALE = ATTN_SCALE * LOG2_E
NEG_INF = -1e30
DEPTH = 2
DN_ALPHA = (2 * DEPTH) ** 0.25

LANES = 128
Z_COLS = 2304
CHUNK = 64
WKV_BATCH = 4
TAIL_TILE = 512
ATT_TQ = 256
ATT_TK = 512
ATT_SKEW = 3
SAMPLE_PAGES_PER_CHUNK = 8
SAMPLE_SKEW = 3

VMEM_LIMIT = 48 * 1024 * 1024


def _cparams(*sem):
    return pltpu.CompilerParams(dimension_semantics=sem, vmem_limit_bytes=VMEM_LIMIT)


def _dot(a, b):
    return jnp.dot(a, b, preferred_element_type=F32)


def _dot_nt(a, b):
    return lax.dot_general(a, b, (((1,), (1,)), ((), ())), preferred_element_type=F32)


def _dot_tn(a, b):
    return lax.dot_general(a, b, (((0,), (0,)), ((), ())), preferred_element_type=F32)


def _dot_hi(a, b):
    return jnp.dot(a, b, preferred_element_type=F32, precision=HIGHEST)


def _split(x):
    hi = x.astype(BF16)
    return hi, (x - hi.astype(F32)).astype(BF16)


def _split3(x):
    hi, mid = _split(x)
    return hi, mid, (x - hi.astype(F32) - mid.astype(F32)).astype(BF16)


def _dot_sel(sel, b):
    b_hi, b_mid, b_lo = _split3(b)
    return _dot(sel, b_hi) + (_dot(sel, b_mid) + _dot(sel, b_lo))


def _dot_sel_r(a, sel):
    a_hi, a_mid, a_lo = _split3(a)
    return _dot(a_hi, sel) + (_dot(a_mid, sel) + _dot(a_lo, sel))


def _softplus(x):
    return jnp.maximum(x, 0.0) + jnp.log1p(jnp.exp(-jnp.abs(x)))


def _silu(x):
    return x * jax.nn.sigmoid(x)


def _lane_tiles(x):
    return [x[:, j * LANES:(j + 1) * LANES] for j in range(x.shape[1] // LANES)]


def _mm_kernel(x_ref, w_ref, o_ref):
    o_ref[...] = _dot(x_ref[...].astype(BF16), w_ref[...])


def _in_proj(x, w, tm):
    n, k = x.shape
    m = w.shape[1]
    return pl.pallas_call(
        _mm_kernel,
        grid=(n // tm,),
        in_specs=[pl.BlockSpec((tm, k), lambda i: (i, 0)), pl.BlockSpec((k, m), lambda i: (0, 0))],
        out_specs=pl.BlockSpec((tm, m), lambda i: (i, 0)),
        out_shape=jax.ShapeDtypeStruct((n, m), F32),
        compiler_params=_cparams("parallel"),
        name="in_proj",
    )(x, w)


def _rglru_kernel(*refs, seq_len, pos0, has_state):
    if has_state:
        (u_ref, g_ref, f_ref, hp_ref, cw_ref, cb_ref, wr_ref, br_ref, wi_ref, bi_ref, lam_ref,
         y_ref, h_ref) = refs
    else:
        (u_ref, g_ref, cw_ref, cb_ref, wr_ref, br_ref, wi_ref, bi_ref, lam_ref,
         y_ref, h_ref, a_scr, b_scr) = refs
    u = u_ref[...]
    rows = u.shape[0]
    row = lax.broadcasted_iota(jnp.int32, u.shape, 0)
    t = row if rows == seq_len else row & (seq_len - 1)

    cw = cw_ref[...]
    xc = cb_ref[...] + u * cw[CONV_W - 1:CONV_W]
    for k in range(1, CONV_W):
        sh = pltpu.roll(u, k, 0)
        if has_state:
            f = f_ref[...]
            prev = f if k == CONV_W - 1 else pltpu.roll(f, rows - (CONV_W - 1 - k), 0)
        else:
            prev = 0.0
        sh = jnp.where(t < k, prev, sh)
        xc = xc + sh * cw[CONV_W - 1 - k:CONV_W - k]

    xcb = xc.astype(BF16)
    gate_r = jax.nn.sigmoid(_dot(xcb, wr_ref[...]) + br_ref[...])
    gate_i = jax.nn.sigmoid(_dot(xcb, wi_ref[...]) + bi_ref[...])
    log_a = (-LRU_C) * gate_r * _softplus(-lam_ref[...])
    th = jnp.tanh(log_a)
    neg_expm1 = -2.0 * th / (1.0 - th)
    reset = (t + pos0) == 0
    a = jnp.where(reset, 0.0, jnp.exp(log_a))
    mult = jnp.where(reset, 1.0, jnp.sqrt(neg_expm1))
    bv = xc * gate_i * mult

    t8 = row & 7
    for s in (1, 2, 4):
        a_s = jnp.where(t8 >= s, pltpu.roll(a, s, 0), 1.0)
        b_s = jnp.where(t8 >= s, pltpu.roll(bv, s, 0), 0.0)
        bv = a * b_s + bv
        a = a * a_s

    if has_state:
        h = bv + a * hp_ref[...]
        h_ref[...] = h
    else:
        a_scr[...] = a
        b_scr[...] = bv

        def body(i, carry):
            r0 = pl.multiple_of(i * 8, 8)
            hh = b_scr[pl.ds(r0, 8), :] + a_scr[pl.ds(r0, 8), :] * carry
            b_scr[pl.ds(r0, 8), :] = hh
            return jnp.broadcast_to(hh[7:8, :], hh.shape)

        carry = lax.fori_loop(0, rows // 8, body, jnp.zeros((8, u.shape[1]), F32))
        h = b_scr[...]
        h_ref[...] = carry[0:1, :]
    y_ref[...] = (h * _silu(g_ref[...])).astype(BF16)


def _rglru(z, lw, *, nseq, seq_len, pos0, conv_ext=None, h_ext=None):
    n = z.shape[0]
    wspecs = [pl.BlockSpec(w.shape, lambda i: (0,) * w.ndim) for w in lw]
    if conv_ext is None:
        rows = seq_len
        in_specs = [pl.BlockSpec((rows, W_A), lambda i: (i, 0)),
                    pl.BlockSpec((rows, W_A), lambda i: (i, 1))] + wspecs
        out_specs = [pl.BlockSpec((rows, W_A), lambda i: (i, 0)),
                     pl.BlockSpec((None, 1, W_A), lambda i: (i, 0, 0))]
        out_shape = [jax.ShapeDtypeStruct((n, W_A), BF16), jax.ShapeDtypeStruct((nseq, 1, W_A), F32)]
        scratch = [pltpu.VMEM((rows, W_A), F32), pltpu.VMEM((rows, W_A), F32)]
        args = (z, z) + tuple(lw)
        grid = (nseq,)
    else:
        rows = n
        full = pl.BlockSpec((rows, W_A), lambda i: (0, 0))
        in_specs = [full, pl.BlockSpec((rows, W_A), lambda i: (0, 1)), full, full] + wspecs
        out_specs = [full, full]
        out_shape = [jax.ShapeDtypeStruct((n, W_A), BF16), jax.ShapeDtypeStruct((n, W_A), F32)]
        scratch = []
        args = (z, z, conv_ext, h_ext) + tuple(lw)
        grid = (1,)
    return pl.pallas_call(
        functools.partial(_rglru_kernel, seq_len=seq_len, pos0=pos0, has_state=conv_ext is not None),
        grid=grid, in_specs=in_specs, out_specs=out_specs, out_shape=out_shape,
        scratch_shapes=scratch, compiler_params=_cparams("parallel"), name="rglru",
    )(*args)


def _rglru_weights(p):
    row2 = lambda v: v.reshape(1, -1)
    return (p['conv_w'], row2(p['conv_b']), _block_diag(p['lru_wr']).astype(BF16), row2(p['lru_br']),
            _block_diag(p['lru_wi']).astype(BF16), row2(p['lru_bi']), row2(p['lru_lambda']))


def _rglru_branch_prompt(z3, lru_w):
    nb, t_len, _ = z3.shape
    y, h_last = _rglru(z3.reshape(nb * t_len, Z_COLS), lru_w, nseq=nb, seq_len=t_len, pos0=0)
    return y, h_last.reshape(nb, W_A)


def _rglru_branch_sample(z3, conv_prev, h_prev, pos0, lru_w):
    nb, ts, _ = z3.shape
    conv_ext = _pad_to(conv_prev, (nb, ts, W_A)).reshape(nb * ts, W_A)
    h_ext = jnp.repeat(h_prev, ts, axis=0)
    y, h_all = _rglru(z3.reshape(nb * ts, Z_COLS), lru_w, nseq=nb, seq_len=ts, pos0=pos0,
                      conv_ext=conv_ext, h_ext=h_ext)
    return y, h_all.reshape(nb, ts, W_A)[:, ts - 1]


def _rwkv_prep_kernel(u_ref, p_ref, mix_ref, wr_ref, wk_ref, wv_ref, w0_ref, w1_ref, w2_ref,
                      a0_ref, a1_ref, a2_ref, kkw_ref, ka_ref, ones_ref,
                      r_ref, lw_ref, k_ref, v_ref, kk_ref, a_ref, *, seq_len, tiles_per_seq):
    u = u_ref[...]
    row = lax.broadcasted_iota(jnp.int32, u.shape, 0)
    rolled = pltpu.roll(u, 1, 0)
    if tiles_per_seq is None:
        u_prev = jnp.where((row & (seq_len - 1)) == 0, p_ref[...], rolled)
    else:
        first = (pl.program_id(0) % tiles_per_seq) == 0
        prow = jnp.where(first, 0.0, p_ref[7:8, :])
        u_prev = jnp.where(row == 0, prow, rolled)
    du = u_prev - u
    mix = mix_ref[...]
    xr = (u + du * mix[0:1]).astype(BF16)
    xw = (u + du * mix[1:2]).astype(BF16)
    xk = (u + du * mix[2:3]).astype(BF16)
    xv = (u + du * mix[3:4]).astype(BF16)
    xa = (u + du * mix[4:5]).astype(BF16)
    r = _dot(xr, wr_ref[...])
    k = _dot(xk, wk_ref[...])
    v = _dot(xv, wv_ref[...])
    lora_w = _dot(jnp.tanh(_dot(xw, w1_ref[...])).astype(BF16), w2_ref[...])
    w_log = -_softplus(-(w0_ref[...] + lora_w)) - 0.5
    a = jax.nn.sigmoid(a0_ref[...] + _dot(_dot(xa, a1_ref[...]).astype(BF16), a2_ref[...]))
    kk = k * kkw_ref[...]
    ssq = _dot_sel_r(kk * kk, ones_ref[...])
    kk = kk / jnp.maximum(jnp.sqrt(ssq), 1e-12)
    r_ref[...] = r
    lw_ref[...] = -jnp.exp(w_log)
    k_ref[...] = k * (1.0 + (a - 1.0) * ka_ref[...])
    v_ref[...] = v
    kk_ref[...] = kk
    a_ref[...] = a


def _rwkv_prep(z, pw, *, seq_len, tm, prev_ext=None):
    n = z.shape[0]
    wspecs = [pl.BlockSpec(w.shape, lambda i: (0,) * w.ndim) for w in pw]
    if prev_ext is None:
        tiles = seq_len // tm
        p_arg = z
        p_spec = pl.BlockSpec((8, W_B), lambda i: (jnp.maximum(i * (tm // 8) - 1, 0), 2))
    else:
        tiles = None
        p_arg = prev_ext
        p_spec = pl.BlockSpec((tm, W_B), lambda i: (i, 0))
    out = jax.ShapeDtypeStruct((n, W_B), F32)
    ospec = pl.BlockSpec((tm, W_B), lambda i: (i, 0))
    return pl.pallas_call(
        functools.partial(_rwkv_prep_kernel, seq_len=seq_len, tiles_per_seq=tiles),
        grid=(n // tm,),
        in_specs=[pl.BlockSpec((tm, W_B), lambda i: (i, 2)), p_spec] + wspecs,
        out_specs=[ospec] * 6, out_shape=[out] * 6,
        compiler_params=_cparams("parallel"), name="rwkv_prep",
    )(z, p_arg, *pw)


def _wkv_kernel(r_ref, lw_ref, k_ref, v_ref, kk_ref, a_ref, g_ref, s0_ref, hm_ref, ones_ref,
                rk_ref, gng_ref, gnb_ref, y_ref, s_ref, *, t_valid, nbatch, nseq, sub_len):
    c = pl.program_id(1)

    @pl.when(c == 0)
    def _():
        s_ref[...] = s0_ref[...]

    chains = [_wkv_chunk(c, j, r_ref, lw_ref, k_ref, v_ref, kk_ref, a_ref, g_ref, hm_ref, ones_ref,
                         rk_ref, gng_ref, gnb_ref, y_ref, s_ref, t_valid=t_valid, nseq=nseq, sub_len=sub_len)
              for j in range(nbatch)]
    for _ in itertools.zip_longest(*chains):
        pass


def _wkv_chunk(c, j, r_ref, lw_ref, k_ref, v_ref, kk_ref, a_ref, g_ref, hm_ref, ones_ref,
               rk_ref, gng_ref, gnb_ref, y_ref, s_ref, *, t_valid, nseq, sub_len):
    width = W_B
    nh = H_B
    s_base = j * nseq
    row = lax.broadcasted_iota(jnp.int32, (CHUNK, width), 0)
    valid = (c * CHUNK + row) < t_valid
    zero = jnp.zeros((CHUNK, width), F32)
    r = jnp.where(valid, r_ref[j], zero)
    lw = jnp.where(valid, lw_ref[j], zero)
    k = jnp.where(valid, k_ref[j], zero)
    v = jnp.where(valid, v_ref[j], zero)
    kk = jnp.where(valid, kk_ref[j], zero)
    a = jnp.where(valid, a_ref[j], zero)

    qi = lax.broadcasted_iota(jnp.int32, (CHUNK, CHUNK), 0)
    si = lax.broadcasted_iota(jnp.int32, (CHUNK, CHUNK), 1)
    same = (qi // sub_len) == (si // sub_len)
    cum = _dot_sel(jnp.where(same & (qi >= si), 1.0, 0.0).astype(BF16), lw)
    tot = _dot_sel(jnp.where(same, 1.0, 0.0).astype(BF16), lw)
    yield
    e_neg = jnp.exp(-cum)
    e_tail = jnp.exp(tot - cum)
    rg = r * jnp.exp(cum)
    ag = -kk * jnp.exp(cum - lw)
    kka = kk * a
    bg = kka * e_neg
    kg = k * e_neg
    bt = kka * e_tail
    kt = k * e_tail

    hm = hm_ref[...]

    def stack(x):
        return jnp.concatenate([x * hm[h:h + 1] for h in range(nh)], axis=0)

    def unstack(x):
        out = x[0:CHUNK]
        for h in range(1, nh):
            out = out + x[h * CHUNK:(h + 1) * CHUNK]
        return out

    a_st = stack(ag).astype(BF16)
    r_st = stack(rg)
    b_st = stack(bg).astype(BF16)
    k_st = stack(kg).astype(BF16)
    v_st = stack(v)

    n_st = nh * CHUNK
    rq = lax.broadcasted_iota(jnp.int32, (n_st, n_st), 0) % CHUNK
    cs = lax.broadcasted_iota(jnp.int32, (n_st, n_st), 1) % CHUNK
    same_st = (rq // sub_len) == (cs // sub_len)
    strict = same_st & (rq > cs)
    incl = same_st & (rq >= cs)

    l_ab = jnp.where(strict, _dot_nt(a_st, b_st), 0.0)
    l_ak = jnp.where(strict, _dot_nt(a_st, k_st), 0.0)
    yield
    p_rb = jnp.where(incl, _dot_nt(r_st.astype(BF16), b_st), 0.0)
    p_rk = jnp.where(incl, _dot_nt(r_st.astype(BF16), k_st), 0.0)
    yield

    eye = jnp.where(lax.broadcasted_iota(jnp.int32, (n_st, n_st), 0)
                    == lax.broadcasted_iota(jnp.int32, (n_st, n_st), 1), 1.0, 0.0)
    tinv = eye + l_ab
    lp = l_ab
    levels = int(np.log2(sub_len))
    for _ in range(levels - 1):
        lpb = lp.astype(BF16)
        lp = _dot(lpb, lpb)
        yield
        tinv = tinv + _dot(lp.astype(BF16), tinv.astype(BF16))
        yield

    x0_parts, r0_parts = [], []
    for s in range(nseq):
        st = s_ref[s_base + s].astype(BF16)
        sl = slice(s * sub_len, (s + 1) * sub_len)
        rows_s = jnp.concatenate([ag[sl], rg[sl]], axis=0).astype(BF16)
        xs = _dot_nt(rows_s, st)
        x0_parts.append(xs[0:sub_len])
        r0_parts.append(xs[sub_len:2 * sub_len])
    x0 = x0_parts[0] if nseq == 1 else jnp.concatenate(x0_parts, axis=0)
    r0 = r0_parts[0] if nseq == 1 else jnp.concatenate(r0_parts, axis=0)
    yield

    x_st = stack(x0) + _dot(l_ak.astype(BF16), v_st.astype(BF16))
    yield
    u_st = _dot(tinv.astype(BF16), x_st.astype(BF16))
    yield
    o_st = stack(r0) + _dot(p_rb.astype(BF16), u_st.astype(BF16)) + _dot(p_rk.astype(BF16), v_st.astype(BF16))
    u_all = unstack(u_st)
    o = unstack(o_st)
    yield

    bd = (lax.broadcasted_iota(jnp.int32, (width, width), 0) // HD_B
          == lax.broadcasted_iota(jnp.int32, (width, width), 1) // HD_B)
    for s in range(nseq):
        sl = slice(s * sub_len, (s + 1) * sub_len)
        upd = (_dot_tn(u_all[sl].astype(BF16), bt[sl].astype(BF16))
               + _dot_tn(v[sl].astype(BF16), kt[sl].astype(BF16)))
        gam = jnp.exp(tot[s * sub_len:s * sub_len + 1])
        s_ref[s_base + s] = s_ref[s_base + s] * gam + jnp.where(bd, upd, 0.0)

    ones = ones_ref[...]
    mu = _dot_sel_r(o, ones) * (1.0 / HD_B)
    d = o - mu
    var = _dot_sel_r(d * d, ones) * (1.0 / HD_B)
    o = d * lax.rsqrt(var + GN_EPS) * gng_ref[...] + gnb_ref[...]
    o = o + _dot_sel_r(r * k * rk_ref[...], ones) * v
    y_ref[j] = (o * _silu(g_ref[j])).astype(BF16)


def _wkv(z3, prep, s0, cw, *, t_valid, nbatch, nseq, sub_len):
    nb, t_len, _ = z3.shape
    nc = -(-t_len // CHUNK)
    seq = pl.BlockSpec((nbatch, CHUNK, W_B), lambda b, c: (b, c, 0))
    wspecs = [pl.BlockSpec(w.shape, lambda b, c: (0,) * w.ndim) for w in cw]
    sspec = pl.BlockSpec((nbatch * nseq, W_B, W_B), lambda b, c: (b, 0, 0))
    return pl.pallas_call(
        functools.partial(_wkv_kernel, t_valid=t_valid, nbatch=nbatch, nseq=nseq, sub_len=sub_len),
        grid=(nb // nbatch, nc),
        in_specs=[seq] * 6 + [pl.BlockSpec((nbatch, CHUNK, W_B), lambda b, c: (b, c, 3)), sspec] + wspecs,
        out_specs=[seq, sspec],
        out_shape=[jax.ShapeDtypeStruct((nb, t_len, W_B), BF16),
                   jax.ShapeDtypeStruct(s0.shape, F32)],
        compiler_params=_cparams("parallel", "arbitrary"), name="wkv",
    )(*prep, z3, s0, *cw)


def _rwkv_weights(p, hmb, ones_bd):
    row2 = lambda v: v.reshape(1, -1)
    prep_w = (_pad_to(p['rk_mix'], (8, W_B)), p['rk_wr'].astype(BF16), p['rk_wk'].astype(BF16),
              p['rk_wv'].astype(BF16), row2(p['rk_w0']), _pad_to(p['rk_w1'], (W_B, LANES)).astype(BF16),
              _pad_to(p['rk_w2'], (LANES, W_B)).astype(BF16), row2(p['rk_a0']),
              _pad_to(p['rk_a1'], (W_B, LANES)).astype(BF16), _pad_to(p['rk_a2'], (LANES, W_B)).astype(BF16),
              row2(p['rk_kk']), row2(p['rk_ka']), ones_bd)
    wkv_w = (hmb, ones_bd, row2(p['rk_rk']), row2(p['rk_gn_g']), row2(p['rk_gn_b']))
    return prep_w, wkv_w


def _state_to_bd(s):
    rows = [jnp.pad(s[:, h], ((0, 0), (0, 0), (h * HD_B, W_B - (h + 1) * HD_B))) for h in range(H_B)]
    return jnp.concatenate(rows, axis=1)


def _state_from_bd(s):
    return jnp.stack([s[:, h * HD_B:(h + 1) * HD_B, h * HD_B:(h + 1) * HD_B] for h in range(H_B)], axis=1)


def _rwkv_branch_prompt(z3, prep_w, wkv_w):
    nb, t_len, _ = z3.shape
    prep = _rwkv_prep(z3.reshape(nb * t_len, Z_COLS), prep_w, seq_len=t_len, tm=t_len // 3)
    prep3 = tuple(a.reshape(nb, t_len, W_B) for a in prep)
    y, s_bd = _wkv(z3, prep3, jnp.zeros((nb, W_B, W_B), F32), wkv_w, t_valid=t_len,
                   nbatch=min(nb, WKV_BATCH), nseq=1, sub_len=CHUNK)
    return y, _state_from_bd(s_bd)


def _rwkv_branch_sample(z3, shift_prev, s_prev, prep_w, wkv_w):
    nb, ts, _ = z3.shape
    z = z3.reshape(nb * ts, Z_COLS)
    prep = _rwkv_prep(z, prep_w, seq_len=ts, tm=nb * ts, prev_ext=jnp.repeat(shift_prev, ts, axis=0))
    per_chunk = CHUNK // ts
    groups = nb // per_chunk
    prep3 = tuple(a.reshape(groups, CHUNK, W_B) for a in prep)
    y, s_bd = _wkv(z.reshape(groups, CHUNK, Z_COLS), prep3, _state_to_bd(s_prev), wkv_w,
                   t_valid=CHUNK, nbatch=1, nseq=per_chunk, sub_len=ts)
    return y.reshape(nb, ts, W_B), _state_from_bd(s_bd)


def _mla_prep_kernel(cq_ref, ckv_ref, za_ref, zb_ref, cos_ref, sin_ref, qn_ref, kvn_ref, wr1_ref, wr2_ref,
                     ckv_out, krt_out, kp_out, cqn_out, of_out, *, t_valid):
    tm = cq_ref.shape[0]
    row = lax.broadcasted_iota(jnp.int32, (tm, 1), 0)
    valid = (pl.program_id(1) * tm + row) < t_valid
    cq = cq_ref[...]
    cqn = cq * lax.rsqrt(jnp.mean(cq * cq, axis=-1, keepdims=True) + 1e-6) * qn_ref[...]
    cv = ckv_ref[...]
    ckv = cv * lax.rsqrt(jnp.mean(cv * cv, axis=-1, keepdims=True) + 1e-6) * kvn_ref[...]
    cos = cos_ref[...]
    sin = sin_ref[...]
    za = za_ref[...]
    zb = zb_ref[...]
    kr1 = za * cos - zb * sin
    kr2 = za * sin + zb * cos
    cqb = cqn.astype(BF16)
    q1 = _dot(cqb, wr1_ref[...])
    q2 = _dot(cqb, wr2_ref[...])
    o1 = q1 * cos - q2 * sin
    o2 = q1 * sin + q2 * cos
    ckv_out[...] = ckv
    krt_out[:, 0:LANES] = kr1
    krt_out[:, LANES:] = kr2
    kp_out[:, 0:KV_RANK] = jnp.where(valid, ckv, 0.0).astype(BF16)
    kp_out[:, KV_RANK:KV_RANK + LANES] = jnp.where(valid, kr1, 0.0).astype(BF16)
    kp_out[:, KV_RANK + LANES:] = jnp.where(valid, kr2, 0.0).astype(BF16)
    cqn_out[...] = jnp.where(valid, cqn, 0.0).astype(BF16)
    of_out[:, 0:LANES] = jnp.where(valid, o1 * Q_SCALE, 0.0).astype(BF16)
    of_out[:, LANES:] = jnp.where(valid, o2 * Q_SCALE, 0.0).astype(BF16)


def _mla_prep(z3, cos, sin, mw, *, tm, tp):
    nb, t_len, _ = z3.shape
    nt = tp // tm
    last = -(-t_len // tm) - 1
    wspecs = [pl.BlockSpec(w.shape, lambda b, j: (0,) * w.ndim) for w in mw]
    real = lambda width, col: pl.BlockSpec((None, tm, width), lambda b, j: (b, jnp.minimum(j, last), col))
    blk = lambda width: pl.BlockSpec((None, tm, width), lambda b, j: (b, j, 0))
    tab = pl.BlockSpec((tm, LANES), lambda b, j: (jnp.minimum(j, last), 0))
    return pl.pallas_call(
        functools.partial(_mla_prep_kernel, t_valid=t_len),
        grid=(nb, nt),
        in_specs=[real(256, 4), real(256, 5), real(LANES, 16), real(LANES, 17), tab, tab] + wspecs,
        out_specs=[real(256, 0), real(256, 0), blk(512), blk(256), blk(256)],
        out_shape=[jax.ShapeDtypeStruct((nb, t_len, KV_RANK), F32),
                   jax.ShapeDtypeStruct((nb, t_len, 256), F32),
                   jax.ShapeDtypeStruct((nb, tp, 512), BF16),
                   jax.ShapeDtypeStruct((nb, tp, Q_RANK), BF16),
                   jax.ShapeDtypeStruct((nb, tp, 256), BF16)],
        compiler_params=_cparams("parallel", "parallel"), name="mla_prep",
    )(z3, z3, z3, z3, cos, sin, *mw)


def _fold_kernel(wq_ref, wk_ref, o_ref):
    o_ref[...] = _dot_hi(wq_ref[...], wk_ref[...]).astype(BF16)


def _fold_q(wq_nope, wuk_t):
    return pl.pallas_call(
        _fold_kernel,
        grid=(H_C,),
        in_specs=[pl.BlockSpec((None, Q_RANK, NOPE_DIM), lambda h: (h, 0, 0)),
                  pl.BlockSpec((None, NOPE_DIM, KV_RANK), lambda h: (h, 0, 0))],
        out_specs=pl.BlockSpec((Q_RANK, KV_RANK), lambda h: (0, h)),
        out_shape=jax.ShapeDtypeStruct((Q_RANK, H_C * KV_RANK), BF16),
        compiler_params=_cparams("parallel"), name="fold_q",
    )(wq_nope, wuk_t)


def _mla_weights(p):
    row2 = lambda v: v.reshape(1, -1)
    wq = p['w_uq'].reshape(Q_RANK, H_C, NOPE_DIM + ROPE_DIM)
    wq_nope = jnp.transpose(wq[:, :, 0:NOPE_DIM], (1, 0, 2))
    wr1 = wq[:, :, NOPE_DIM:NOPE_DIM + ROPE_HALF].reshape(Q_RANK, LANES).astype(BF16)
    wr2 = wq[:, :, NOPE_DIM + ROPE_HALF:].reshape(Q_RANK, LANES).astype(BF16)
    wuk_t = jnp.transpose(p['w_uk'], (1, 2, 0))
    wf = _fold_q(wq_nope, wuk_t)
    mla_w = (row2(p['q_norm']), row2(p['kv_norm']), wr1, wr2)
    eye = jnp.eye(H_C, dtype=F32)
    wuv = jnp.einsum('rhv,hg->hrgv', p['w_uv'], eye).reshape(H_C, KV_RANK, W_C).astype(BF16)
    return mla_w, wf, wuv


def _build_q(cqn, of, wf, hm, tq):
    q_all = _dot(cqn, wf) * Q_SCALE
    lat = [q_all[:, h * KV_RANK:(h + 1) * KV_RANK].astype(BF16) for h in range(H_C)]
    rope = [of * hm[h:h + 1].astype(BF16) for h in range(H_C)]
    return jnp.concatenate(lat, axis=0), jnp.concatenate(rope, axis=0)


def _project_out(o, wuv_ref, tq):
    y = _dot(o[0:tq].astype(BF16), wuv_ref[0])
    for h in range(1, H_C):
        y = y + _dot(o[h * tq:(h + 1) * tq].astype(BF16), wuv_ref[h])
    return y


def _attn_prompt_kernel(qt_ref, kt_ref, cqn_ref, of_ref, kp_ref, wf_ref, hm_ref, wuv_ref, y_ref,
                        ql_scr, qr_scr, m_scr, acc_scr):
    tq = cqn_ref.shape[0]
    tk = kp_ref.shape[0]
    t = pl.program_id(1)
    qi = qt_ref[t]
    ki = kt_ref[t]
    last = (qi * tq + tq - 1) // tk

    @pl.when(ki == 0)
    def _():
        lat, rope = _build_q(cqn_ref[...], of_ref[...], wf_ref[...], hm_ref[...], tq)
        ql_scr[...] = lat
        qr_scr[...] = rope
        m_scr[...] = jnp.full(m_scr.shape, NEG_INF, F32)
        acc_scr[...] = jnp.zeros(acc_scr.shape, F32)

    def step(masked):
        kp = kp_ref[...]
        ckv = kp[:, 0:KV_RANK]
        krope = kp[:, KV_RANK:]
        def scores(h):
            rows = pl.ds(h * tq, tq)
            return _dot_nt(ql_scr[rows, :], ckv) + _dot_nt(qr_scr[rows, :], krope)

        ahead = [scores(h) for h in range(ATT_SKEW)]
        for h in range(H_C):
            rows = pl.ds(h * tq, tq)
            s = ahead.pop(0)
            if h + ATT_SKEW < H_C:
                ahead.append(scores(h + ATT_SKEW))
            if masked:
                qpos = qi * tq + lax.broadcasted_iota(jnp.int32, s.shape, 0)
                kpos = ki * tk + lax.broadcasted_iota(jnp.int32, s.shape, 1)
                s = jnp.where(kpos <= qpos, s, NEG_INF)
            m_old = m_scr[rows, :]
            m_new = jnp.maximum(m_old, jnp.max(s, axis=-1, keepdims=True))
            alpha = jnp.exp2(m_old - m_new)
            pt = [jnp.exp2(st - m_new) for st in _lane_tiles(s)]
            p = jnp.concatenate(pt, axis=1).astype(BF16)
            pv = jnp.concatenate([_dot(p, ckv), functools.reduce(lambda x, y: x + y, pt)], axis=1)
            acc = acc_scr[rows, :]
            acc_scr[rows, :] = jnp.concatenate([at * alpha for at in _lane_tiles(acc)], axis=1) + pv
            m_scr[rows, :] = m_new

    @pl.when(ki < last)
    def _():
        step(False)

    @pl.when(ki == last)
    def _():
        step(True)
        acc = acc_scr[...]
        l = jnp.sum(acc[:, KV_RANK:KV_RANK + LANES], axis=-1, keepdims=True)
        y_ref[...] = _project_out(acc[:, 0:KV_RANK] / l, wuv_ref, tq)


def _attn_prompt(cqn, of, kp, wf, hm, wuv, t_len):
    nb, tp, _ = kp.shape
    tq, tk = ATT_TQ, ATT_TK
    nq = -(-t_len // tq)
    pairs = [(i, j) for i in range(nq) for j in range((i * tq + tq - 1) // tk + 1)]
    q_tab = jnp.asarray([p[0] for p in pairs], jnp.int32)
    k_tab = jnp.asarray([p[1] for p in pairs], jnp.int32)
    rows = H_C * tq
    const = lambda w: pl.BlockSpec(w.shape, lambda b, t, qt, kt: (0,) * w.ndim)
    grid_spec = pltpu.PrefetchScalarGridSpec(
        num_scalar_prefetch=2,
        grid=(nb, len(pairs)),
        in_specs=[pl.BlockSpec((None, tq, Q_RANK), lambda b, t, qt, kt: (b, qt[t], 0)),
                  pl.BlockSpec((None, tq, 256), lambda b, t, qt, kt: (b, qt[t], 0)),
                  pl.BlockSpec((None, tk, 512), lambda b, t, qt, kt: (b, kt[t], 0)),
                  const(wf), const(hm), const(wuv)],
        out_specs=pl.BlockSpec((None, tq, W_C), lambda b, t, qt, kt: (b, qt[t], 0)),
        scratch_shapes=[pltpu.VMEM((rows, KV_RANK), BF16), pltpu.VMEM((rows, 256), BF16),
                        pltpu.VMEM((rows, LANES), F32), pltpu.VMEM((rows, KV_RANK + LANES), F32)],
    )
    return pl.pallas_call(
        _attn_prompt_kernel,
        grid_spec=grid_spec,
        out_shape=jax.ShapeDtypeStruct((nb, t_len, W_C), F32),
        compiler_params=_cparams("parallel", "arbitrary"), name="attn_prompt",
    )(q_tab, k_tab, cqn, of, kp, wf, hm, wuv)


def _kr_leaf(krt):
    return jnp.concatenate([krt[..., 0:ROPE_HALF], krt[..., LANES:LANES + ROPE_HALF]], axis=-1)


def _mla_branch_prompt(z3, mla_w, wf, wuv, hm):
    nb, t_len, _ = z3.shape
    tp = -(-t_len // ATT_TK) * ATT_TK
    cos, sin = _rope_tables(jnp.arange(tp))
    ckv, krt, kp, cqn, of = _mla_prep(z3, cos, sin, mla_w, tm=ATT_TQ, tp=tp)
    yc = _attn_prompt(cqn, of, kp, wf, hm, wuv, t_len)
    return yc, ckv, _kr_leaf(krt)


def _page_copies(pt_ref, ckv_hbm, kr_hbm, kbuf, krbuf, sem, seq, slot, *, layer, n_pages):
    out = []
    for pg in range(n_pages):
        pid = pt_ref[seq * n_pages + pg]
        out.append(pltpu.make_async_copy(ckv_hbm.at[layer, pid], kbuf.at[slot, pg], sem.at[slot, 0]))
        out.append(pltpu.make_async_copy(kr_hbm.at[layer, pid], krbuf.at[slot, pg], sem.at[slot, 1]))
    return out


def _attn_sample_kernel(pt_ref, ckv_hbm, kr_hbm, cqn_ref, of_ref, cnew_ref, knew_ref, wf_ref, hm_ref,
                        tile_ref, wuv_ref, y_ref, kbuf, krbuf, k_scr, kr_scr, sem, *, layer, n_pages):
    tq = cqn_ref.shape[1]
    page = kbuf.shape[2]
    span = SAMPLE_PAGES_PER_CHUNK * page
    n_chunks = n_pages // SAMPLE_PAGES_PER_CHUNK
    step = pl.program_id(0)
    last_step = pl.num_programs(0) - 1
    copies = functools.partial(_page_copies, pt_ref, ckv_hbm, kr_hbm, kbuf, krbuf, sem,
                               layer=layer, n_pages=n_pages)

    def attend(j, slot):
        lat, rope = _build_q(cqn_ref[j], of_ref[j], wf_ref[...], hm_ref[...], tq)
        qr = _dot(rope, tile_ref[...]).astype(BF16)
        cnew = cnew_ref[j].astype(BF16)
        s_new = _dot_nt(lat, cnew) + _dot_nt(qr, knew_ref[j].astype(BF16))
        tpos = lax.broadcasted_iota(jnp.int32, s_new.shape, 0) % tq
        s_new = jnp.where(lax.broadcasted_iota(jnp.int32, s_new.shape, 1) <= tpos, s_new, NEG_INF)
        m = jnp.max(s_new, axis=-1, keepdims=True)
        p_new = jnp.exp2(s_new - m)
        l = jnp.sum(p_new, axis=-1, keepdims=True)
        acc = _dot(p_new.astype(BF16), cnew)

        def scores(c):
            for i in range(c * SAMPLE_PAGES_PER_CHUNK, (c + 1) * SAMPLE_PAGES_PER_CHUNK):
                k_scr[i * page:(i + 1) * page, :] = kbuf[slot, i].astype(BF16)
                kr_scr[:, i * page:(i + 1) * page] = krbuf[slot, i].astype(BF16)
            cols = slice(c * span, (c + 1) * span)
            return _dot_nt(lat, k_scr[cols, :]) + _dot(qr, kr_scr[:, cols])

        ahead = [scores(c) for c in range(SAMPLE_SKEW)]
        for c in range(n_chunks):
            s = ahead.pop(0)
            if c + SAMPLE_SKEW < n_chunks:
                ahead.append(scores(c + SAMPLE_SKEW))
            m_new = jnp.maximum(m, jnp.max(s, axis=-1, keepdims=True))
            alpha = jnp.exp2(m - m_new)
            p = jnp.exp2(s - m_new)
            l = alpha * l + jnp.sum(p, axis=-1, keepdims=True)
            acc = alpha * acc + _dot(p.astype(BF16), k_scr[c * span:(c + 1) * span, :])
            m = m_new
        y_ref[j] = _project_out(acc / l, wuv_ref, tq)

    seq0 = 2 * step

    @pl.when(step == 0)
    def _():
        for cp in copies(seq0, 0):
            cp.start()

    for cp in copies(seq0 + 1, 1):
        cp.start()
    for cp in copies(seq0, 0):
        cp.wait()
    attend(0, 0)
    seq_next = jnp.minimum(seq0 + 2, 2 * last_step)
    for cp in copies(seq_next, 0):
        cp.start()
    for cp in copies(seq0 + 1, 1):
        cp.wait()
    attend(1, 1)

    @pl.when(step == last_step)
    def _():
        for cp in copies(seq_next, 0):
            cp.wait()


def _attn_sample(page_table, cache_ckv, cache_krope_t, layer, cqn, of, cnew, knew, wf, hm, tile_t, wuv):
    nb, tq, _ = cqn.shape
    n_pages = page_table.shape[1]
    page = cache_ckv.shape[2]
    assert nb % 2 == 0
    per_step = lambda width: pl.BlockSpec((2, tq, width), lambda i, pt: (i, 0, 0))
    const = lambda w: pl.BlockSpec(w.shape, lambda i, pt: (0,) * w.ndim)
    hbm = pl.BlockSpec(memory_space=pl.ANY)
    grid_spec = pltpu.PrefetchScalarGridSpec(
        num_scalar_prefetch=1,
        grid=(nb // 2,),
        in_specs=[hbm, hbm, per_step(Q_RANK), per_step(256), per_step(KV_RANK), per_step(ROPE_DIM),
                  const(wf), const(hm), const(tile_t), const(wuv)],
        out_specs=per_step(W_C),
        scratch_shapes=[pltpu.VMEM((2, n_pages, page, KV_RANK), F32),
                        pltpu.VMEM((2, n_pages, ROPE_DIM, page), F32),
                        pltpu.VMEM((n_pages * page, KV_RANK), BF16),
                        pltpu.VMEM((ROPE_DIM, n_pages * page), BF16),
                        pltpu.SemaphoreType.DMA((2, 2))],
    )
    return pl.pallas_call(
        functools.partial(_attn_sample_kernel, layer=layer, n_pages=n_pages),
        grid_spec=grid_spec,
        out_shape=jax.ShapeDtypeStruct((nb, tq, W_C), F32),
        compiler_params=_cparams("arbitrary"), name="attn_sample",
    )(page_table.reshape(-1), cache_ckv, cache_krope_t, cqn, of, cnew, knew, wf, hm, tile_t, wuv)


def _mla_branch_sample(z3, page_table, cache_ckv, cache_krope_t, layer, n_past, mla_w, wf, wuv, hm, tile_t):
    nb, ts, _ = z3.shape
    cos, sin = _rope_tables(n_past + jnp.arange(nb * ts) % ts)
    ckv, krt, _, cqn, of = _mla_prep(z3.reshape(1, nb * ts, Z_COLS), cos, sin, mla_w, tm=ATT_TQ, tp=nb * ts)
    ckv = ckv.reshape(nb, ts, KV_RANK)
    kr = _kr_leaf(krt.reshape(nb, ts, 256))
    yc = _attn_sample(page_table, cache_ckv, cache_krope_t, layer, cqn.reshape(nb, ts, Q_RANK),
                      of.reshape(nb, ts, 256), ckv, kr, wf, hm, tile_t, wuv)
    return yc, ckv, kr


def _out_kernel(ya_ref, yb_ref, yc_ref, gc_ref, x_ref, wa_ref, wb_ref, wc_ref, g_ref, b_ref, o_ref):
    yc = (yc_ref[...] * _silu(gc_ref[...])).astype(BF16)
    out = _dot(ya_ref[...], wa_ref[...]) + _dot(yb_ref[...], wb_ref[...]) + _dot(yc, wc_ref[...])
    h = DN_ALPHA * x_ref[...] + out
    mu = jnp.mean(h, axis=-1, keepdims=True)
    d = h - mu
    var = jnp.mean(d * d, axis=-1, keepdims=True)
    o_ref[...] = d * lax.rsqrt(var + 1e-5) * g_ref[...] + b_ref[...]


def _out_proj(ya, yb, yc, z, x, ow, tm):
    n = x.shape[0]
    wspecs = [pl.BlockSpec(w.shape, lambda i: (0,) * w.ndim) for w in ow]
    rowblk = lambda width, col=0: pl.BlockSpec((tm, width), lambda i: (i, col))
    return pl.pallas_call(
        _out_kernel,
        grid=(n // tm,),
        in_specs=[rowblk(W_A), rowblk(W_B), rowblk(W_C), rowblk(W_C, 3), rowblk(D_MODEL)] + wspecs,
        out_specs=rowblk(D_MODEL),
        out_shape=jax.ShapeDtypeStruct((n, D_MODEL), F32),
        compiler_params=_cparams("parallel"), name="out_proj",
    )(ya, yb, yc, z, x, *ow)


def _out_proj_tail(ya, yb, yc, z3, x, ow, *, skip, tm):
    nb, t_len, _ = z3.shape
    keep = t_len - skip
    seq3 = lambda a: a.reshape(nb, t_len, a.shape[-1])
    wspecs = [pl.BlockSpec(w.shape, lambda b, j: (0,) * w.ndim) for w in ow]
    rows = lambda width, col=0: pl.BlockSpec((pl.Element(1), pl.Element(tm), pl.Element(width)),
                                             lambda b, j: (b, pl.multiple_of(skip + j * tm, skip), col * width))

    def tail_kernel(ya_ref, yb_ref, yc_ref, gc_ref, x_ref, *rest):
        _out_kernel(ya_ref.at[0], yb_ref.at[0], yc_ref.at[0], gc_ref.at[0], x_ref.at[0], *rest)

    return pl.pallas_call(
        tail_kernel,
        grid=(nb, keep // tm),
        in_specs=[rows(W_A), rows(W_B), rows(W_C), rows(W_C, 3), rows(D_MODEL)] + wspecs,
        out_specs=pl.BlockSpec((None, tm, D_MODEL), lambda b, j: (b, j, 0)),
        out_shape=jax.ShapeDtypeStruct((nb, keep, D_MODEL), F32),
        compiler_params=_cparams("parallel", "parallel"), name="out_proj_tail",
    )(seq3(ya), seq3(yb), seq3(yc), z3, seq3(x), *ow)


def _block_diag(blocks):
    nblk, bw, _ = blocks.shape
    eye = jnp.eye(nblk, dtype=blocks.dtype)
    return jnp.einsum('hij,hg->higj', blocks, eye).reshape(nblk * bw, nblk * bw)


def _pad_to(x, shape):
    return jnp.pad(x, [(0, s - d) for d, s in zip(x.shape, shape)])


def _rope_tables(pos):
    inv = ROPE_THETA ** (-2.0 * jnp.arange(ROPE_HALF, dtype=F32) / ROPE_DIM)
    ang = pos.astype(F32)[:, None] * inv[None, :]
    return jnp.tile(jnp.cos(ang), (1, H_C)), jnp.tile(jnp.sin(ang), (1, H_C))


def _static_tables():
    lane = np.arange(256)
    hm = np.stack([((lane % LANES) // ROPE_HALF == h) for h in range(H_C)]).astype(np.float32)
    hmb = np.stack([(lane // HD_B == h) for h in range(H_B)]).astype(np.float32)
    ones_bd = (lane[:, None] // HD_B == lane[None, :] // HD_B).astype(np.float32)
    tile_t = np.zeros((256, ROPE_DIM), np.float32)
    for ln in lane:
        tile_t[ln, (ln // LANES) * ROPE_HALF + ln % ROPE_HALF] = 1.0
    return jnp.asarray(hm), jnp.asarray(hmb), jnp.asarray(ones_bd, BF16), jnp.asarray(tile_t, BF16)


_PER_LAYER = ('w_in', 'conv_w', 'conv_b', 'lru_wr', 'lru_br', 'lru_wi', 'lru_bi', 'lru_lambda', 'rk_mix',
              'rk_wr', 'rk_wk', 'rk_wv', 'rk_w0', 'rk_w1', 'rk_w2', 'rk_a0', 'rk_a1', 'rk_a2', 'rk_kk',
              'rk_ka', 'rk_rk', 'rk_gn_g', 'rk_gn_b', 'q_norm', 'kv_norm', 'w_uq', 'w_uk', 'w_uv', 'w_out',
              'ln_g', 'ln_b')


def kernel(x_prompt, x_sample, cache_ckv, cache_krope, state_conv, state_lru, state_shift, state_wkv,
           page_table, meta_tokens, w_in, conv_w, conv_b, lru_wr, lru_br, lru_wi, lru_bi, lru_lambda,
           rk_mix, rk_wr, rk_wk, rk_wv, rk_w0, rk_w1, rk_w2, rk_a0, rk_a1, rk_a2, rk_kk, rk_ka, rk_rk,
           rk_gn_g, rk_gn_b, q_norm, kv_norm, w_uq, w_uk, w_uv, w_out, ln_g, ln_b):
    stacked = dict(zip(_PER_LAYER, (w_in, conv_w, conv_b, lru_wr, lru_br, lru_wi, lru_bi, lru_lambda, rk_mix,
                                    rk_wr, rk_wk, rk_wv, rk_w0, rk_w1, rk_w2, rk_a0, rk_a1, rk_a2, rk_kk,
                                    rk_ka, rk_rk, rk_gn_g, rk_gn_b, q_norm, kv_norm, w_uq, w_uk, w_uv, w_out,
                                    ln_g, ln_b)))
    bp, seq, _ = x_prompt.shape
    bs, ts, _ = x_sample.shape
    tp_len = N_META + seq
    n_past = page_table.shape[1] * cache_ckv.shape[2]
    depth = w_in.shape[0]
    assert ts == 8 and CHUNK % ts == 0 and bs % (CHUNK // ts) == 0
    assert tp_len % 24 == 0 and seq % TAIL_TILE == 0 and N_META % 8 == 0

    hm, hmb, ones_bd, tile_t = _static_tables()
    row2 = lambda v: v.reshape(1, -1)
    cache_krope_t = jnp.swapaxes(cache_krope, 2, 3)

    xp = jnp.concatenate(
        [jnp.broadcast_to(meta_tokens[None].astype(x_prompt.dtype), (bp, N_META, D_MODEL)), x_prompt],
        axis=1).reshape(bp * tp_len, D_MODEL)
    xs = x_sample.reshape(bs * ts, D_MODEL)
    tm_p = tp_len // 3
    tm_s = bs * ts

    p_states, s_states = [], []
    for l in range(depth):
        p = {name: arr[l] for name, arr in stacked.items()}
        wi = p['w_in']
        kr_cols = wi[:, 1536:1568]
        w_in_r = jnp.concatenate(
            [wi[:, 0:1536], wi[:, 1568:2080],
             jnp.tile(kr_cols[:, 0:ROPE_HALF], (1, H_C)), jnp.tile(kr_cols[:, ROPE_HALF:], (1, H_C))],
            axis=1).astype(BF16)
        lru_w = _rglru_weights(p)
        prep_w, wkv_w = _rwkv_weights(p, hmb, ones_bd)
        mla_w, wf, wuv = _mla_weights(p)
        wo = p['w_out'].astype(BF16)
        out_w = (wo[0:W_A], wo[W_A:W_A + W_B], wo[W_A + W_B:], row2(p['ln_g']), row2(p['ln_b']))

        z = _in_proj(xp, w_in_r, tm_p)
        z3 = z.reshape(bp, tp_len, Z_COLS)
        ya, lru_new = _rglru_branch_prompt(z3, lru_w)
        yb, wkv_new = _rwkv_branch_prompt(z3, prep_w, wkv_w)
        yc, ckv, kr = _mla_branch_prompt(z3, mla_w, wf, wuv, hm)
        if l + 1 < depth:
            xp = _out_proj(ya, yb.reshape(bp * tp_len, W_B), yc.reshape(bp * tp_len, W_C), z, xp, out_w, tm_p)
        else:
            y_prompt = _out_proj_tail(ya, yb, yc, z3, xp, out_w, skip=N_META, tm=TAIL_TILE)
        p_states.append((ckv, kr, z3[:, tp_len - (CONV_W - 1):, 0:W_A], lru_new,
                         z3[:, tp_len - 1, 2 * W_A:2 * W_A + W_B], wkv_new))

        z = _in_proj(xs, w_in_r, tm_s)
        z3 = z.reshape(bs, ts, Z_COLS)
        ya, lru_new = _rglru_branch_sample(z3, state_conv[l], state_lru[l], n_past, lru_w)
        yb, wkv_new = _rwkv_branch_sample(z3, state_shift[l], state_wkv[l], prep_w, wkv_w)
        yc, ckv, kr = _mla_branch_sample(z3, page_table, cache_ckv, cache_krope_t, l, n_past,
                                         mla_w, wf, wuv, hm, tile_t)
        x_new = _out_proj(ya, yb.reshape(bs * ts, W_B), yc.reshape(bs * ts, W_C), z, xs, out_w, tm_s)
        s_states.append((ckv, kr, z3[:, ts - (CONV_W - 1):, 0:W_A], lru_new,
                         z3[:, ts - 1, 2 * W_A:2 * W_A + W_B], wkv_new))
        xs = x_new

    y_sample = xs.reshape(bs, ts, D_MODEL)
    p_out = [jnp.stack([st[i] for st in p_states]) for i in range(6)]
    s_out = [jnp.stack([st[i] for st in s_states]) for i in range(6)]
    return (y_prompt, y_sample, *p_out, *s_out)
```

```python
import functools
import itertools

import numpy as np
import jax
import jax.numpy as jnp
from jax import lax
from jax.experimental import pallas as pl
from jax.experimental.pallas import tpu as pltpu

F32 = jnp.float32
BF16 = jnp.bfloat16
HIGHEST = lax.Precision.HIGHEST

D_MODEL = 1024
N_META = 16
W_A = 256
W_B = 256
W_C = 512
CONV_W = 4
LRU_C = 8.0
HD_B = 64
H_B = 4
GN_EPS = 64e-5
H_C = 8
V_DIM = 64
NOPE_DIM = 64
ROPE_DIM = 32
ROPE_HALF = 16
Q_RANK = 256
KV_RANK = 256
ROPE_THETA = 10000.0
ATTN_SCALE = (NOPE_DIM + ROPE_DIM) ** -0.5
LOG2_E = 1.4426950408889634
Q_SCALE = ATTN_SCALE * LOG2_E
NEG_INF = -1e30
DEPTH = 2
DN_ALPHA = (2 * DEPTH) ** 0.25

LANES = 128
Z_COLS = 2304
CHUNK = 64
WKV_BATCH = 4
TAIL_TILE = 512
ATT_TQ = 256
ATT_TK = 512
ATT_SKEW = 3
SAMPLE_PAGES_PER_CHUNK = 8
SAMPLE_SKEW = 3

VMEM_LIMIT = 48 * 1024 * 1024


def _cparams(*sem):
    return pltpu.CompilerParams(dimension_semantics=sem, vmem_limit_bytes=VMEM_LIMIT)


def _dot(a, b):
    return jnp.dot(a, b, preferred_element_type=F32)


def _dot_nt(a, b):
    return lax.dot_general(a, b, (((1,), (1,)), ((), ())), preferred_element_type=F32)


def _dot_tn(a, b):
    return lax.dot_general(a, b, (((0,), (0,)), ((), ())), preferred_element_type=F32)


def _dot_hi(a, b):
    return jnp.dot(a, b, preferred_element_type=F32, precision=HIGHEST)


def _split(x):
    hi = x.astype(BF16)
    return hi, (x - hi.astype(F32)).astype(BF16)


def _split3(x):
    hi, mid = _split(x)
    return hi, mid, (x - hi.astype(F32) - mid.astype(F32)).astype(BF16)


def _dot_sel(sel, b):
    b_hi, b_mid, b_lo = _split3(b)
    return _dot(sel, b_hi) + (_dot(sel, b_mid) + _dot(sel, b_lo))


def _dot_sel_r(a, sel):
    a_hi, a_mid, a_lo = _split3(a)
    return _dot(a_hi, sel) + (_dot(a_mid, sel) + _dot(a_lo, sel))


def _softplus(x):
    return jnp.maximum(x, 0.0) + jnp.log1p(jnp.exp(-jnp.abs(x)))


def _silu(x):
    return x * jax.nn.sigmoid(x)


def _lane_tiles(x):
    return [x[:, j * LANES:(j + 1) * LANES] for j in range(x.shape[1] // LANES)]


def _mm_kernel(x_ref, w_ref, o_ref):
    o_ref[...] = _dot(x_ref[...].astype(BF16), w_ref[...])


def _in_proj(x, w, tm):
    n, k = x.shape
    m = w.shape[1]
    return pl.pallas_call(
        _mm_kernel,
        grid=(n // tm,),
        in_specs=[pl.BlockSpec((tm, k), lambda i: (i, 0)), pl.BlockSpec((k, m), lambda i: (0, 0))],
        out_specs=pl.BlockSpec((tm, m), lambda i: (i, 0)),
        out_shape=jax.ShapeDtypeStruct((n, m), F32),
        compiler_params=_cparams("parallel"),
        name="in_proj",
    )(x, w)


def _rglru_kernel(*refs, seq_len, pos0, has_state):
    if has_state:
        (u_ref, g_ref, f_ref, hp_ref, cw_ref, cb_ref, wr_ref, br_ref, wi_ref, bi_ref, lam_ref,
         y_ref, h_ref) = refs
    else:
        (u_ref, g_ref, cw_ref, cb_ref, wr_ref, br_ref, wi_ref, bi_ref, lam_ref,
         y_ref, h_ref, a_scr, b_scr) = refs
    u = u_ref[...]
    rows = u.shape[0]
    row = lax.broadcasted_iota(jnp.int32, u.shape, 0)
    t = row if rows == seq_len else row & (seq_len - 1)

    cw = cw_ref[...]
    xc = cb_ref[...] + u * cw[CONV_W - 1:CONV_W]
    for k in range(1, CONV_W):
        sh = pltpu.roll(u, k, 0)
        if has_state:
            f = f_ref[...]
            prev = f if k == CONV_W - 1 else pltpu.roll(f, rows - (CONV_W - 1 - k), 0)
        else:
            prev = 0.0
        sh = jnp.where(t < k, prev, sh)
        xc = xc + sh * cw[CONV_W - 1 - k:CONV_W - k]

    xcb = xc.astype(BF16)
    gate_r = jax.nn.sigmoid(_dot(xcb, wr_ref[...]) + br_ref[...])
    gate_i = jax.nn.sigmoid(_dot(xcb, wi_ref[...]) + bi_ref[...])
    log_a = (-LRU_C) * gate_r * _softplus(-lam_ref[...])
    th = jnp.tanh(log_a)
    neg_expm1 = -2.0 * th / (1.0 - th)
    reset = (t + pos0) == 0
    a = jnp.where(reset, 0.0, jnp.exp(log_a))
    mult = jnp.where(reset, 1.0, jnp.sqrt(neg_expm1))
    bv = xc * gate_i * mult

    t8 = row & 7
    for s in (1, 2, 4):
        a_s = jnp.where(t8 >= s, pltpu.roll(a, s, 0), 1.0)
        b_s = jnp.where(t8 >= s, pltpu.roll(bv, s, 0), 0.0)
        bv = a * b_s + bv
        a = a * a_s

    if has_state:
        h = bv + a * hp_ref[...]
        h_ref[...] = h
    else:
        a_scr[...] = a
        b_scr[...] = bv

        def body(i, carry):
            r0 = pl.multiple_of(i * 8, 8)
            hh = b_scr[pl.ds(r0, 8), :] + a_scr[pl.ds(r0, 8), :] * carry
            b_scr[pl.ds(r0, 8), :] = hh
            return jnp.broadcast_to(hh[7:8, :], hh.shape)

        carry = lax.fori_loop(0, rows // 8, body, jnp.zeros((8, u.shape[1]), F32))
        h = b_scr[...]
        h_ref[...] = carry[0:1, :]
    y_ref[...] = (h * _silu(g_ref[...])).astype(BF16)


def _rglru(z, lw, *, nseq, seq_len, pos0, conv_ext=None, h_ext=None):
    n = z.shape[0]
    wspecs = [pl.BlockSpec(w.shape, lambda i: (0,) * w.ndim) for w in lw]
    if conv_ext is None:
        rows = seq_len
        in_specs = [pl.BlockSpec((rows, W_A), lambda i: (i, 0)),
                    pl.BlockSpec((rows, W_A), lambda i: (i, 1))] + wspecs
        out_specs = [pl.BlockSpec((rows, W_A), lambda i: (i, 0)),
                     pl.BlockSpec((None, 1, W_A), lambda i: (i, 0, 0))]
        out_shape = [jax.ShapeDtypeStruct((n, W_A), BF16), jax.ShapeDtypeStruct((nseq, 1, W_A), F32)]
        scratch = [pltpu.VMEM((rows, W_A), F32), pltpu.VMEM((rows, W_A), F32)]
        args = (z, z) + tuple(lw)
        grid = (nseq,)
    else:
        rows = n
        full = pl.BlockSpec((rows, W_A), lambda i: (0, 0))
        in_specs = [full, pl.BlockSpec((rows, W_A), lambda i: (0, 1)), full, full] + wspecs
        out_specs = [full, full]
        out_shape = [jax.ShapeDtypeStruct((n, W_A), BF16), jax.ShapeDtypeStruct((n, W_A), F32)]
        scratch = []
        args = (z, z, conv_ext, h_ext) + tuple(lw)
        grid = (1,)
    return pl.pallas_call(
        functools.partial(_rglru_kernel, seq_len=seq_len, pos0=pos0, has_state=conv_ext is not None),
        grid=grid, in_specs=in_specs, out_specs=out_specs, out_shape=out_shape,
        scratch_shapes=scratch, compiler_params=_cparams("parallel"), name="rglru",
    )(*args)


def _rglru_weights(p):
    row2 = lambda v: v.reshape(1, -1)
    return (p['conv_w'], row2(p['conv_b']), _block_diag(p['lru_wr']).astype(BF16), row2(p['lru_br']),
            _block_diag(p['lru_wi']).astype(BF16), row2(p['lru_bi']), row2(p['lru_lambda']))


def _rglru_branch_prompt(z3, lru_w):
    nb, t_len, _ = z3.shape
    y, h_last = _rglru(z3.reshape(nb * t_len, Z_COLS), lru_w, nseq=nb, seq_len=t_len, pos0=0)
    return y, h_last.reshape(nb, W_A)


def _rglru_branch_sample(z3, conv_prev, h_prev, pos0, lru_w):
    nb, ts, _ = z3.shape
    conv_ext = _pad_to(conv_prev, (nb, ts, W_A)).reshape(nb * ts, W_A)
    h_ext = jnp.repeat(h_prev, ts, axis=0)
    y, h_all = _rglru(z3.reshape(nb * ts, Z_COLS), lru_w, nseq=nb, seq_len=ts, pos0=pos0,
                      conv_ext=conv_ext, h_ext=h_ext)
    return y, h_all.reshape(nb, ts, W_A)[:, ts - 1]


def _rwkv_prep_kernel(u_ref, p_ref, mix_ref, wr_ref, wk_ref, wv_ref, w0_ref, w1_ref, w2_ref,
                      a0_ref, a1_ref, a2_ref, kkw_ref, ka_ref, ones_ref,
                      r_ref, lw_ref, k_ref, v_ref, kk_ref, a_ref, *, seq_len, tiles_per_seq):
    u = u_ref[...]
    row = lax.broadcasted_iota(jnp.int32, u.shape, 0)
    rolled = pltpu.roll(u, 1, 0)
    if tiles_per_seq is None:
        u_prev = jnp.where((row & (seq_len - 1)) == 0, p_ref[...], rolled)
    else:
        first = (pl.program_id(0) % tiles_per_seq) == 0
        prow = jnp.where(first, 0.0, p_ref[7:8, :])
        u_prev = jnp.where(row == 0, prow, rolled)
    du = u_prev - u
    mix = mix_ref[...]
    xr = (u + du * mix[0:1]).astype(BF16)
    xw = (u + du * mix[1:2]).astype(BF16)
    xk = (u + du * mix[2:3]).astype(BF16)
    xv = (u + du * mix[3:4]).astype(BF16)
    xa = (u + du * mix[4:5]).astype(BF16)
    r = _dot(xr, wr_ref[...])
    k = _dot(xk, wk_ref[...])
    v = _dot(xv, wv_ref[...])
    lora_w = _dot(jnp.tanh(_dot(xw, w1_ref[...])).astype(BF16), w2_ref[...])
    w_log = -_softplus(-(w0_ref[...] + lora_w)) - 0.5
    a = jax.nn.sigmoid(a0_ref[...] + _dot(_dot(xa, a1_ref[...]).astype(BF16), a2_ref[...]))
    kk = k * kkw_ref[...]
    ssq = _dot_sel_r(kk * kk, ones_ref[...])
    kk = kk / jnp.maximum(jnp.sqrt(ssq), 1e-12)
    r_ref[...] = r
    lw_ref[...] = -jnp.exp(w_log)
    k_ref[...] = k * (1.0 + (a - 1.0) * ka_ref[...])
    v_ref[...] = v
    kk_ref[...] = kk
    a_ref[...] = a


def _rwkv_prep(z, pw, *, seq_len, tm, prev_ext=None):
    n = z.shape[0]
    wspecs = [pl.BlockSpec(w.shape, lambda i: (0,) * w.ndim) for w in pw]
    if prev_ext is None:
        tiles = seq_len // tm
        p_arg = z
        p_spec = pl.BlockSpec((8, W_B), lambda i: (jnp.maximum(i * (tm // 8) - 1, 0), 2))
    else:
        tiles = None
        p_arg = prev_ext
        p_spec = pl.BlockSpec((tm, W_B), lambda i: (i, 0))
    out = jax.ShapeDtypeStruct((n, W_B), F32)
    ospec = pl.BlockSpec((tm, W_B), lambda i: (i, 0))
    return pl.pallas_call(
        functools.partial(_rwkv_prep_kernel, seq_len=seq_len, tiles_per_seq=tiles),
        grid=(n // tm,),
        in_specs=[pl.BlockSpec((tm, W_B), lambda i: (i, 2)), p_spec] + wspecs,
        out_specs=[ospec] * 6, out_shape=[out] * 6,
        compiler_params=_cparams("parallel"), name="rwkv_prep",
    )(z, p_arg, *pw)


def _wkv_kernel(r_ref, lw_ref, k_ref, v_ref, kk_ref, a_ref, g_ref, s0_ref, hm_ref, ones_ref,
                rk_ref, gng_ref, gnb_ref, y_ref, s_ref, *, t_valid, nbatch, nseq, sub_len):
    c = pl.program_id(1)

    @pl.when(c == 0)
    def _():
        s_ref[...] = s0_ref[...]

    chains = [_wkv_chunk(c, j, r_ref, lw_ref, k_ref, v_ref, kk_ref, a_ref, g_ref, hm_ref, ones_ref,
                         rk_ref, gng_ref, gnb_ref, y_ref, s_ref, t_valid=t_valid, nseq=nseq, sub_len=sub_len)
              for j in range(nbatch)]
    for _ in itertools.zip_longest(*chains):
        pass


def _wkv_chunk(c, j, r_ref, lw_ref, k_ref, v_ref, kk_ref, a_ref, g_ref, hm_ref, ones_ref,
               rk_ref, gng_ref, gnb_ref, y_ref, s_ref, *, t_valid, nseq, sub_len):
    width = W_B
    nh = H_B
    s_base = j * nseq
    row = lax.broadcasted_iota(jnp.int32, (CHUNK, width), 0)
    valid = (c * CHUNK + row) < t_valid
    zero = jnp.zeros((CHUNK, width), F32)
    r = jnp.where(valid, r_ref[j], zero)
    lw = jnp.where(valid, lw_ref[j], zero)
    k = jnp.where(valid, k_ref[j], zero)
    v = jnp.where(valid, v_ref[j], zero)
    kk = jnp.where(valid, kk_ref[j], zero)
    a = jnp.where(valid, a_ref[j], zero)

    qi = lax.broadcasted_iota(jnp.int32, (CHUNK, CHUNK), 0)
    si = lax.broadcasted_iota(jnp.int32, (CHUNK, CHUNK), 1)
    same = (qi // sub_len) == (si // sub_len)
    cum = _dot_sel(jnp.where(same & (qi >= si), 1.0, 0.0).astype(BF16), lw)
    tot = _dot_sel(jnp.where(same, 1.0, 0.0).astype(BF16), lw)
    yield
    e_neg = jnp.exp(-cum)
    e_tail = jnp.exp(tot - cum)
    rg = r * jnp.exp(cum)
    ag = -kk * jnp.exp(cum - lw)
    kka = kk * a
    bg = kka * e_neg
    kg = k * e_neg
    bt = kka * e_tail
    kt = k * e_tail

    hm = hm_ref[...]

    def stack(x):
        return jnp.concatenate([x * hm[h:h + 1] for h in range(nh)], axis=0)

    def unstack(x):
        out = x[0:CHUNK]
        for h in range(1, nh):
            out = out + x[h * CHUNK:(h + 1) * CHUNK]
        return out

    a_st = stack(ag).astype(BF16)
    r_st = stack(rg)
    b_st = stack(bg).astype(BF16)
    k_st = stack(kg).astype(BF16)
    v_st = stack(v)

    n_st = nh * CHUNK
    rq = lax.broadcasted_iota(jnp.int32, (n_st, n_st), 0) % CHUNK
    cs = lax.broadcasted_iota(jnp.int32, (n_st, n_st), 1) % CHUNK
    same_st = (rq // sub_len) == (cs // sub_len)
    strict = same_st & (rq > cs)
    incl = same_st & (rq >= cs)

    l_ab = jnp.where(strict, _dot_nt(a_st, b_st), 0.0)
    l_ak = jnp.where(strict, _dot_nt(a_st, k_st), 0.0)
    yield
    p_rb = jnp.where(incl, _dot_nt(r_st.astype(BF16), b_st), 0.0)
    p_rk = jnp.where(incl, _dot_nt(r_st.astype(BF16), k_st), 0.0)
    yield

    eye = jnp.where(lax.broadcasted_iota(jnp.int32, (n_st, n_st), 0)
                    == lax.broadcasted_iota(jnp.int32, (n_st, n_st), 1), 1.0, 0.0)
    tinv = eye + l_ab
    lp = l_ab
    levels = int(np.log2(sub_len))
    for _ in range(levels - 1):
        lpb = lp.astype(BF16)
        lp = _dot(lpb, lpb)
        yield
        tinv = tinv + _dot(lp.astype(BF16), tinv.astype(BF16))
        yield

    x0_parts, r0_parts = [], []
    for s in range(nseq):
        st = s_ref[s_base + s].astype(BF16)
        sl = slice(s * sub_len, (s + 1) * sub_len)
        rows_s = jnp.concatenate([ag[sl], rg[sl]], axis=0).astype(BF16)
        xs = _dot_nt(rows_s, st)
        x0_parts.append(xs[0:sub_len])
        r0_parts.append(xs[sub_len:2 * sub_len])
    x0 = x0_parts[0] if nseq == 1 else jnp.concatenate(x0_parts, axis=0)
    r0 = r0_parts[0] if nseq == 1 else jnp.concatenate(r0_parts, axis=0)
    yield

    x_st = stack(x0) + _dot(l_ak.astype(BF16), v_st.astype(BF16))
    yield
    u_st = _dot(tinv.astype(BF16), x_st.astype(BF16))
    yield
    o_st = stack(r0) + _dot(p_rb.astype(BF16), u_st.astype(BF16)) + _dot(p_rk.astype(BF16), v_st.astype(BF16))
    u_all = unstack(u_st)
    o = unstack(o_st)
    yield

    bd = (lax.broadcasted_iota(jnp.int32, (width, width), 0) // HD_B
          == lax.broadcasted_iota(jnp.int32, (width, width), 1) // HD_B)
    for s in range(nseq):
        sl = slice(s * sub_len, (s + 1) * sub_len)
        upd = (_dot_tn(u_all[sl].astype(BF16), bt[sl].astype(BF16))
               + _dot_tn(v[sl].astype(BF16), kt[sl].astype(BF16)))
        gam = jnp.exp(tot[s * sub_len:s * sub_len + 1])
        s_ref[s_base + s] = s_ref[s_base + s] * gam + jnp.where(bd, upd, 0.0)

    ones = ones_ref[...]
    mu = _dot_sel_r(o, ones) * (1.0 / HD_B)
    d = o - mu
    var = _dot_sel_r(d * d, ones) * (1.0 / HD_B)
    o = d * lax.rsqrt(var + GN_EPS) * gng_ref[...] + gnb_ref[...]
    o = o + _dot_sel_r(r * k * rk_ref[...], ones) * v
    y_ref[j] = (o * _silu(g_ref[j])).astype(BF16)


def _wkv(z3, prep, s0, cw, *, t_valid, nbatch, nseq, sub_len):
    nb, t_len, _ = z3.shape
    nc = -(-t_len // CHUNK)
    seq = pl.BlockSpec((nbatch, CHUNK, W_B), lambda b, c: (b, c, 0))
    wspecs = [pl.BlockSpec(w.shape, lambda b, c: (0,) * w.ndim) for w in cw]
    sspec = pl.BlockSpec((nbatch * nseq, W_B, W_B), lambda b, c: (b, 0, 0))
    return pl.pallas_call(
        functools.partial(_wkv_kernel, t_valid=t_valid, nbatch=nbatch, nseq=nseq, sub_len=sub_len),
        grid=(nb // nbatch, nc),
        in_specs=[seq] * 6 + [pl.BlockSpec((nbatch, CHUNK, W_B), lambda b, c: (b, c, 3)), sspec] + wspecs,
        out_specs=[seq, sspec],
        out_shape=[jax.ShapeDtypeStruct((nb, t_len, W_B), BF16),
                   jax.ShapeDtypeStruct(s0.shape, F32)],
        compiler_params=_cparams("parallel", "arbitrary"), name="wkv",
    )(*prep, z3, s0, *cw)


def _rwkv_weights(p, hmb, ones_bd):
    row2 = lambda v: v.reshape(1, -1)
    prep_w = (_pad_to(p['rk_mix'], (8, W_B)), p['rk_wr'].astype(BF16), p['rk_wk'].astype(BF16),
              p['rk_wv'].astype(BF16), row2(p['rk_w0']), _pad_to(p['rk_w1'], (W_B, LANES)).astype(BF16),
              _pad_to(p['rk_w2'], (LANES, W_B)).astype(BF16), row2(p['rk_a0']),
              _pad_to(p['rk_a1'], (W_B, LANES)).astype(BF16), _pad_to(p['rk_a2'], (LANES, W_B)).astype(BF16),
              row2(p['rk_kk']), row2(p['rk_ka']), ones_bd)
    wkv_w = (hmb, ones_bd, row2(p['rk_rk']), row2(p['rk_gn_g']), row2(p['rk_gn_b']))
    return prep_w, wkv_w


def _state_to_bd(s):
    rows = [jnp.pad(s[:, h], ((0, 0), (0, 0), (h * HD_B, W_B - (h + 1) * HD_B))) for h in range(H_B)]
    return jnp.concatenate(rows, axis=1)


def _state_from_bd(s):
    return jnp.stack([s[:, h * HD_B:(h + 1) * HD_B, h * HD_B:(h + 1) * HD_B] for h in range(H_B)], axis=1)


def _rwkv_branch_prompt(z3, prep_w, wkv_w):
    nb, t_len, _ = z3.shape
    prep = _rwkv_prep(z3.reshape(nb * t_len, Z_COLS), prep_w, seq_len=t_len, tm=t_len // 3)
    prep3 = tuple(a.reshape(nb, t_len, W_B) for a in prep)
    y, s_bd = _wkv(z3, prep3, jnp.zeros((nb, W_B, W_B), F32), wkv_w, t_valid=t_len,
                   nbatch=min(nb, WKV_BATCH), nseq=1, sub_len=CHUNK)
    return y, _state_from_bd(s_bd)


def _rwkv_branch_sample(z3, shift_prev, s_prev, prep_w, wkv_w):
    nb, ts, _ = z3.shape
    z = z3.reshape(nb * ts, Z_COLS)
    prep = _rwkv_prep(z, prep_w, seq_len=ts, tm=nb * ts, prev_ext=jnp.repeat(shift_prev, ts, axis=0))
    per_chunk = CHUNK // ts
    groups = nb // per_chunk
    prep3 = tuple(a.reshape(groups, CHUNK, W_B) for a in prep)
    y, s_bd = _wkv(z.reshape(groups, CHUNK, Z_COLS), prep3, _state_to_bd(s_prev), wkv_w,
                   t_valid=CHUNK, nbatch=1, nseq=per_chunk, sub_len=ts)
    return y.reshape(nb, ts, W_B), _state_from_bd(s_bd)


def _mla_prep_kernel(cq_ref, ckv_ref, za_ref, zb_ref, cos_ref, sin_ref, qn_ref, kvn_ref, wr1_ref, wr2_ref,
                     ckv_out, krt_out, kp_out, cqn_out, of_out, *, t_valid):
    tm = cq_ref.shape[0]
    row = lax.broadcasted_iota(jnp.int32, (tm, 1), 0)
    valid = (pl.program_id(1) * tm + row) < t_valid
    cq = cq_ref[...]
    cqn = cq * lax.rsqrt(jnp.mean(cq * cq, axis=-1, keepdims=True) + 1e-6) * qn_ref[...]
    cv = ckv_ref[...]
    ckv = cv * lax.rsqrt(jnp.mean(cv * cv, axis=-1, keepdims=True) + 1e-6) * kvn_ref[...]
    cos = cos_ref[...]
    sin = sin_ref[...]
    za = za_ref[...]
    zb = zb_ref[...]
    kr1 = za * cos - zb * sin
    kr2 = za * sin + zb * cos
    cqb = cqn.astype(BF16)
    q1 = _dot(cqb, wr1_ref[...])
    q2 = _dot(cqb, wr2_ref[...])
    o1 = q1 * cos - q2 * sin
    o2 = q1 * sin + q2 * cos
    ckv_out[...] = ckv
    krt_out[:, 0:LANES] = kr1
    krt_out[:, LANES:] = kr2
    kp_out[:, 0:KV_RANK] = jnp.where(valid, ckv, 0.0).astype(BF16)
    kp_out[:, KV_RANK:KV_RANK + LANES] = jnp.where(valid, kr1, 0.0).astype(BF16)
    kp_out[:, KV_RANK + LANES:] = jnp.where(valid, kr2, 0.0).astype(BF16)
    cqn_out[...] = jnp.where(valid, cqn, 0.0).astype(BF16)
    of_out[:, 0:LANES] = jnp.where(valid, o1 * Q_SCALE, 0.0).astype(BF16)
    of_out[:, LANES:] = jnp.where(valid, o2 * Q_SCALE, 0.0).astype(BF16)


def _mla_prep(z3, cos, sin, mw, *, tm, tp):
    nb, t_len, _ = z3.shape
    nt = tp // tm
    last = -(-t_len // tm) - 1
    wspecs = [pl.BlockSpec(w.shape, lambda b, j: (0,) * w.ndim) for w in mw]
    real = lambda width, col: pl.BlockSpec((None, tm, width), lambda b, j: (b, jnp.minimum(j, last), col))
    blk = lambda width: pl.BlockSpec((None, tm, width), lambda b, j: (b, j, 0))
    tab = pl.BlockSpec((tm, LANES), lambda b, j: (jnp.minimum(j, last), 0))
    return pl.pallas_call(
        functools.partial(_mla_prep_kernel, t_valid=t_len),
        grid=(nb, nt),
        in_specs=[real(256, 4), real(256, 5), real(LANES, 16), real(LANES, 17), tab, tab] + wspecs,
        out_specs=[real(256, 0), real(256, 0), blk(512), blk(256), blk(256)],
        out_shape=[jax.ShapeDtypeStruct((nb, t_len, KV_RANK), F32),
                   jax.ShapeDtypeStruct((nb, t_len, 256), F32),
                   jax.ShapeDtypeStruct((nb, tp, 512), BF16),
                   jax.ShapeDtypeStruct((nb, tp, Q_RANK), BF16),
                   jax.ShapeDtypeStruct((nb, tp, 256), BF16)],
        compiler_params=_cparams("parallel", "parallel"), name="mla_prep",
    )(z3, z3, z3, z3, cos, sin, *mw)


def _fold_kernel(wq_ref, wk_ref, o_ref):
    o_ref[...] = _dot_hi(wq_ref[...], wk_ref[...]).astype(BF16)


def _fold_q(wq_nope, wuk_t):
    return pl.pallas_call(
        _fold_kernel,
        grid=(H_C,),
        in_specs=[pl.BlockSpec((None, Q_RANK, NOPE_DIM), lambda h: (h, 0, 0)),
                  pl.BlockSpec((None, NOPE_DIM, KV_RANK), lambda h: (h, 0, 0))],
        out_specs=pl.BlockSpec((Q_RANK, KV_RANK), lambda h: (0, h)),
        out_shape=jax.ShapeDtypeStruct((Q_RANK, H_C * KV_RANK), BF16),
        compiler_params=_cparams("parallel"), name="fold_q",
    )(wq_nope, wuk_t)


def _mla_weights(p):
    row2 = lambda v: v.reshape(1, -1)
    wq = p['w_uq'].reshape(Q_RANK, H_C, NOPE_DIM + ROPE_DIM)
    wq_nope = jnp.transpose(wq[:, :, 0:NOPE_DIM], (1, 0, 2))
    wr1 = wq[:, :, NOPE_DIM:NOPE_DIM + ROPE_HALF].reshape(Q_RANK, LANES).astype(BF16)
    wr2 = wq[:, :, NOPE_DIM + ROPE_HALF:].reshape(Q_RANK, LANES).astype(BF16)
    wuk_t = jnp.transpose(p['w_uk'], (1, 2, 0))
    wf = _fold_q(wq_nope, wuk_t)
    mla_w = (row2(p['q_norm']), row2(p['kv_norm']), wr1, wr2)
    eye = jnp.eye(2, dtype=F32)[jnp.arange(H_C) % 2]
    wuv = jnp.einsum('rhv,hg->hrgv', p['w_uv'], eye).reshape(H_C, KV_RANK, 2 * V_DIM).astype(BF16)
    return mla_w, wf, wuv


def _build_q(cqn, of, wf, hm, tq):
    q_all = _dot(cqn, wf) * Q_SCALE
    lat = [q_all[:, h * KV_RANK:(h + 1) * KV_RANK].astype(BF16) for h in range(H_C)]
    rope = [of * hm[h:h + 1].astype(BF16) for h in range(H_C)]
    return jnp.concatenate(lat, axis=0), jnp.concatenate(rope, axis=0)


def _project_out(o, wuv_ref, tq):
    blocks = []
    for g in range(H_C // 2):
        h0, h1 = 2 * g, 2 * g + 1
        blocks.append(_dot(o[h0 * tq:(h0 + 1) * tq].astype(BF16), wuv_ref[h0])
                      + _dot(o[h1 * tq:(h1 + 1) * tq].astype(BF16), wuv_ref[h1]))
    return jnp.concatenate(blocks, axis=1)


def _attn_prompt_kernel(qt_ref, kt_ref, cqn_ref, of_ref, kp_ref, wf_ref, hm_ref, wuv_ref, *refs, has_prev):
    y_ref, ql_scr, qr_scr, m_scr, acc_scr = refs[1:] if has_prev else refs
    tq = cqn_ref.shape[0]
    tk = kp_ref.shape[0]
    t = pl.program_id(1)
    qi = qt_ref[t]
    ki = kt_ref[t]
    last = (qi * tq + tq - 1) // tk

    @pl.when(ki == 0)
    def _():
        lat, rope = _build_q(cqn_ref[...], of_ref[...], wf_ref[...], hm_ref[...], tq)
        ql_scr[...] = lat
        qr_scr[...] = rope
        m_scr[...] = jnp.full(m_scr.shape, NEG_INF, F32)
        acc_scr[...] = jnp.zeros(acc_scr.shape, F32)

    def step(masked, nk=tk):
        kp = kp_ref[0:nk, :]
        ckv = kp[:, 0:KV_RANK]
        krope = kp[:, KV_RANK:]
        gr = max(1, min(H_C, ATT_TQ // tq)) * tq
        n_groups = H_C * tq // gr

        def scores(g):
            rows = pl.ds(g * gr, gr)
            return _dot_nt(ql_scr[rows, :], ckv) + _dot_nt(qr_scr[rows, :], krope)

        skew = min(ATT_SKEW, n_groups)
        ahead = [scores(g) for g in range(skew)]
        for g in range(n_groups):
            rows = pl.ds(g * gr, gr)
            s = ahead.pop(0)
            if g + skew < n_groups:
                ahead.append(scores(g + skew))
            if masked:
                qpos = qi * tq + lax.broadcasted_iota(jnp.int32, s.shape, 0) % tq
                kpos = ki * tk + lax.broadcasted_iota(jnp.int32, s.shape, 1)
                s = jnp.where(kpos <= qpos, s, NEG_INF)
            m_old = m_scr[rows, :]
            m_new = jnp.maximum(m_old, jnp.max(s, axis=-1, keepdims=True))
            alpha = jnp.exp2(m_old - m_new)
            pt = [jnp.exp2(st - m_new) for st in _lane_tiles(s)]
            p = jnp.concatenate(pt, axis=1).astype(BF16)
            pv = jnp.concatenate([_dot(p, ckv), functools.reduce(lambda x, y: x + y, pt)], axis=1)
            acc = acc_scr[rows, :]
            acc_scr[rows, :] = jnp.concatenate([at * alpha for at in _lane_tiles(acc)], axis=1) + pv
            m_scr[rows, :] = m_new

    @pl.when(ki < last)
    def _():
        step(False)

    visible = qi * tq + tq - ki * tk

    @pl.when((ki == last) & (visible <= tk // 2))
    def _():
        step(True, tk // 2)

    @pl.when((ki == last) & (visible > tk // 2))
    def _():
        step(True)

    @pl.when(ki == last)
    def _():
        acc = acc_scr[...]
        l = jnp.sum(acc[:, KV_RANK:KV_RANK + LANES], axis=-1, keepdims=True)
        y_ref[...] = _project_out(acc[:, 0:KV_RANK] / l, wuv_ref, tq)


def _attn_prompt_tiles(cqn, of, kp, wf, hm, wuv, t_len, *, tq, q_tiles, y_prev=None):
    nb = kp.shape[0]
    tk = ATT_TK
    pairs = [(i, j) for i in q_tiles for j in range((i * tq + tq - 1) // tk + 1)]
    q_tab = jnp.asarray([p[0] for p in pairs], jnp.int32)
    k_tab = jnp.asarray([p[1] for p in pairs], jnp.int32)
    rows = H_C * tq
    const = lambda w: pl.BlockSpec(w.shape, lambda b, t, qt, kt: (0,) * w.ndim)
    in_specs = [pl.BlockSpec((None, tq, Q_RANK), lambda b, t, qt, kt: (b, qt[t], 0)),
                pl.BlockSpec((None, tq, 256), lambda b, t, qt, kt: (b, qt[t], 0)),
                pl.BlockSpec((None, tk, 512), lambda b, t, qt, kt: (b, kt[t], 0)),
                const(wf), const(hm), const(wuv)]
    args = [q_tab, k_tab, cqn, of, kp, wf, hm, wuv]
    aliases = {}
    if y_prev is not None:
        in_specs.append(pl.BlockSpec(memory_space=pl.ANY))
        args.append(y_prev)
        aliases = {len(args) - 1: 0}
    grid_spec = pltpu.PrefetchScalarGridSpec(
        num_scalar_prefetch=2,
        grid=(nb, len(pairs)),
        in_specs=in_specs,
        out_specs=pl.BlockSpec((None, tq, W_C), lambda b, t, qt, kt: (b, qt[t], 0)),
        scratch_shapes=[pltpu.VMEM((rows, KV_RANK), BF16), pltpu.VMEM((rows, 256), BF16),
                        pltpu.VMEM((rows, LANES), F32), pltpu.VMEM((rows, KV_RANK + LANES), F32)],
    )
    return pl.pallas_call(
        functools.partial(_attn_prompt_kernel, has_prev=y_prev is not None),
        grid_spec=grid_spec,
        out_shape=jax.ShapeDtypeStruct((nb, t_len, W_C), F32),
        input_output_aliases=aliases,
        compiler_params=_cparams("parallel", "arbitrary"), name="attn_prompt",
    )(*args)


def _attn_prompt(cqn, of, kp, wf, hm, wuv, t_len):
    n_full = t_len // ATT_TQ
    rem = t_len - n_full * ATT_TQ
    y = jnp.zeros((kp.shape[0], t_len, W_C), F32)
    y = _attn_prompt_tiles(cqn, of, kp, wf, hm, wuv, t_len, tq=ATT_TQ, q_tiles=range(n_full), y_prev=y)
    if rem:
        assert rem % 16 == 0 and (n_full * ATT_TQ) % rem == 0
        y = _attn_prompt_tiles(cqn, of, kp, wf, hm, wuv, t_len, tq=rem, q_tiles=[n_full * ATT_TQ // rem],
                               y_prev=y)
    return y


def _kr_leaf(krt):
    return jnp.concatenate([krt[..., 0:ROPE_HALF], krt[..., LANES:LANES + ROPE_HALF]], axis=-1)


def _mla_branch_prompt(z3, mla_w, wf, wuv, hm):
    nb, t_len, _ = z3.shape
    tp = -(-t_len // ATT_TK) * ATT_TK
    cos, sin = _rope_tables(jnp.arange(tp))
    ckv, krt, kp, cqn, of = _mla_prep(z3, cos, sin, mla_w, tm=ATT_TQ, tp=tp)
    yc = _attn_prompt(cqn, of, kp, wf, hm, wuv, t_len)
    return yc, ckv, _kr_leaf(krt)


def _page_copies(pt_ref, ckv_hbm, kr_hbm, kbuf, krbuf, sem, seq, slot, *, layer, n_pages):
    out = []
    for pg in range(n_pages):
        pid = pt_ref[seq * n_pages + pg]
        out.append(pltpu.make_async_copy(ckv_hbm.at[layer, pid], kbuf.at[slot, pg], sem.at[slot, 0]))
        out.append(pltpu.make_async_copy(kr_hbm.at[layer, pid], krbuf.at[slot, pg], sem.at[slot, 1]))
    return out


def _attn_sample_kernel(pt_ref, ckv_hbm, kr_hbm, cqn_ref, of_ref, cnew_ref, knew_ref, wf_ref, hm_ref,
                        tile_ref, wuv_ref, y_ref, kbuf, krbuf, k_scr, kr_scr, sem, *, layer, n_pages):
    tq = cqn_ref.shape[1]
    page = kbuf.shape[2]
    span = SAMPLE_PAGES_PER_CHUNK * page
    n_chunks = n_pages // SAMPLE_PAGES_PER_CHUNK
    step = pl.program_id(0)
    last_step = pl.num_programs(0) - 1
    copies = functools.partial(_page_copies, pt_ref, ckv_hbm, kr_hbm, kbuf, krbuf, sem,
                               layer=layer, n_pages=n_pages)

    def attend(j, slot):
        lat, rope = _build_q(cqn_ref[j], of_ref[j], wf_ref[...], hm_ref[...], tq)
        qr = _dot(rope, tile_ref[...]).astype(BF16)
        cnew = cnew_ref[j].astype(BF16)
        s_new = _dot_nt(lat, cnew) + _dot_nt(qr, knew_ref[j].astype(BF16))
        tpos = lax.broadcasted_iota(jnp.int32, s_new.shape, 0) % tq
        s_new = jnp.where(lax.broadcasted_iota(jnp.int32, s_new.shape, 1) <= tpos, s_new, NEG_INF)
        m = jnp.max(s_new, axis=-1, keepdims=True)
        p_new = jnp.exp2(s_new - m)
        l = jnp.sum(p_new, axis=-1, keepdims=True)
        acc = _dot(p_new.astype(BF16), cnew)

        def scores(c):
            for i in range(c * SAMPLE_PAGES_PER_CHUNK, (c + 1) * SAMPLE_PAGES_PER_CHUNK):
                k_scr[i * page:(i + 1) * page, :] = kbuf[slot, i].astype(BF16)
                kr_scr[:, i * page:(i + 1) * page] = krbuf[slot, i].astype(BF16)
            cols = slice(c * span, (c + 1) * span)
            return _dot_nt(lat, k_scr[cols, :]) + _dot(qr, kr_scr[:, cols])

        ahead = [scores(c) for c in range(SAMPLE_SKEW)]
        for c in range(n_chunks):
            s = ahead.pop(0)
            if c + SAMPLE_SKEW < n_chunks:
                ahead.append(scores(c + SAMPLE_SKEW))
            m_new = jnp.maximum(m, jnp.max(s, axis=-1, keepdims=True))
            alpha = jnp.exp2(m - m_new)
            p = jnp.exp2(s - m_new)
            l = alpha * l + jnp.sum(p, axis=-1, keepdims=True)
            acc = alpha * acc + _dot(p.astype(BF16), k_scr[c * span:(c + 1) * span, :])
            m = m_new
        y_ref[j] = _project_out(acc / l, wuv_ref, tq)

    seq0 = 2 * step

    @pl.when(step == 0)
    def _():
        for cp in copies(seq0, 0):
            cp.start()

    for cp in copies(seq0 + 1, 1):
        cp.start()
    for cp in copies(seq0, 0):
        cp.wait()
    attend(0, 0)
    seq_next = jnp.minimum(seq0 + 2, 2 * last_step)
    for cp in copies(seq_next, 0):
        cp.start()
    for cp in copies(seq0 + 1, 1):
        cp.wait()
    attend(1, 1)

    @pl.when(step == last_step)
    def _():
        for cp in copies(seq_next, 0):
            cp.wait()


def _attn_sample(page_table, cache_ckv, cache_krope_t, layer, cqn, of, cnew, knew, wf, hm, tile_t, wuv):
    nb, tq, _ = cqn.shape
    n_pages = page_table.shape[1]
    page = cache_ckv.shape[2]
    assert nb % 2 == 0
    per_step = lambda width: pl.BlockSpec((2, tq, width), lambda i, pt: (i, 0, 0))
    const = lambda w: pl.BlockSpec(w.shape, lambda i, pt: (0,) * w.ndim)
    hbm = pl.BlockSpec(memory_space=pl.ANY)
    grid_spec = pltpu.PrefetchScalarGridSpec(
        num_scalar_prefetch=1,
        grid=(nb // 2,),
        in_specs=[hbm, hbm, per_step(Q_RANK), per_step(256), per_step(KV_RANK), per_step(ROPE_DIM),
                  const(wf), const(hm), const(tile_t), const(wuv)],
        out_specs=per_step(W_C),
        scratch_shapes=[pltpu.VMEM((2, n_pages, page, KV_RANK), F32),
                        pltpu.VMEM((2, n_pages, ROPE_DIM, page), F32),
                        pltpu.VMEM((n_pages * page, KV_RANK), BF16),
                        pltpu.VMEM((ROPE_DIM, n_pages * page), BF16),
                        pltpu.SemaphoreType.DMA((2, 2))],
    )
    return pl.pallas_call(
        functools.partial(_attn_sample_kernel, layer=layer, n_pages=n_pages),
        grid_spec=grid_spec,
        out_shape=jax.ShapeDtypeStruct((nb, tq, W_C), F32),
        compiler_params=_cparams("arbitrary"), name="attn_sample",
    )(page_table.reshape(-1), cache_ckv, cache_krope_t, cqn, of, cnew, knew, wf, hm, tile_t, wuv)


def _mla_branch_sample(z3, page_table, cache_ckv, cache_krope_t, layer, n_past, mla_w, wf, wuv, hm, tile_t):
    nb, ts, _ = z3.shape
    cos, sin = _rope_tables(n_past + jnp.arange(nb * ts) % ts)
    ckv, krt, _, cqn, of = _mla_prep(z3.reshape(1, nb * ts, Z_COLS), cos, sin, mla_w, tm=ATT_TQ, tp=nb * ts)
    ckv = ckv.reshape(nb, ts, KV_RANK)
    kr = _kr_leaf(krt.reshape(nb, ts, 256))
    yc = _attn_sample(page_table, cache_ckv, cache_krope_t, layer, cqn.reshape(nb, ts, Q_RANK),
                      of.reshape(nb, ts, 256), ckv, kr, wf, hm, tile_t, wuv)
    return yc, ckv, kr


def _out_kernel(ya_ref, yb_ref, yc_ref, gc_ref, x_ref, wa_ref, wb_ref, wc_ref, g_ref, b_ref, o_ref):
    yc = (yc_ref[...] * _silu(gc_ref[...])).astype(BF16)
    out = _dot(ya_ref[...], wa_ref[...]) + _dot(yb_ref[...], wb_ref[...]) + _dot(yc, wc_ref[...])
    h = DN_ALPHA * x_ref[...] + out
    mu = jnp.mean(h, axis=-1, keepdims=True)
    d = h - mu
    var = jnp.mean(d * d, axis=-1, keepdims=True)
    o_ref[...] = d * lax.rsqrt(var + 1e-5) * g_ref[...] + b_ref[...]


def _out_proj(ya, yb, yc, z, x, ow, tm):
    n = x.shape[0]
    wspecs = [pl.BlockSpec(w.shape, lambda i: (0,) * w.ndim) for w in ow]
    rowblk = lambda width, col=0: pl.BlockSpec((tm, width), lambda i: (i, col))
    return pl.pallas_call(
        _out_kernel,
        grid=(n // tm,),
        in_specs=[rowblk(W_A), rowblk(W_B), rowblk(W_C), rowblk(W_C, 3), rowblk(D_MODEL)] + wspecs,
        out_specs=rowblk(D_MODEL),
        out_shape=jax.ShapeDtypeStruct((n, D_MODEL), F32),
        compiler_params=_cparams("parallel"), name="out_proj",
    )(ya, yb, yc, z, x, *ow)


def _out_proj_tail(ya, yb, yc, z3, x, ow, *, skip, tm):
    nb, t_len, _ = z3.shape
    keep = t_len - skip
    seq3 = lambda a: a.reshape(nb, t_len, a.shape[-1])
    wspecs = [pl.BlockSpec(w.shape, lambda b, j: (0,) * w.ndim) for w in ow]
    rows = lambda width, col=0: pl.BlockSpec((pl.Element(1), pl.Element(tm), pl.Element(width)),
                                             lambda b, j: (b, pl.multiple_of(skip + j * tm, skip), col * width))

    def tail_kernel(ya_ref, yb_ref, yc_ref, gc_ref, x_ref, *rest):
        _out_kernel(ya_ref.at[0], yb_ref.at[0], yc_ref.at[0], gc_ref.at[0], x_ref.at[0], *rest)

    return pl.pallas_call(
        tail_kernel,
        grid=(nb, keep // tm),
        in_specs=[rows(W_A), rows(W_B), rows(W_C), rows(W_C, 3), rows(D_MODEL)] + wspecs,
        out_specs=pl.BlockSpec((None, tm, D_MODEL), lambda b, j: (b, j, 0)),
        out_shape=jax.ShapeDtypeStruct((nb, keep, D_MODEL), F32),
        compiler_params=_cparams("parallel", "parallel"), name="out_proj_tail",
    )(seq3(ya), seq3(yb), seq3(yc), z3, seq3(x), *ow)


def _block_diag(blocks):
    nblk, bw, _ = blocks.shape
    eye = jnp.eye(nblk, dtype=blocks.dtype)
    return jnp.einsum('hij,hg->higj', blocks, eye).reshape(nblk * bw, nblk * bw)


def _pad_to(x, shape):
    return jnp.pad(x, [(0, s - d) for d, s in zip(x.shape, shape)])


def _rope_tables(pos):
    inv = ROPE_THETA ** (-2.0 * jnp.arange(ROPE_HALF, dtype=F32) / ROPE_DIM)
    ang = pos.astype(F32)[:, None] * inv[None, :]
    return jnp.tile(jnp.cos(ang), (1, H_C)), jnp.tile(jnp.sin(ang), (1, H_C))


def _static_tables():
    lane = np.arange(256)
    hm = np.stack([((lane % LANES) // ROPE_HALF == h) for h in range(H_C)]).astype(np.float32)
    hmb = np.stack([(lane // HD_B == h) for h in range(H_B)]).astype(np.float32)
    ones_bd = (lane[:, None] // HD_B == lane[None, :] // HD_B).astype(np.float32)
    tile_t = np.zeros((256, ROPE_DIM), np.float32)
    for ln in lane:
        tile_t[ln, (ln // LANES) * ROPE_HALF + ln % ROPE_HALF] = 1.0
    return jnp.asarray(hm), jnp.asarray(hmb), jnp.asarray(ones_bd, BF16), jnp.asarray(tile_t, BF16)


_PER_LAYER = ('w_in', 'conv_w', 'conv_b', 'lru_wr', 'lru_br', 'lru_wi', 'lru_bi', 'lru_lambda', 'rk_mix',
              'rk_wr', 'rk_wk', 'rk_wv', 'rk_w0', 'rk_w1', 'rk_w2', 'rk_a0', 'rk_a1', 'rk_a2', 'rk_kk',
              'rk_ka', 'rk_rk', 'rk_gn_g', 'rk_gn_b', 'q_norm', 'kv_norm', 'w_uq', 'w_uk', 'w_uv', 'w_out',
              'ln_g', 'ln_b')


def kernel(x_prompt, x_sample, cache_ckv, cache_krope, state_conv, state_lru, state_shift, state_wkv,
           page_table, meta_tokens, w_in, conv_w, conv_b, lru_wr, lru_br, lru_wi, lru_bi, lru_lambda,
           rk_mix, rk_wr, rk_wk, rk_wv, rk_w0, rk_w1, rk_w2, rk_a0, rk_a1, rk_a2, rk_kk, rk_ka, rk_rk,
           rk_gn_g, rk_gn_b, q_norm, kv_norm, w_uq, w_uk, w_uv, w_out, ln_g, ln_b):
    stacked = dict(zip(_PER_LAYER, (w_in, conv_w, conv_b, lru_wr, lru_br, lru_wi, lru_bi, lru_lambda, rk_mix,
                                    rk_wr, rk_wk, rk_wv, rk_w0, rk_w1, rk_w2, rk_a0, rk_a1, rk_a2, rk_kk,
                                    rk_ka, rk_rk, rk_gn_g, rk_gn_b, q_norm, kv_norm, w_uq, w_uk, w_uv, w_out,
                                    ln_g, ln_b)))
    bp, seq, _ = x_prompt.shape
    bs, ts, _ = x_sample.shape
    tp_len = N_META + seq
    n_past = page_table.shape[1] * cache_ckv.shape[2]
    depth = w_in.shape[0]
    assert ts == 8 and CHUNK % ts == 0 and bs % (CHUNK // ts) == 0
    assert tp_len % 24 == 0 and seq % TAIL_TILE == 0 and N_META % 8 == 0

    hm, hmb, ones_bd, tile_t = _static_tables()
    row2 = lambda v: v.reshape(1, -1)
    cache_krope_t = jnp.swapaxes(cache_krope, 2, 3)

    xp = jnp.concatenate(
        [jnp.broadcast_to(meta_tokens[None].astype(x_prompt.dtype), (bp, N_META, D_MODEL)), x_prompt],
        axis=1).reshape(bp * tp_len, D_MODEL)
    xs = x_sample.reshape(bs * ts, D_MODEL)
    tm_p = tp_len // 3
    tm_s = bs * ts

    p_states, s_states = [], []
    for l in range(depth):
        p = {name: arr[l] for name, arr in stacked.items()}
        wi = p['w_in']
        kr_cols = wi[:, 1536:1568]
        w_in_r = jnp.concatenate(
            [wi[:, 0:1536], wi[:, 1568:2080],
             jnp.tile(kr_cols[:, 0:ROPE_HALF], (1, H_C)), jnp.tile(kr_cols[:, ROPE_HALF:], (1, H_C))],
            axis=1).astype(BF16)
        lru_w = _rglru_weights(p)
        prep_w, wkv_w = _rwkv_weights(p, hmb, ones_bd)
        mla_w, wf, wuv = _mla_weights(p)
        wo = p['w_out'].astype(BF16)
        out_w = (wo[0:W_A], wo[W_A:W_A + W_B], wo[W_A + W_B:], row2(p['ln_g']), row2(p['ln_b']))

        z = _in_proj(xp, w_in_r, tm_p)
        z3 = z.reshape(bp, tp_len, Z_COLS)
        ya, lru_new = _rglru_branch_prompt(z3, lru_w)
        yb, wkv_new = _rwkv_branch_prompt(z3, prep_w, wkv_w)
        yc, ckv, kr = _mla_branch_prompt(z3, mla_w, wf, wuv, hm)
        if l + 1 < depth:
            xp = _out_proj(ya, yb.reshape(bp * tp_len, W_B), yc.reshape(bp * tp_len, W_C), z, xp, out_w, tm_p)
        else:
            y_prompt = _out_proj_tail(ya, yb, yc, z3, xp, out_w, skip=N_META, tm=TAIL_TILE)
        p_states.append((ckv, kr, z3[:, tp_len - (CONV_W - 1):, 0:W_A], lru_new,
                         z3[:, tp_len - 1, 2 * W_A:2 * W_A + W_B], wkv_new))

        z = _in_proj(xs, w_in_r, tm_s)
        z3 = z.reshape(bs, ts, Z_COLS)
        ya, lru_new = _rglru_branch_sample(z3, state_conv[l], state_lru[l], n_past, lru_w)
        yb, wkv_new = _rwkv_branch_sample(z3, state_shift[l], state_wkv[l], prep_w, wkv_w)
        yc, ckv, kr = _mla_branch_sample(z3, page_table, cache_ckv, cache_krope_t, l, n_past,
                                         mla_w, wf, wuv, hm, tile_t)
        x_new = _out_proj(ya, yb.reshape(bs * ts, W_B), yc.reshape(bs * ts, W_C), z, xs, out_w, tm_s)
        s_states.append((ckv, kr, z3[:, ts - (CONV_W - 1):, 0:W_A], lru_new,
                         z3[:, ts - 1, 2 * W_A:2 * W_A + W_B], wkv_new))
        xs = x_new

    y_sample = xs.reshape(bs, ts, D_MODEL)
    p_out = [jnp.stack([st[i] for st in p_states]) for i in range(6)]
    s_out = [jnp.stack([st[i] for st in s_states]) for i in range(6)]
    return (y_prompt, y_sample, *p_out, *s_out)
```

```python
import functools
import itertools

import numpy as np
import jax
import jax.numpy as jnp
from jax import lax
from jax.experimental import pallas as pl
from jax.experimental.pallas import tpu as pltpu

F32 = jnp.float32
BF16 = jnp.bfloat16
HIGHEST = lax.Precision.HIGHEST

D_MODEL = 1024
N_META = 16
W_A = 256
W_B = 256
W_C = 512
CONV_W = 4
LRU_C = 8.0
HD_B = 64
H_B = 4
GN_EPS = 64e-5
H_C = 8
V_DIM = 64
NOPE_DIM = 64
ROPE_DIM = 32
ROPE_HALF = 16
Q_RANK = 256
KV_RANK = 256
ROPE_THETA = 10000.0
ATTN_SCALE = (NOPE_DIM + ROPE_DIM) ** -0.5
LOG2_E = 1.4426950408889634
Q_SCALE = ATTN_SCALE * LOG2_E
NEG_INF = -1e30
DEPTH = 2
DN_ALPHA = (2 * DEPTH) ** 0.25

LANES = 128
Z_COLS = 2304
CHUNK = 64
WKV_BATCH = 4
TAIL_TILE = 512
ATT_TQ = 256
ATT_TK = 512
ATT_SKEW = 3
SAMPLE_PAGES_PER_CHUNK = 8
SAMPLE_SKEW = 3

VMEM_LIMIT = 48 * 1024 * 1024


def _cparams(*sem):
    return pltpu.CompilerParams(dimension_semantics=sem, vmem_limit_bytes=VMEM_LIMIT)


def _dot(a, b):
    return jnp.dot(a, b, preferred_element_type=F32)


def _dot_nt(a, b):
    return lax.dot_general(a, b, (((1,), (1,)), ((), ())), preferred_element_type=F32)


def _dot_tn(a, b):
    return lax.dot_general(a, b, (((0,), (0,)), ((), ())), preferred_element_type=F32)


def _dot_hi(a, b):
    return jnp.dot(a, b, preferred_element_type=F32, precision=HIGHEST)


def _split(x):
    hi = x.astype(BF16)
    return hi, (x - hi.astype(F32)).astype(BF16)


def _split3(x):
    hi, mid = _split(x)
    return hi, mid, (x - hi.astype(F32) - mid.astype(F32)).astype(BF16)


def _dot_sel(sel, b):
    b_hi, b_mid, b_lo = _split3(b)
    return _dot(sel, b_hi) + (_dot(sel, b_mid) + _dot(sel, b_lo))


def _dot_sel_r(a, sel):
    a_hi, a_mid, a_lo = _split3(a)
    return _dot(a_hi, sel) + (_dot(a_mid, sel) + _dot(a_lo, sel))


def _softplus(x):
    return jnp.maximum(x, 0.0) + jnp.log1p(jnp.exp(-jnp.abs(x)))


def _silu(x):
    return x * jax.nn.sigmoid(x)


def _lane_tiles(x):
    return [x[:, j * LANES:(j + 1) * LANES] for j in range(x.shape[1] // LANES)]


def _mm_kernel(x_ref, w_ref, o_ref):
    o_ref[...] = _dot(x_ref[...].astype(BF16), w_ref[...])


def _in_proj(x, w, tm):
    n, k = x.shape
    m = w.shape[1]
    return pl.pallas_call(
        _mm_kernel,
        grid=(n // tm,),
        in_specs=[pl.BlockSpec((tm, k), lambda i: (i, 0)), pl.BlockSpec((k, m), lambda i: (0, 0))],
        out_specs=pl.BlockSpec((tm, m), lambda i: (i, 0)),
        out_shape=jax.ShapeDtypeStruct((n, m), F32),
        compiler_params=_cparams("parallel"),
        name="in_proj",
    )(x, w)


def _rglru_kernel(*refs, seq_len, pos0, has_state):
    if has_state:
        (u_ref, g_ref, f_ref, hp_ref, cw_ref, cb_ref, wr_ref, br_ref, wi_ref, bi_ref, lam_ref,
         y_ref, h_ref) = refs
    else:
        (u_ref, g_ref, cw_ref, cb_ref, wr_ref, br_ref, wi_ref, bi_ref, lam_ref,
         y_ref, h_ref, a_scr, b_scr) = refs
    u = u_ref[...]
    rows = u.shape[0]
    row = lax.broadcasted_iota(jnp.int32, u.shape, 0)
    t = row if rows == seq_len else row & (seq_len - 1)

    cw = cw_ref[...]
    xc = cb_ref[...] + u * cw[CONV_W - 1:CONV_W]
    for k in range(1, CONV_W):
        sh = pltpu.roll(u, k, 0)
        if has_state:
            f = f_ref[...]
            prev = f if k == CONV_W - 1 else pltpu.roll(f, rows - (CONV_W - 1 - k), 0)
        else:
            prev = 0.0
        sh = jnp.where(t < k, prev, sh)
        xc = xc + sh * cw[CONV_W - 1 - k:CONV_W - k]

    xcb = xc.astype(BF16)
    gate_r = jax.nn.sigmoid(_dot(xcb, wr_ref[...]) + br_ref[...])
    gate_i = jax.nn.sigmoid(_dot(xcb, wi_ref[...]) + bi_ref[...])
    log_a = (-LRU_C) * gate_r * _softplus(-lam_ref[...])
    th = jnp.tanh(log_a)
    neg_expm1 = -2.0 * th / (1.0 - th)
    reset = (t + pos0) == 0
    a = jnp.where(reset, 0.0, jnp.exp(log_a))
    mult = jnp.where(reset, 1.0, jnp.sqrt(neg_expm1))
    bv = xc * gate_i * mult

    t8 = row & 7
    for s in (1, 2, 4):
        a_s = jnp.where(t8 >= s, pltpu.roll(a, s, 0), 1.0)
        b_s = jnp.where(t8 >= s, pltpu.roll(bv, s, 0), 0.0)
        bv = a * b_s + bv
        a = a * a_s

    if has_state:
        h = bv + a * hp_ref[...]
        h_ref[...] = h
    else:
        a_scr[...] = a
        b_scr[...] = bv

        def body(i, carry):
            r0 = pl.multiple_of(i * 8, 8)
            hh = b_scr[pl.ds(r0, 8), :] + a_scr[pl.ds(r0, 8), :] * carry
            b_scr[pl.ds(r0, 8), :] = hh
            return jnp.broadcast_to(hh[7:8, :], hh.shape)

        carry = lax.fori_loop(0, rows // 8, body, jnp.zeros((8, u.shape[1]), F32))
        h = b_scr[...]
        h_ref[...] = carry[0:1, :]
    y_ref[...] = (h * _silu(g_ref[...])).astype(BF16)


def _rglru(z, lw, *, nseq, seq_len, pos0, conv_ext=None, h_ext=None):
    n = z.shape[0]
    wspecs = [pl.BlockSpec(w.shape, lambda i: (0,) * w.ndim) for w in lw]
    if conv_ext is None:
        rows = seq_len
        in_specs = [pl.BlockSpec((rows, W_A), lambda i: (i, 0)),
                    pl.BlockSpec((rows, W_A), lambda i: (i, 1))] + wspecs
        out_specs = [pl.BlockSpec((rows, W_A), lambda i: (i, 0)),
                     pl.BlockSpec((None, 1, W_A), lambda i: (i, 0, 0))]
        out_shape = [jax.ShapeDtypeStruct((n, W_A), BF16), jax.ShapeDtypeStruct((nseq, 1, W_A), F32)]
        scratch = [pltpu.VMEM((rows, W_A), F32), pltpu.VMEM((rows, W_A), F32)]
        args = (z, z) + tuple(lw)
        grid = (nseq,)
    else:
        rows = n
        full = pl.BlockSpec((rows, W_A), lambda i: (0, 0))
        in_specs = [full, pl.BlockSpec((rows, W_A), lambda i: (0, 1)), full, full] + wspecs
        out_specs = [full, full]
        out_shape = [jax.ShapeDtypeStruct((n, W_A), BF16), jax.ShapeDtypeStruct((n, W_A), F32)]
        scratch = []
        args = (z, z, conv_ext, h_ext) + tuple(lw)
        grid = (1,)
    return pl.pallas_call(
        functools.partial(_rglru_kernel, seq_len=seq_len, pos0=pos0, has_state=conv_ext is not None),
        grid=grid, in_specs=in_specs, out_specs=out_specs, out_shape=out_shape,
        scratch_shapes=scratch, compiler_params=_cparams("parallel"), name="rglru",
    )(*args)


def _rglru_weights(p):
    row2 = lambda v: v.reshape(1, -1)
    return (p['conv_w'], row2(p['conv_b']), _block_diag(p['lru_wr']).astype(BF16), row2(p['lru_br']),
            _block_diag(p['lru_wi']).astype(BF16), row2(p['lru_bi']), row2(p['lru_lambda']))


def _rglru_branch_prompt(z3, lru_w):
    nb, t_len, _ = z3.shape
    y, h_last = _rglru(z3.reshape(nb * t_len, Z_COLS), lru_w, nseq=nb, seq_len=t_len, pos0=0)
    return y, h_last.reshape(nb, W_A)


def _rglru_branch_sample(z3, conv_prev, h_prev, pos0, lru_w):
    nb, ts, _ = z3.shape
    conv_ext = _pad_to(conv_prev, (nb, ts, W_A)).reshape(nb * ts, W_A)
    h_ext = jnp.repeat(h_prev, ts, axis=0)
    y, h_all = _rglru(z3.reshape(nb * ts, Z_COLS), lru_w, nseq=nb, seq_len=ts, pos0=pos0,
                      conv_ext=conv_ext, h_ext=h_ext)
    return y, h_all.reshape(nb, ts, W_A)[:, ts - 1]


def _rwkv_prep_kernel(u_ref, p_ref, mix_ref, wr_ref, wk_ref, wv_ref, w0_ref, w1_ref, w2_ref,
                      a0_ref, a1_ref, a2_ref, kkw_ref, ka_ref, ones_ref,
                      r_ref, lw_ref, k_ref, v_ref, kk_ref, a_ref, *, seq_len, tiles_per_seq):
    u = u_ref[...]
    row = lax.broadcasted_iota(jnp.int32, u.shape, 0)
    rolled = pltpu.roll(u, 1, 0)
    if tiles_per_seq is None:
        u_prev = jnp.where((row & (seq_len - 1)) == 0, p_ref[...], rolled)
    else:
        first = (pl.program_id(0) % tiles_per_seq) == 0
        prow = jnp.where(first, 0.0, p_ref[7:8, :])
        u_prev = jnp.where(row == 0, prow, rolled)
    du = u_prev - u
    mix = mix_ref[...]
    xr = (u + du * mix[0:1]).astype(BF16)
    xw = (u + du * mix[1:2]).astype(BF16)
    xk = (u + du * mix[2:3]).astype(BF16)
    xv = (u + du * mix[3:4]).astype(BF16)
    xa = (u + du * mix[4:5]).astype(BF16)
    r = _dot(xr, wr_ref[...])
    k = _dot(xk, wk_ref[...])
    v = _dot(xv, wv_ref[...])
    lora_w = _dot(jnp.tanh(_dot(xw, w1_ref[...])).astype(BF16), w2_ref[...])
    w_log = -_softplus(-(w0_ref[...] + lora_w)) - 0.5
    a = jax.nn.sigmoid(a0_ref[...] + _dot(_dot(xa, a1_ref[...]).astype(BF16), a2_ref[...]))
    kk = k * kkw_ref[...]
    ssq = _dot_sel_r(kk * kk, ones_ref[...])
    kk = kk / jnp.maximum(jnp.sqrt(ssq), 1e-12)
    r_ref[...] = r.astype(BF16)
    lw_ref[...] = -jnp.exp(w_log)
    k_ref[...] = (k * (1.0 + (a - 1.0) * ka_ref[...])).astype(BF16)
    v_ref[...] = v.astype(BF16)
    kk_ref[...] = kk.astype(BF16)
    a_ref[...] = a.astype(BF16)


def _rwkv_prep(z, pw, *, seq_len, tm, prev_ext=None):
    n = z.shape[0]
    wspecs = [pl.BlockSpec(w.shape, lambda i: (0,) * w.ndim) for w in pw]
    if prev_ext is None:
        tiles = seq_len // tm
        p_arg = z
        p_spec = pl.BlockSpec((8, W_B), lambda i: (jnp.maximum(i * (tm // 8) - 1, 0), 2))
    else:
        tiles = None
        p_arg = prev_ext
        p_spec = pl.BlockSpec((tm, W_B), lambda i: (i, 0))
    out = [jax.ShapeDtypeStruct((n, W_B), F32 if i == 1 else BF16) for i in range(6)]
    ospec = pl.BlockSpec((tm, W_B), lambda i: (i, 0))
    return pl.pallas_call(
        functools.partial(_rwkv_prep_kernel, seq_len=seq_len, tiles_per_seq=tiles),
        grid=(n // tm,),
        in_specs=[pl.BlockSpec((tm, W_B), lambda i: (i, 2)), p_spec] + wspecs,
        out_specs=[ospec] * 6, out_shape=out,
        compiler_params=_cparams("parallel"), name="rwkv_prep",
    )(z, p_arg, *pw)


def _wkv_kernel(r_ref, lw_ref, k_ref, v_ref, kk_ref, a_ref, g_ref, s0_ref, hm_ref, ones_ref,
                rk_ref, gng_ref, gnb_ref, y_ref, s_ref, *, t_valid, nbatch, nseq, sub_len):
    c = pl.program_id(1)

    @pl.when(c == 0)
    def _():
        s_ref[...] = s0_ref[...]

    chains = [_wkv_chunk(c, j, r_ref, lw_ref, k_ref, v_ref, kk_ref, a_ref, g_ref, hm_ref, ones_ref,
                         rk_ref, gng_ref, gnb_ref, y_ref, s_ref, t_valid=t_valid, nseq=nseq, sub_len=sub_len)
              for j in range(nbatch)]
    for _ in itertools.zip_longest(*chains):
        pass


def _wkv_chunk(c, j, r_ref, lw_ref, k_ref, v_ref, kk_ref, a_ref, g_ref, hm_ref, ones_ref,
               rk_ref, gng_ref, gnb_ref, y_ref, s_ref, *, t_valid, nseq, sub_len):
    width = W_B
    nh = H_B
    s_base = j * nseq
    row = lax.broadcasted_iota(jnp.int32, (CHUNK, width), 0)
    valid = (c * CHUNK + row) < t_valid
    zero = jnp.zeros((CHUNK, width), F32)
    r = jnp.where(valid, r_ref[j].astype(F32), zero)
    lw = jnp.where(valid, lw_ref[j], zero)
    k = jnp.where(valid, k_ref[j].astype(F32), zero)
    v = jnp.where(valid, v_ref[j].astype(F32), zero)
    kk = jnp.where(valid, kk_ref[j].astype(F32), zero)
    a = jnp.where(valid, a_ref[j].astype(F32), zero)

    qi = lax.broadcasted_iota(jnp.int32, (CHUNK, CHUNK), 0)
    si = lax.broadcasted_iota(jnp.int32, (CHUNK, CHUNK), 1)
    same = (qi // sub_len) == (si // sub_len)
    cum = _dot_sel(jnp.where(same & (qi >= si), 1.0, 0.0).astype(BF16), lw)
    tot = _dot_sel(jnp.where(same, 1.0, 0.0).astype(BF16), lw)
    yield
    e_neg = jnp.exp(-cum)
    e_tail = jnp.exp(tot - cum)
    rg = r * jnp.exp(cum)
    ag = -kk * jnp.exp(cum - lw)
    kka = kk * a
    bg = kka * e_neg
    kg = k * e_neg
    bt = kka * e_tail
    kt = k * e_tail

    hm = hm_ref[...]

    def stack(x):
        return jnp.concatenate([x * hm[h:h + 1] for h in range(nh)], axis=0)

    def unstack(x):
        out = x[0:CHUNK]
        for h in range(1, nh):
            out = out + x[h * CHUNK:(h + 1) * CHUNK]
        return out

    a_st = stack(ag).astype(BF16)
    r_st = stack(rg)
    b_st = stack(bg).astype(BF16)
    k_st = stack(kg).astype(BF16)
    v_st = stack(v)

    n_st = nh * CHUNK
    rq = lax.broadcasted_iota(jnp.int32, (n_st, n_st), 0) % CHUNK
    cs = lax.broadcasted_iota(jnp.int32, (n_st, n_st), 1) % CHUNK
    same_st = (rq // sub_len) == (cs // sub_len)
    strict = same_st & (rq > cs)
    incl = same_st & (rq >= cs)

    l_ab = jnp.where(strict, _dot_nt(a_st, b_st), 0.0)
    l_ak = jnp.where(strict, _dot_nt(a_st, k_st), 0.0)
    yield
    p_rb = jnp.where(incl, _dot_nt(r_st.astype(BF16), b_st), 0.0)
    p_rk = jnp.where(incl, _dot_nt(r_st.astype(BF16), k_st), 0.0)
    yield

    eye = jnp.where(lax.broadcasted_iota(jnp.int32, (n_st, n_st), 0)
                    == lax.broadcasted_iota(jnp.int32, (n_st, n_st), 1), 1.0, 0.0)
    tinv = eye + l_ab
    lp = l_ab
    levels = int(np.log2(sub_len))
    for _ in range(levels - 1):
        lpb = lp.astype(BF16)
        lp = _dot(lpb, lpb)
        yield
        tinv = tinv + _dot(lp.astype(BF16), tinv.astype(BF16))
        yield

    x0_parts, r0_parts = [], []
    for s in range(nseq):
        st = s_ref[s_base + s].astype(BF16)
        sl = slice(s * sub_len, (s + 1) * sub_len)
        rows_s = jnp.concatenate([ag[sl], rg[sl]], axis=0).astype(BF16)
        xs = _dot_nt(rows_s, st)
        x0_parts.append(xs[0:sub_len])
        r0_parts.append(xs[sub_len:2 * sub_len])
    x0 = x0_parts[0] if nseq == 1 else jnp.concatenate(x0_parts, axis=0)
    r0 = r0_parts[0] if nseq == 1 else jnp.concatenate(r0_parts, axis=0)
    yield

    x_st = stack(x0) + _dot(l_ak.astype(BF16), v_st.astype(BF16))
    yield
    u_st = _dot(tinv.astype(BF16), x_st.astype(BF16))
    yield
    o_st = stack(r0) + _dot(p_rb.astype(BF16), u_st.astype(BF16)) + _dot(p_rk.astype(BF16), v_st.astype(BF16))
    u_all = unstack(u_st)
    o = unstack(o_st)
    yield

    bd = (lax.broadcasted_iota(jnp.int32, (width, width), 0) // HD_B
          == lax.broadcasted_iota(jnp.int32, (width, width), 1) // HD_B)
    for s in range(nseq):
        sl = slice(s * sub_len, (s + 1) * sub_len)
        upd = (_dot_tn(u_all[sl].astype(BF16), bt[sl].astype(BF16))
               + _dot_tn(v[sl].astype(BF16), kt[sl].astype(BF16)))
        gam = jnp.exp(tot[s * sub_len:s * sub_len + 1])
        s_ref[s_base + s] = s_ref[s_base + s] * gam + jnp.where(bd, upd, 0.0)

    ones = ones_ref[...]
    mu = _dot_sel_r(o, ones) * (1.0 / HD_B)
    d = o - mu
    var = _dot_sel_r(d * d, ones) * (1.0 / HD_B)
    o = d * lax.rsqrt(var + GN_EPS) * gng_ref[...] + gnb_ref[...]
    o = o + _dot_sel_r(r * k * rk_ref[...], ones) * v
    y_ref[j] = (o * _silu(g_ref[j])).astype(BF16)


def _wkv(z3, prep, s0, cw, *, t_valid, nbatch, nseq, sub_len):
    nb, t_len, _ = z3.shape
    nc = -(-t_len // CHUNK)
    seq = pl.BlockSpec((nbatch, CHUNK, W_B), lambda b, c: (b, c, 0))
    wspecs = [pl.BlockSpec(w.shape, lambda b, c: (0,) * w.ndim) for w in cw]
    sspec = pl.BlockSpec((nbatch * nseq, W_B, W_B), lambda b, c: (b, 0, 0))
    return pl.pallas_call(
        functools.partial(_wkv_kernel, t_valid=t_valid, nbatch=nbatch, nseq=nseq, sub_len=sub_len),
        grid=(nb // nbatch, nc),
        in_specs=[seq] * 6 + [pl.BlockSpec((nbatch, CHUNK, W_B), lambda b, c: (b, c, 3)), sspec] + wspecs,
        out_specs=[seq, sspec],
        out_shape=[jax.ShapeDtypeStruct((nb, t_len, W_B), BF16),
                   jax.ShapeDtypeStruct(s0.shape, F32)],
        compiler_params=_cparams("parallel", "arbitrary"), name="wkv",
    )(*prep, z3, s0, *cw)


def _rwkv_weights(p, hmb, ones_bd):
    row2 = lambda v: v.reshape(1, -1)
    prep_w = (_pad_to(p['rk_mix'], (8, W_B)), p['rk_wr'].astype(BF16), p['rk_wk'].astype(BF16),
              p['rk_wv'].astype(BF16), row2(p['rk_w0']), _pad_to(p['rk_w1'], (W_B, LANES)).astype(BF16),
              _pad_to(p['rk_w2'], (LANES, W_B)).astype(BF16), row2(p['rk_a0']),
              _pad_to(p['rk_a1'], (W_B, LANES)).astype(BF16), _pad_to(p['rk_a2'], (LANES, W_B)).astype(BF16),
              row2(p['rk_kk']), row2(p['rk_ka']), ones_bd)
    wkv_w = (hmb, ones_bd, row2(p['rk_rk']), row2(p['rk_gn_g']), row2(p['rk_gn_b']))
    return prep_w, wkv_w


def _state_to_bd(s):
    rows = [jnp.pad(s[:, h], ((0, 0), (0, 0), (h * HD_B, W_B - (h + 1) * HD_B))) for h in range(H_B)]
    return jnp.concatenate(rows, axis=1)


def _state_from_bd(s):
    return jnp.stack([s[:, h * HD_B:(h + 1) * HD_B, h * HD_B:(h + 1) * HD_B] for h in range(H_B)], axis=1)


def _rwkv_branch_prompt(z3, prep_w, wkv_w):
    nb, t_len, _ = z3.shape
    prep = _rwkv_prep(z3.reshape(nb * t_len, Z_COLS), prep_w, seq_len=t_len, tm=t_len // 3)
    prep3 = tuple(a.reshape(nb, t_len, W_B) for a in prep)
    y, s_bd = _wkv(z3, prep3, jnp.zeros((nb, W_B, W_B), F32), wkv_w, t_valid=t_len,
                   nbatch=min(nb, WKV_BATCH), nseq=1, sub_len=CHUNK)
    return y, _state_from_bd(s_bd)


def _rwkv_branch_sample(z3, shift_prev, s_prev, prep_w, wkv_w):
    nb, ts, _ = z3.shape
    z = z3.reshape(nb * ts, Z_COLS)
    prep = _rwkv_prep(z, prep_w, seq_len=ts, tm=nb * ts, prev_ext=jnp.repeat(shift_prev, ts, axis=0))
    per_chunk = CHUNK // ts
    groups = nb // per_chunk
    prep3 = tuple(a.reshape(groups, CHUNK, W_B) for a in prep)
    y, s_bd = _wkv(z.reshape(groups, CHUNK, Z_COLS), prep3, _state_to_bd(s_prev), wkv_w,
                   t_valid=CHUNK, nbatch=1, nseq=per_chunk, sub_len=ts)
    return y.reshape(nb, ts, W_B), _state_from_bd(s_bd)


def _mla_prep_kernel(cq_ref, ckv_ref, za_ref, zb_ref, cos_ref, sin_ref, qn_ref, kvn_ref, wr1_ref, wr2_ref,
                     ckv_out, krt_out, kp_out, cqn_out, of_out, *, t_valid):
    tm = cq_ref.shape[0]
    row = lax.broadcasted_iota(jnp.int32, (tm, 1), 0)
    valid = (pl.program_id(1) * tm + row) < t_valid
    cq = cq_ref[...]
    cqn = cq * lax.rsqrt(jnp.mean(cq * cq, axis=-1, keepdims=True) + 1e-6) * qn_ref[...]
    cv = ckv_ref[...]
    ckv = cv * lax.rsqrt(jnp.mean(cv * cv, axis=-1, keepdims=True) + 1e-6) * kvn_ref[...]
    cos = cos_ref[...]
    sin = sin_ref[...]
    za = za_ref[...]
    zb = zb_ref[...]
    kr1 = za * cos - zb * sin
    kr2 = za * sin + zb * cos
    cqb = cqn.astype(BF16)
    q1 = _dot(cqb, wr1_ref[...])
    q2 = _dot(cqb, wr2_ref[...])
    o1 = q1 * cos - q2 * sin
    o2 = q1 * sin + q2 * cos
    ckv_out[...] = ckv
    krt_out[...] = jnp.concatenate([kr1[:, 0:ROPE_HALF], kr2[:, 0:ROPE_HALF]], axis=1)
    kp_out[:, 0:KV_RANK] = jnp.where(valid, ckv, 0.0).astype(BF16)
    kp_out[:, KV_RANK:KV_RANK + LANES] = jnp.where(valid, kr1, 0.0).astype(BF16)
    kp_out[:, KV_RANK + LANES:] = jnp.where(valid, kr2, 0.0).astype(BF16)
    cqn_out[...] = jnp.where(valid, cqn, 0.0).astype(BF16)
    of_out[:, 0:LANES] = jnp.where(valid, o1 * Q_SCALE, 0.0).astype(BF16)
    of_out[:, LANES:] = jnp.where(valid, o2 * Q_SCALE, 0.0).astype(BF16)


def _mla_prep(z3, cos, sin, mw, *, tm, tp):
    nb, t_len, _ = z3.shape
    nt = tp // tm
    last = -(-t_len // tm) - 1
    wspecs = [pl.BlockSpec(w.shape, lambda b, j: (0,) * w.ndim) for w in mw]
    real = lambda width, col: pl.BlockSpec((None, tm, width), lambda b, j: (b, jnp.minimum(j, last), col))
    blk = lambda width: pl.BlockSpec((None, tm, width), lambda b, j: (b, j, 0))
    tab = pl.BlockSpec((tm, LANES), lambda b, j: (jnp.minimum(j, last), 0))
    return pl.pallas_call(
        functools.partial(_mla_prep_kernel, t_valid=t_len),
        grid=(nb, nt),
        in_specs=[real(256, 4), real(256, 5), real(LANES, 16), real(LANES, 17), tab, tab] + wspecs,
        out_specs=[real(256, 0), real(ROPE_DIM, 0), blk(512), blk(256), blk(256)],
        out_shape=[jax.ShapeDtypeStruct((nb, t_len, KV_RANK), F32),
                   jax.ShapeDtypeStruct((nb, t_len, ROPE_DIM), F32),
                   jax.ShapeDtypeStruct((nb, tp, 512), BF16),
                   jax.ShapeDtypeStruct((nb, tp, Q_RANK), BF16),
                   jax.ShapeDtypeStruct((nb, tp, 256), BF16)],
        compiler_params=_cparams("parallel", "parallel"), name="mla_prep",
    )(z3, z3, z3, z3, cos, sin, *mw)


def _fold_kernel(wq_ref, wk_ref, o_ref):
    o_ref[...] = _dot_hi(wq_ref[...], wk_ref[...]).astype(BF16)


def _fold_q(wq_nope, wuk_t):
    return pl.pallas_call(
        _fold_kernel,
        grid=(H_C,),
        in_specs=[pl.BlockSpec((None, Q_RANK, NOPE_DIM), lambda h: (h, 0, 0)),
                  pl.BlockSpec((None, NOPE_DIM, KV_RANK), lambda h: (h, 0, 0))],
        out_specs=pl.BlockSpec((Q_RANK, KV_RANK), lambda h: (0, h)),
        out_shape=jax.ShapeDtypeStruct((Q_RANK, H_C * KV_RANK), BF16),
        compiler_params=_cparams("parallel"), name="fold_q",
    )(wq_nope, wuk_t)


def _mla_weights(p):
    row2 = lambda v: v.reshape(1, -1)
    wq = p['w_uq'].reshape(Q_RANK, H_C, NOPE_DIM + ROPE_DIM)
    wq_nope = jnp.transpose(wq[:, :, 0:NOPE_DIM], (1, 0, 2))
    wr1 = wq[:, :, NOPE_DIM:NOPE_DIM + ROPE_HALF].reshape(Q_RANK, LANES).astype(BF16)
    wr2 = wq[:, :, NOPE_DIM + ROPE_HALF:].reshape(Q_RANK, LANES).astype(BF16)
    wuk_t = jnp.transpose(p['w_uk'], (1, 2, 0))
    wf = _fold_q(wq_nope, wuk_t)
    mla_w = (row2(p['q_norm']), row2(p['kv_norm']), wr1, wr2)
    eye = jnp.eye(2, dtype=F32)[jnp.arange(H_C) % 2]
    wuv = jnp.einsum('rhv,hg->hrgv', p['w_uv'], eye).reshape(H_C, KV_RANK, 2 * V_DIM).astype(BF16)
    return mla_w, wf, wuv


def _build_q(cqn, of, wf, hm, tq):
    q_all = _dot(cqn, wf) * Q_SCALE
    lat = [q_all[:, h * KV_RANK:(h + 1) * KV_RANK].astype(BF16) for h in range(H_C)]
    rope = [of * hm[h:h + 1].astype(BF16) for h in range(H_C)]
    return jnp.concatenate(lat, axis=0), jnp.concatenate(rope, axis=0)


def _project_out(o, wuv_ref, tq):
    blocks = []
    for g in range(H_C // 2):
        h0, h1 = 2 * g, 2 * g + 1
        blocks.append(_dot(o[h0 * tq:(h0 + 1) * tq].astype(BF16), wuv_ref[h0])
                      + _dot(o[h1 * tq:(h1 + 1) * tq].astype(BF16), wuv_ref[h1]))
    return jnp.concatenate(blocks, axis=1)


def _attn_prompt_kernel(qt_ref, kt_ref, cqn_ref, of_ref, kp_ref, wf_ref, hm_ref, wuv_ref, *refs, has_prev):
    y_ref, ql_scr, qr_scr, m_scr, acc_scr = refs[1:] if has_prev else refs
    tq = cqn_ref.shape[0]
    tk = kp_ref.shape[0]
    t = pl.program_id(1)
    qi = qt_ref[t]
    ki = kt_ref[t]
    last = (qi * tq + tq - 1) // tk

    @pl.when(ki == 0)
    def _():
        lat, rope = _build_q(cqn_ref[...], of_ref[...], wf_ref[...], hm_ref[...], tq)
        ql_scr[...] = lat
        qr_scr[...] = rope
        m_scr[...] = jnp.full(m_scr.shape, NEG_INF, F32)
        acc_scr[...] = jnp.zeros(acc_scr.shape, F32)

    def step(masked, nk=tk):
        kp = kp_ref[0:nk, :]
        ckv = kp[:, 0:KV_RANK]
        krope = kp[:, KV_RANK:]
        gr = max(1, min(H_C, ATT_TQ // tq)) * tq
        n_groups = H_C * tq // gr

        def scores(g):
            rows = pl.ds(g * gr, gr)
            return _dot_nt(ql_scr[rows, :], ckv) + _dot_nt(qr_scr[rows, :], krope)

        skew = min(ATT_SKEW, n_groups)
        ahead = [scores(g) for g in range(skew)]
        for g in range(n_groups):
            rows = pl.ds(g * gr, gr)
            s = ahead.pop(0)
            if g + skew < n_groups:
                ahead.append(scores(g + skew))
            if masked:
                qpos = qi * tq + lax.broadcasted_iota(jnp.int32, s.shape, 0) % tq
                kpos = ki * tk + lax.broadcasted_iota(jnp.int32, s.shape, 1)
                s = jnp.where(kpos <= qpos, s, NEG_INF)
            m_old = m_scr[rows, :]
            m_new = jnp.maximum(m_old, jnp.max(s, axis=-1, keepdims=True))
            alpha = jnp.exp2(m_old - m_new)
            pt = [jnp.exp2(st - m_new) for st in _lane_tiles(s)]
            p = jnp.concatenate(pt, axis=1).astype(BF16)
            pv = jnp.concatenate([_dot(p, ckv), functools.reduce(lambda x, y: x + y, pt)], axis=1)
            acc = acc_scr[rows, :]
            acc_scr[rows, :] = jnp.concatenate([at * alpha for at in _lane_tiles(acc)], axis=1) + pv
            m_scr[rows, :] = m_new

    @pl.when(ki < last)
    def _():
        step(False)

    visible = qi * tq + tq - ki * tk

    @pl.when((ki == last) & (visible <= tk // 2))
    def _():
        step(True, tk // 2)

    @pl.when((ki == last) & (visible > tk // 2))
    def _():
        step(True)

    @pl.when(ki == last)
    def _():
        acc = acc_scr[...]
        l = jnp.sum(acc[:, KV_RANK:KV_RANK + LANES], axis=-1, keepdims=True)
        y_ref[...] = _project_out(acc[:, 0:KV_RANK] / l, wuv_ref, tq)


def _attn_prompt_tiles(cqn, of, kp, wf, hm, wuv, t_len, *, tq, q_tiles, y_prev=None):
    nb = kp.shape[0]
    tk = ATT_TK
    pairs = [(i, j) for i in q_tiles for j in range((i * tq + tq - 1) // tk + 1)]
    q_tab = jnp.asarray([p[0] for p in pairs], jnp.int32)
    k_tab = jnp.asarray([p[1] for p in pairs], jnp.int32)
    rows = H_C * tq
    const = lambda w: pl.BlockSpec(w.shape, lambda b, t, qt, kt: (0,) * w.ndim)
    in_specs = [pl.BlockSpec((None, tq, Q_RANK), lambda b, t, qt, kt: (b, qt[t], 0)),
                pl.BlockSpec((None, tq, 256), lambda b, t, qt, kt: (b, qt[t], 0)),
                pl.BlockSpec((None, tk, 512), lambda b, t, qt, kt: (b, kt[t], 0)),
                const(wf), const(hm), const(wuv)]
    args = [q_tab, k_tab, cqn, of, kp, wf, hm, wuv]
    aliases = {}
    if y_prev is not None:
        in_specs.append(pl.BlockSpec(memory_space=pl.ANY))
        args.append(y_prev)
        aliases = {len(args) - 1: 0}
    grid_spec = pltpu.PrefetchScalarGridSpec(
        num_scalar_prefetch=2,
        grid=(nb, len(pairs)),
        in_specs=in_specs,
        out_specs=pl.BlockSpec((None, tq, W_C), lambda b, t, qt, kt: (b, qt[t], 0)),
        scratch_shapes=[pltpu.VMEM((rows, KV_RANK), BF16), pltpu.VMEM((rows, 256), BF16),
                        pltpu.VMEM((rows, LANES), F32), pltpu.VMEM((rows, KV_RANK + LANES), F32)],
    )
    return pl.pallas_call(
        functools.partial(_attn_prompt_kernel, has_prev=y_prev is not None),
        grid_spec=grid_spec,
        out_shape=jax.ShapeDtypeStruct((nb, t_len, W_C), F32),
        input_output_aliases=aliases,
        compiler_params=_cparams("parallel", "arbitrary"), name="attn_prompt",
    )(*args)


def _attn_prompt(cqn, of, kp, wf, hm, wuv, t_len):
    n_full = t_len // ATT_TQ
    rem = t_len - n_full * ATT_TQ
    y = jnp.zeros((kp.shape[0], t_len, W_C), F32)
    y = _attn_prompt_tiles(cqn, of, kp, wf, hm, wuv, t_len, tq=ATT_TQ, q_tiles=range(n_full), y_prev=y)
    if rem:
        assert rem % 16 == 0 and (n_full * ATT_TQ) % rem == 0
        y = _attn_prompt_tiles(cqn, of, kp, wf, hm, wuv, t_len, tq=rem, q_tiles=[n_full * ATT_TQ // rem],
                               y_prev=y)
    return y


def _mla_branch_prompt(z3, mla_w, wf, wuv, hm):
    nb, t_len, _ = z3.shape
    tp = -(-t_len // ATT_TK) * ATT_TK
    cos, sin = _rope_tables(jnp.arange(tp))
    ckv, kr, kp, cqn, of = _mla_prep(z3, cos, sin, mla_w, tm=ATT_TQ, tp=tp)
    yc = _attn_prompt(cqn, of, kp, wf, hm, wuv, t_len)
    return yc, ckv, kr


def _page_copies(pt_ref, ckv_hbm, kr_hbm, kbuf, krbuf, sem, seq, slot, pages=None, *, layer, n_pages):
    out = []
    for pg in (range(n_pages) if pages is None else pages):
        pid = pt_ref[seq * n_pages + pg]
        out.append(pltpu.make_async_copy(ckv_hbm.at[layer, pid], kbuf.at[slot, pg], sem.at[slot, 0]))
        out.append(pltpu.make_async_copy(kr_hbm.at[layer, pid], krbuf.at[slot, pg], sem.at[slot, 1]))
    return out


def _attn_sample_kernel(pt_ref, ckv_hbm, kr_hbm, cqn_ref, of_ref, cnew_ref, knew_ref, wf_ref, hm_ref,
                        tile_ref, wuv_ref, y_ref, kbuf, krbuf, k_scr, kr_scr, sem, *, layer, n_pages):
    tq = cqn_ref.shape[1]
    page = kbuf.shape[2]
    span = SAMPLE_PAGES_PER_CHUNK * page
    n_chunks = n_pages // SAMPLE_PAGES_PER_CHUNK
    step = pl.program_id(0)
    last_step = pl.num_programs(0) - 1
    copies = functools.partial(_page_copies, pt_ref, ckv_hbm, kr_hbm, kbuf, krbuf, sem,
                               layer=layer, n_pages=n_pages)

    def attend(j, slot, seq_after):
        lat, rope = _build_q(cqn_ref[j], of_ref[j], wf_ref[...], hm_ref[...], tq)
        qr = _dot(rope, tile_ref[...]).astype(BF16)
        cnew = cnew_ref[j].astype(BF16)
        s_new = _dot_nt(lat, cnew) + _dot_nt(qr, knew_ref[j].astype(BF16))
        tpos = lax.broadcasted_iota(jnp.int32, s_new.shape, 0) % tq
        s_new = jnp.where(lax.broadcasted_iota(jnp.int32, s_new.shape, 1) <= tpos, s_new, NEG_INF)
        m = jnp.max(s_new, axis=-1, keepdims=True)
        p_new = jnp.exp2(s_new - m)
        l = jnp.sum(p_new, axis=-1, keepdims=True)
        acc = _dot(p_new.astype(BF16), cnew)

        def scores(c):
            pages = range(c * SAMPLE_PAGES_PER_CHUNK, (c + 1) * SAMPLE_PAGES_PER_CHUNK)
            for i in pages:
                k_scr[i * page:(i + 1) * page, :] = kbuf[slot, i].astype(BF16)
                kr_scr[:, i * page:(i + 1) * page] = krbuf[slot, i].astype(BF16)
            for cp in copies(seq_after, slot, pages):
                cp.start()
            cols = slice(c * span, (c + 1) * span)
            return _dot_nt(lat, k_scr[cols, :]) + _dot(qr, kr_scr[:, cols])

        ahead = [scores(c) for c in range(SAMPLE_SKEW)]
        for c in range(n_chunks):
            s = ahead.pop(0)
            if c + SAMPLE_SKEW < n_chunks:
                ahead.append(scores(c + SAMPLE_SKEW))
            m_new = jnp.maximum(m, jnp.max(s, axis=-1, keepdims=True))
            alpha = jnp.exp2(m - m_new)
            p = jnp.exp2(s - m_new)
            l = alpha * l + jnp.sum(p, axis=-1, keepdims=True)
            acc = alpha * acc + _dot(p.astype(BF16), k_scr[c * span:(c + 1) * span, :])
            m = m_new
        y_ref[j] = _project_out(acc / l, wuv_ref, tq)

    seq0 = 2 * step

    @pl.when(step == 0)
    def _():
        for slot in range(2):
            for cp in copies(seq0 + slot, slot):
                cp.start()

    after = [jnp.minimum(seq0 + 2 + slot, 2 * last_step + slot) for slot in range(2)]
    for slot in range(2):
        for cp in copies(seq0 + slot, slot):
            cp.wait()
        attend(slot, slot, after[slot])

    @pl.when(step == last_step)
    def _():
        for slot in range(2):
            for cp in copies(after[slot], slot):
                cp.wait()


def _attn_sample(page_table, cache_ckv, cache_krope_t, layer, cqn, of, cnew, knew, wf, hm, tile_t, wuv):
    nb, tq, _ = cqn.shape
    n_pages = page_table.shape[1]
    page = cache_ckv.shape[2]
    assert nb % 2 == 0
    per_step = lambda width: pl.BlockSpec((2, tq, width), lambda i, pt: (i, 0, 0))
    const = lambda w: pl.BlockSpec(w.shape, lambda i, pt: (0,) * w.ndim)
    hbm = pl.BlockSpec(memory_space=pl.ANY)
    grid_spec = pltpu.PrefetchScalarGridSpec(
        num_scalar_prefetch=1,
        grid=(nb // 2,),
        in_specs=[hbm, hbm, per_step(Q_RANK), per_step(256), per_step(KV_RANK), per_step(ROPE_DIM),
                  const(wf), const(hm), const(tile_t), const(wuv)],
        out_specs=per_step(W_C),
        scratch_shapes=[pltpu.VMEM((2, n_pages, page, KV_RANK), F32),
                        pltpu.VMEM((2, n_pages, ROPE_DIM, page), F32),
                        pltpu.VMEM((n_pages * page, KV_RANK), BF16),
                        pltpu.VMEM((ROPE_DIM, n_pages * page), BF16),
                        pltpu.SemaphoreType.DMA((2, 2))],
    )
    return pl.pallas_call(
        functools.partial(_attn_sample_kernel, layer=layer, n_pages=n_pages),
        grid_spec=grid_spec,
        out_shape=jax.ShapeDtypeStruct((nb, tq, W_C), F32),
        compiler_params=_cparams("arbitrary"), name="attn_sample",
    )(page_table.reshape(-1), cache_ckv, cache_krope_t, cqn, of, cnew, knew, wf, hm, tile_t, wuv)


def _mla_branch_sample(z3, page_table, cache_ckv, cache_krope_t, layer, n_past, mla_w, wf, wuv, hm, tile_t):
    nb, ts, _ = z3.shape
    cos, sin = _rope_tables(n_past + jnp.arange(nb * ts) % ts)
    ckv, kr, _, cqn, of = _mla_prep(z3.reshape(1, nb * ts, Z_COLS), cos, sin, mla_w, tm=ATT_TQ, tp=nb * ts)
    ckv = ckv.reshape(nb, ts, KV_RANK)
    kr = kr.reshape(nb, ts, ROPE_DIM)
    yc = _attn_sample(page_table, cache_ckv, cache_krope_t, layer, cqn.reshape(nb, ts, Q_RANK),
                      of.reshape(nb, ts, 256), ckv, kr, wf, hm, tile_t, wuv)
    return yc, ckv, kr


def _out_kernel(ya_ref, yb_ref, yc_ref, gc_ref, x_ref, wa_ref, wb_ref, wc_ref, g_ref, b_ref, o_ref):
    yc = (yc_ref[...] * _silu(gc_ref[...])).astype(BF16)
    out = _dot(ya_ref[...], wa_ref[...]) + _dot(yb_ref[...], wb_ref[...]) + _dot(yc, wc_ref[...])
    h = DN_ALPHA * x_ref[...] + out
    mu = jnp.mean(h, axis=-1, keepdims=True)
    d = h - mu
    var = jnp.mean(d * d, axis=-1, keepdims=True)
    o_ref[...] = d * lax.rsqrt(var + 1e-5) * g_ref[...] + b_ref[...]


def _out_proj(ya, yb, yc, z, x, ow, tm):
    n = x.shape[0]
    wspecs = [pl.BlockSpec(w.shape, lambda i: (0,) * w.ndim) for w in ow]
    rowblk = lambda width, col=0: pl.BlockSpec((tm, width), lambda i: (i, col))
    return pl.pallas_call(
        _out_kernel,
        grid=(n // tm,),
        in_specs=[rowblk(W_A), rowblk(W_B), rowblk(W_C), rowblk(W_C, 3), rowblk(D_MODEL)] + wspecs,
        out_specs=rowblk(D_MODEL),
        out_shape=jax.ShapeDtypeStruct((n, D_MODEL), F32),
        compiler_params=_cparams("parallel"), name="out_proj",
    )(ya, yb, yc, z, x, *ow)


def _out_proj_tail(ya, yb, yc, z3, x, ow, *, skip, tm):
    nb, t_len, _ = z3.shape
    keep = t_len - skip
    seq3 = lambda a: a.reshape(nb, t_len, a.shape[-1])
    wspecs = [pl.BlockSpec(w.shape, lambda b, j: (0,) * w.ndim) for w in ow]
    rows = lambda width, col=0: pl.BlockSpec((pl.Element(1), pl.Element(tm), pl.Element(width)),
                                             lambda b, j: (b, pl.multiple_of(skip + j * tm, skip), col * width))

    def tail_kernel(ya_ref, yb_ref, yc_ref, gc_ref, x_ref, *rest):
        _out_kernel(ya_ref.at[0], yb_ref.at[0], yc_ref.at[0], gc_ref.at[0], x_ref.at[0], *rest)

    return pl.pallas_call(
        tail_kernel,
        grid=(nb, keep // tm),
        in_specs=[rows(W_A), rows(W_B), rows(W_C), rows(W_C, 3), rows(D_MODEL)] + wspecs,
        out_specs=pl.BlockSpec((None, tm, D_MODEL), lambda b, j: (b, j, 0)),
        out_shape=jax.ShapeDtypeStruct((nb, keep, D_MODEL), F32),
        compiler_params=_cparams("parallel", "parallel"), name="out_proj_tail",
    )(seq3(ya), seq3(yb), seq3(yc), z3, seq3(x), *ow)


def _block_diag(blocks):
    nblk, bw, _ = blocks.shape
    eye = jnp.eye(nblk, dtype=blocks.dtype)
    return jnp.einsum('hij,hg->higj', blocks, eye).reshape(nblk * bw, nblk * bw)


def _pad_to(x, shape):
    return jnp.pad(x, [(0, s - d) for d, s in zip(x.shape, shape)])


def _rope_tables(pos):
    inv = ROPE_THETA ** (-2.0 * jnp.arange(ROPE_HALF, dtype=F32) / ROPE_DIM)
    ang = pos.astype(F32)[:, None] * inv[None, :]
    return jnp.tile(jnp.cos(ang), (1, H_C)), jnp.tile(jnp.sin(ang), (1, H_C))


def _static_tables():
    lane = np.arange(256)
    hm = np.stack([((lane % LANES) // ROPE_HALF == h) for h in range(H_C)]).astype(np.float32)
    hmb = np.stack([(lane // HD_B == h) for h in range(H_B)]).astype(np.float32)
    ones_bd = (lane[:, None] // HD_B == lane[None, :] // HD_B).astype(np.float32)
    tile_t = np.zeros((256, ROPE_DIM), np.float32)
    for ln in lane:
        tile_t[ln, (ln // LANES) * ROPE_HALF + ln % ROPE_HALF] = 1.0
    return jnp.asarray(hm), jnp.asarray(hmb), jnp.asarray(ones_bd, BF16), jnp.asarray(tile_t, BF16)


_PER_LAYER = ('w_in', 'conv_w', 'conv_b', 'lru_wr', 'lru_br', 'lru_wi', 'lru_bi', 'lru_lambda', 'rk_mix',
              'rk_wr', 'rk_wk', 'rk_wv', 'rk_w0', 'rk_w1', 'rk_w2', 'rk_a0', 'rk_a1', 'rk_a2', 'rk_kk',
              'rk_ka', 'rk_rk', 'rk_gn_g', 'rk_gn_b', 'q_norm', 'kv_norm', 'w_uq', 'w_uk', 'w_uv', 'w_out',
              'ln_g', 'ln_b')


def kernel(x_prompt, x_sample, cache_ckv, cache_krope, state_conv, state_lru, state_shift, state_wkv,
           page_table, meta_tokens, w_in, conv_w, conv_b, lru_wr, lru_br, lru_wi, lru_bi, lru_lambda,
           rk_mix, rk_wr, rk_wk, rk_wv, rk_w0, rk_w1, rk_w2, rk_a0, rk_a1, rk_a2, rk_kk, rk_ka, rk_rk,
           rk_gn_g, rk_gn_b, q_norm, kv_norm, w_uq, w_uk, w_uv, w_out, ln_g, ln_b):
    stacked = dict(zip(_PER_LAYER, (w_in, conv_w, conv_b, lru_wr, lru_br, lru_wi, lru_bi, lru_lambda, rk_mix,
                                    rk_wr, rk_wk, rk_wv, rk_w0, rk_w1, rk_w2, rk_a0, rk_a1, rk_a2, rk_kk,
                                    rk_ka, rk_rk, rk_gn_g, rk_gn_b, q_norm, kv_norm, w_uq, w_uk, w_uv, w_out,
                                    ln_g, ln_b)))
    bp, seq, _ = x_prompt.shape
    bs, ts, _ = x_sample.shape
    tp_len = N_META + seq
    n_past = page_table.shape[1] * cache_ckv.shape[2]
    depth = w_in.shape[0]
    assert ts == 8 and CHUNK % ts == 0 and bs % (CHUNK // ts) == 0
    assert tp_len % 24 == 0 and seq % TAIL_TILE == 0 and N_META % 8 == 0

    hm, hmb, ones_bd, tile_t = _static_tables()
    row2 = lambda v: v.reshape(1, -1)
    cache_krope_t = jnp.swapaxes(cache_krope, 2, 3)

    xp = jnp.concatenate(
        [jnp.broadcast_to(meta_tokens[None].astype(x_prompt.dtype), (bp, N_META, D_MODEL)), x_prompt],
        axis=1).reshape(bp * tp_len, D_MODEL)
    xs = x_sample.reshape(bs * ts, D_MODEL)
    tm_p = tp_len // 3
    tm_s = bs * ts

    p_states, s_states = [], []
    for l in range(depth):
        p = {name: arr[l] for name, arr in stacked.items()}
        wi = p['w_in']
        kr_cols = wi[:, 1536:1568]
        w_in_r = jnp.concatenate(
            [wi[:, 0:1536], wi[:, 1568:2080],
             jnp.tile(kr_cols[:, 0:ROPE_HALF], (1, H_C)), jnp.tile(kr_cols[:, ROPE_HALF:], (1, H_C))],
            axis=1).astype(BF16)
        lru_w = _rglru_weights(p)
        prep_w, wkv_w = _rwkv_weights(p, hmb, ones_bd)
        mla_w, wf, wuv = _mla_weights(p)
        wo = p['w_out'].astype(BF16)
        out_w = (wo[0:W_A], wo[W_A:W_A + W_B], wo[W_A + W_B:], row2(p['ln_g']), row2(p['ln_b']))

        z = _in_proj(xp, w_in_r, tm_p)
        z3 = z.reshape(bp, tp_len, Z_COLS)
        ya, lru_new = _rglru_branch_prompt(z3, lru_w)
        yb, wkv_new = _rwkv_branch_prompt(z3, prep_w, wkv_w)
        yc, ckv, kr = _mla_branch_prompt(z3, mla_w, wf, wuv, hm)
        if l + 1 < depth:
            xp = _out_proj(ya, yb.reshape(bp * tp_len, W_B), yc.reshape(bp * tp_len, W_C), z, xp, out_w, tm_p)
        else:
            y_prompt = _out_proj_tail(ya, yb, yc, z3, xp, out_w, skip=N_META, tm=TAIL_TILE)
        p_states.append((ckv, kr, z3[:, tp_len - (CONV_W - 1):, 0:W_A], lru_new,
                         z3[:, tp_len - 1, 2 * W_A:2 * W_A + W_B], wkv_new))

        z = _in_proj(xs, w_in_r, tm_s)
        z3 = z.reshape(bs, ts, Z_COLS)
        ya, lru_new = _rglru_branch_sample(z3, state_conv[l], state_lru[l], n_past, lru_w)
        yb, wkv_new = _rwkv_branch_sample(z3, state_shift[l], state_wkv[l], prep_w, wkv_w)
        yc, ckv, kr = _mla_branch_sample(z3, page_table, cache_ckv, cache_krope_t, l, n_past,
                                         mla_w, wf, wuv, hm, tile_t)
        x_new = _out_proj(ya, yb.reshape(bs * ts, W_B), yc.reshape(bs * ts, W_C), z, xs, out_w, tm_s)
        s_states.append((ckv, kr, z3[:, ts - (CONV_W - 1):, 0:W_A], lru_new,
                         z3[:, ts - 1, 2 * W_A:2 * W_A + W_B], wkv_new))
        xs = x_new

    y_sample = xs.reshape(bs, ts, D_MODEL)
    p_out = [jnp.stack([st[i] for st in p_states]) for i in range(6)]
    s_out = [jnp.stack([st[i] for st in s_states]) for i in range(6)]
    return (y_prompt, y_sample, *p_out, *s_out)
```

```python
import functools
import itertools

import numpy as np
import jax
import jax.numpy as jnp
from jax import lax
from jax.experimental import pallas as pl
from jax.experimental.pallas import tpu as pltpu

F32 = jnp.float32
BF16 = jnp.bfloat16
HIGHEST = lax.Precision.HIGHEST

D_MODEL = 1024
N_META = 16
W_A = 256
W_B = 256
W_C = 512
CONV_W = 4
LRU_C = 8.0
HD_B = 64
H_B = 4
GN_EPS = 64e-5
H_C = 8
V_DIM = 64
NOPE_DIM = 64
ROPE_DIM = 32
ROPE_HALF = 16
Q_RANK = 256
KV_RANK = 256
ROPE_THETA = 10000.0
ATTN_SCALE = (NOPE_DIM + ROPE_DIM) ** -0.5
LOG2_E = 1.4426950408889634
Q_SCALE = ATTN_SCALE * LOG2_E
NEG_INF = -1e30
DEPTH = 2
DN_ALPHA = (2 * DEPTH) ** 0.25

LANES = 128
Z_COLS = 2304
CHUNK = 64
WKV_BATCH = 4
TAIL_TILE = 512
ATT_TQ = 256
ATT_TK = 1024
ATT_SKEW = 3
SAMPLE_PAGES_PER_CHUNK = 8
SAMPLE_SKEW = 3

VMEM_LIMIT = 48 * 1024 * 1024


def _cparams(*sem):
    return pltpu.CompilerParams(dimension_semantics=sem, vmem_limit_bytes=VMEM_LIMIT)


def _dot(a, b):
    return jnp.dot(a, b, preferred_element_type=F32)


def _dot_nt(a, b):
    return lax.dot_general(a, b, (((1,), (1,)), ((), ())), preferred_element_type=F32)


def _dot_tn(a, b):
    return lax.dot_general(a, b, (((0,), (0,)), ((), ())), preferred_element_type=F32)


def _dot_hi(a, b):
    return jnp.dot(a, b, preferred_element_type=F32, precision=HIGHEST)


def _split(x):
    hi = x.astype(BF16)
    return hi, (x - hi.astype(F32)).astype(BF16)


def _split3(x):
    hi, mid = _split(x)
    return hi, mid, (x - hi.astype(F32) - mid.astype(F32)).astype(BF16)


def _dot_sel(sel, b):
    b_hi, b_mid, b_lo = _split3(b)
    return _dot(sel, b_hi) + (_dot(sel, b_mid) + _dot(sel, b_lo))


def _dot_sel_r(a, sel):
    a_hi, a_mid, a_lo = _split3(a)
    return _dot(a_hi, sel) + (_dot(a_mid, sel) + _dot(a_lo, sel))


def _softplus(x):
    return jnp.maximum(x, 0.0) + jnp.log1p(jnp.exp(-jnp.abs(x)))


def _silu(x):
    return x * jax.nn.sigmoid(x)


def _lane_tiles(x):
    return [x[:, j * LANES:(j + 1) * LANES] for j in range(x.shape[1] // LANES)]


def _mm_kernel(x_ref, w_ref, o_ref):
    o_ref[...] = _dot(x_ref[...].astype(BF16), w_ref[...])


def _in_proj(x, w, tm):
    n, k = x.shape
    m = w.shape[1]
    return pl.pallas_call(
        _mm_kernel,
        grid=(n // tm,),
        in_specs=[pl.BlockSpec((tm, k), lambda i: (i, 0)), pl.BlockSpec((k, m), lambda i: (0, 0))],
        out_specs=pl.BlockSpec((tm, m), lambda i: (i, 0)),
        out_shape=jax.ShapeDtypeStruct((n, m), F32),
        compiler_params=_cparams("parallel"),
        name="in_proj",
    )(x, w)


def _rglru_kernel(*refs, seq_len, pos0, has_state):
    if has_state:
        (u_ref, g_ref, f_ref, hp_ref, cw_ref, cb_ref, wr_ref, br_ref, wi_ref, bi_ref, lam_ref,
         y_ref, h_ref) = refs
    else:
        (u_ref, g_ref, cw_ref, cb_ref, wr_ref, br_ref, wi_ref, bi_ref, lam_ref,
         y_ref, h_ref, a_scr, b_scr) = refs
    u = u_ref[...]
    rows = u.shape[0]
    row = lax.broadcasted_iota(jnp.int32, u.shape, 0)
    t = row if rows == seq_len else row & (seq_len - 1)

    cw = cw_ref[...]
    xc = cb_ref[...] + u * cw[CONV_W - 1:CONV_W]
    for k in range(1, CONV_W):
        sh = pltpu.roll(u, k, 0)
        if has_state:
            f = f_ref[...]
            prev = f if k == CONV_W - 1 else pltpu.roll(f, rows - (CONV_W - 1 - k), 0)
        else:
            prev = 0.0
        sh = jnp.where(t < k, prev, sh)
        xc = xc + sh * cw[CONV_W - 1 - k:CONV_W - k]

    xcb = xc.astype(BF16)
    gate_r = jax.nn.sigmoid(_dot(xcb, wr_ref[...]) + br_ref[...])
    gate_i = jax.nn.sigmoid(_dot(xcb, wi_ref[...]) + bi_ref[...])
    log_a = (-LRU_C) * gate_r * _softplus(-lam_ref[...])
    th = jnp.tanh(log_a)
    neg_expm1 = -2.0 * th / (1.0 - th)
    reset = (t + pos0) == 0
    a = jnp.where(reset, 0.0, jnp.exp(log_a))
    mult = jnp.where(reset, 1.0, jnp.sqrt(neg_expm1))
    bv = xc * gate_i * mult

    t8 = row & 7
    for s in (1, 2, 4):
        a_s = jnp.where(t8 >= s, pltpu.roll(a, s, 0), 1.0)
        b_s = jnp.where(t8 >= s, pltpu.roll(bv, s, 0), 0.0)
        bv = a * b_s + bv
        a = a * a_s

    if has_state:
        h = bv + a * hp_ref[...]
        h_ref[...] = h
    else:
        a_scr[...] = a
        b_scr[...] = bv

        def body(i, carry):
            r0 = pl.multiple_of(i * 8, 8)
            hh = b_scr[pl.ds(r0, 8), :] + a_scr[pl.ds(r0, 8), :] * carry
            b_scr[pl.ds(r0, 8), :] = hh
            return jnp.broadcast_to(hh[7:8, :], hh.shape)

        carry = lax.fori_loop(0, rows // 8, body, jnp.zeros((8, u.shape[1]), F32))
        h = b_scr[...]
        h_ref[...] = carry[0:1, :]
    y_ref[...] = (h * _silu(g_ref[...])).astype(BF16)


def _rglru(z, lw, *, nseq, seq_len, pos0, conv_ext=None, h_ext=None):
    n = z.shape[0]
    wspecs = [pl.BlockSpec(w.shape, lambda i: (0,) * w.ndim) for w in lw]
    if conv_ext is None:
        rows = seq_len
        in_specs = [pl.BlockSpec((rows, W_A), lambda i: (i, 0)),
                    pl.BlockSpec((rows, W_A), lambda i: (i, 1))] + wspecs
        out_specs = [pl.BlockSpec((rows, W_A), lambda i: (i, 0)),
                     pl.BlockSpec((None, 1, W_A), lambda i: (i, 0, 0))]
        out_shape = [jax.ShapeDtypeStruct((n, W_A), BF16), jax.ShapeDtypeStruct((nseq, 1, W_A), F32)]
        scratch = [pltpu.VMEM((rows, W_A), F32), pltpu.VMEM((rows, W_A), F32)]
        args = (z, z) + tuple(lw)
        grid = (nseq,)
    else:
        rows = n
        full = pl.BlockSpec((rows, W_A), lambda i: (0, 0))
        in_specs = [full, pl.BlockSpec((rows, W_A), lambda i: (0, 1)), full, full] + wspecs
        out_specs = [full, full]
        out_shape = [jax.ShapeDtypeStruct((n, W_A), BF16), jax.ShapeDtypeStruct((n, W_A), F32)]
        scratch = []
        args = (z, z, conv_ext, h_ext) + tuple(lw)
        grid = (1,)
    return pl.pallas_call(
        functools.partial(_rglru_kernel, seq_len=seq_len, pos0=pos0, has_state=conv_ext is not None),
        grid=grid, in_specs=in_specs, out_specs=out_specs, out_shape=out_shape,
        scratch_shapes=scratch, compiler_params=_cparams("parallel"), name="rglru",
    )(*args)


def _rglru_weights(p):
    row2 = lambda v: v.reshape(1, -1)
    return (p['conv_w'], row2(p['conv_b']), _block_diag(p['lru_wr']).astype(BF16), row2(p['lru_br']),
            _block_diag(p['lru_wi']).astype(BF16), row2(p['lru_bi']), row2(p['lru_lambda']))


def _rglru_branch_prompt(z3, lru_w):
    nb, t_len, _ = z3.shape
    y, h_last = _rglru(z3.reshape(nb * t_len, Z_COLS), lru_w, nseq=nb, seq_len=t_len, pos0=0)
    return y, h_last.reshape(nb, W_A)


def _rglru_branch_sample(z3, conv_prev, h_prev, pos0, lru_w):
    nb, ts, _ = z3.shape
    conv_ext = _pad_to(conv_prev, (nb, ts, W_A)).reshape(nb * ts, W_A)
    h_ext = jnp.repeat(h_prev, ts, axis=0)
    y, h_all = _rglru(z3.reshape(nb * ts, Z_COLS), lru_w, nseq=nb, seq_len=ts, pos0=pos0,
                      conv_ext=conv_ext, h_ext=h_ext)
    return y, h_all.reshape(nb, ts, W_A)[:, ts - 1]


def _rwkv_prep_kernel(u_ref, p_ref, mix_ref, wr_ref, wk_ref, wv_ref, w0_ref, w1_ref, w2_ref,
                      a0_ref, a1_ref, a2_ref, kkw_ref, ka_ref, ones_ref,
                      r_ref, lw_ref, k_ref, v_ref, kk_ref, a_ref, *, seq_len, tiles_per_seq):
    u = u_ref[...]
    row = lax.broadcasted_iota(jnp.int32, u.shape, 0)
    rolled = pltpu.roll(u, 1, 0)
    if tiles_per_seq is None:
        u_prev = jnp.where((row & (seq_len - 1)) == 0, p_ref[...], rolled)
    else:
        first = (pl.program_id(0) % tiles_per_seq) == 0
        prow = jnp.where(first, 0.0, p_ref[7:8, :])
        u_prev = jnp.where(row == 0, prow, rolled)
    du = u_prev - u
    mix = mix_ref[...]
    xr = (u + du * mix[0:1]).astype(BF16)
    xw = (u + du * mix[1:2]).astype(BF16)
    xk = (u + du * mix[2:3]).astype(BF16)
    xv = (u + du * mix[3:4]).astype(BF16)
    xa = (u + du * mix[4:5]).astype(BF16)
    r = _dot(xr, wr_ref[...])
    k = _dot(xk, wk_ref[...])
    v = _dot(xv, wv_ref[...])
    lora_w = _dot(jnp.tanh(_dot(xw, w1_ref[...])).astype(BF16), w2_ref[...])
    w_log = -_softplus(-(w0_ref[...] + lora_w)) - 0.5
    a = jax.nn.sigmoid(a0_ref[...] + _dot(_dot(xa, a1_ref[...]).astype(BF16), a2_ref[...]))
    kk = k * kkw_ref[...]
    ssq = _dot_sel_r(kk * kk, ones_ref[...])
    kk = kk / jnp.maximum(jnp.sqrt(ssq), 1e-12)
    r_ref[...] = r.astype(BF16)
    lw_ref[...] = -jnp.exp(w_log)
    k_ref[...] = (k * (1.0 + (a - 1.0) * ka_ref[...])).astype(BF16)
    v_ref[...] = v.astype(BF16)
    kk_ref[...] = kk.astype(BF16)
    a_ref[...] = a.astype(BF16)


def _rwkv_prep(z, pw, *, seq_len, tm, prev_ext=None):
    n = z.shape[0]
    wspecs = [pl.BlockSpec(w.shape, lambda i: (0,) * w.ndim) for w in pw]
    if prev_ext is None:
        tiles = seq_len // tm
        p_arg = z
        p_spec = pl.BlockSpec((8, W_B), lambda i: (jnp.maximum(i * (tm // 8) - 1, 0), 2))
    else:
        tiles = None
        p_arg = prev_ext
        p_spec = pl.BlockSpec((tm, W_B), lambda i: (i, 0))
    out = [jax.ShapeDtypeStruct((n, W_B), F32 if i == 1 else BF16) for i in range(6)]
    ospec = pl.BlockSpec((tm, W_B), lambda i: (i, 0))
    return pl.pallas_call(
        functools.partial(_rwkv_prep_kernel, seq_len=seq_len, tiles_per_seq=tiles),
        grid=(n // tm,),
        in_specs=[pl.BlockSpec((tm, W_B), lambda i: (i, 2)), p_spec] + wspecs,
        out_specs=[ospec] * 6, out_shape=out,
        compiler_params=_cparams("parallel"), name="rwkv_prep",
    )(z, p_arg, *pw)


def _wkv_kernel(r_ref, lw_ref, k_ref, v_ref, kk_ref, a_ref, g_ref, s0_ref, hm_ref, ones_ref,
                rk_ref, gng_ref, gnb_ref, y_ref, s_ref, *, t_valid, nbatch, nseq, sub_len):
    c = pl.program_id(1)

    @pl.when(c == 0)
    def _():
        s_ref[...] = s0_ref[...]

    chains = [_wkv_chunk(c, j, r_ref, lw_ref, k_ref, v_ref, kk_ref, a_ref, g_ref, hm_ref, ones_ref,
                         rk_ref, gng_ref, gnb_ref, y_ref, s_ref, t_valid=t_valid, nseq=nseq, sub_len=sub_len)
              for j in range(nbatch)]
    for _ in itertools.zip_longest(*chains):
        pass


def _wkv_chunk(c, j, r_ref, lw_ref, k_ref, v_ref, kk_ref, a_ref, g_ref, hm_ref, ones_ref,
               rk_ref, gng_ref, gnb_ref, y_ref, s_ref, *, t_valid, nseq, sub_len):
    width = W_B
    nh = H_B
    s_base = j * nseq
    row = lax.broadcasted_iota(jnp.int32, (CHUNK, width), 0)
    valid = (c * CHUNK + row) < t_valid
    zero = jnp.zeros((CHUNK, width), F32)
    r = jnp.where(valid, r_ref[j].astype(F32), zero)
    lw = jnp.where(valid, lw_ref[j], zero)
    k = jnp.where(valid, k_ref[j].astype(F32), zero)
    v = jnp.where(valid, v_ref[j].astype(F32), zero)
    kk = jnp.where(valid, kk_ref[j].astype(F32), zero)
    a = jnp.where(valid, a_ref[j].astype(F32), zero)

    qi = lax.broadcasted_iota(jnp.int32, (CHUNK, CHUNK), 0)
    si = lax.broadcasted_iota(jnp.int32, (CHUNK, CHUNK), 1)
    same = (qi // sub_len) == (si // sub_len)
    cum = _dot_sel(jnp.where(same & (qi >= si), 1.0, 0.0).astype(BF16), lw)
    tot = _dot_sel(jnp.where(same, 1.0, 0.0).astype(BF16), lw)
    yield
    e_neg = jnp.exp(-cum)
    e_tail = jnp.exp(tot - cum)
    rg = r * jnp.exp(cum)
    ag = -kk * jnp.exp(cum - lw)
    kka = kk * a
    bg = kka * e_neg
    kg = k * e_neg
    bt = kka * e_tail
    kt = k * e_tail

    hm = hm_ref[...]

    def stack(x):
        return jnp.concatenate([x * hm[h:h + 1] for h in range(nh)], axis=0)

    def unstack(x):
        out = x[0:CHUNK]
        for h in range(1, nh):
            out = out + x[h * CHUNK:(h + 1) * CHUNK]
        return out

    a_st = stack(ag).astype(BF16)
    r_st = stack(rg)
    b_st = stack(bg).astype(BF16)
    k_st = stack(kg).astype(BF16)
    v_st = stack(v)

    n_st = nh * CHUNK
    rq = lax.broadcasted_iota(jnp.int32, (n_st, n_st), 0) % CHUNK
    cs = lax.broadcasted_iota(jnp.int32, (n_st, n_st), 1) % CHUNK
    same_st = (rq // sub_len) == (cs // sub_len)
    strict = same_st & (rq > cs)
    incl = same_st & (rq >= cs)

    gram = _dot_nt(jnp.concatenate([a_st, r_st.astype(BF16)], axis=0), jnp.concatenate([b_st, k_st], axis=0))
    l_ab = jnp.where(strict, gram[0:n_st, 0:n_st], 0.0)
    l_ak = jnp.where(strict, gram[0:n_st, n_st:], 0.0)
    p_rb = jnp.where(incl, gram[n_st:, 0:n_st], 0.0)
    p_rk = jnp.where(incl, gram[n_st:, n_st:], 0.0)
    yield

    eye = jnp.where(lax.broadcasted_iota(jnp.int32, (n_st, n_st), 0)
                    == lax.broadcasted_iota(jnp.int32, (n_st, n_st), 1), 1.0, 0.0)
    levels = int(np.log2(sub_len))
    tinv = eye + l_ab
    lpb = l_ab.astype(BF16)
    lp = _dot(lpb, lpb)
    yield
    for level in range(1, levels):
        lpb = lp.astype(BF16)
        if level + 1 < levels:
            both = _dot(lpb, jnp.concatenate([lpb, tinv.astype(BF16)], axis=1))
            lp = both[:, 0:n_st]
            tinv = tinv + both[:, n_st:]
        else:
            tinv = tinv + _dot(lpb, tinv.astype(BF16))
        yield

    x0_parts, r0_parts = [], []
    for s in range(nseq):
        st = s_ref[s_base + s].astype(BF16)
        sl = slice(s * sub_len, (s + 1) * sub_len)
        rows_s = jnp.concatenate([ag[sl], rg[sl]], axis=0).astype(BF16)
        xs = _dot_nt(rows_s, st)
        x0_parts.append(xs[0:sub_len])
        r0_parts.append(xs[sub_len:2 * sub_len])
    x0 = x0_parts[0] if nseq == 1 else jnp.concatenate(x0_parts, axis=0)
    r0 = r0_parts[0] if nseq == 1 else jnp.concatenate(r0_parts, axis=0)
    yield

    x_st = stack(x0) + _dot(l_ak.astype(BF16), v_st.astype(BF16))
    yield
    u_st = _dot(tinv.astype(BF16), x_st.astype(BF16))
    yield
    o_st = stack(r0) + _dot(jnp.concatenate([p_rb, p_rk], axis=1).astype(BF16),
                            jnp.concatenate([u_st, v_st], axis=0).astype(BF16))
    u_all = unstack(u_st)
    o = unstack(o_st)
    yield

    bd = (lax.broadcasted_iota(jnp.int32, (width, width), 0) // HD_B
          == lax.broadcasted_iota(jnp.int32, (width, width), 1) // HD_B)
    for s in range(nseq):
        sl = slice(s * sub_len, (s + 1) * sub_len)
        upd = (_dot_tn(u_all[sl].astype(BF16), bt[sl].astype(BF16))
               + _dot_tn(v[sl].astype(BF16), kt[sl].astype(BF16)))
        gam = jnp.exp(tot[s * sub_len:s * sub_len + 1])
        s_ref[s_base + s] = s_ref[s_base + s] * gam + jnp.where(bd, upd, 0.0)

    ones = ones_ref[...]
    mu = _dot_sel_r(o, ones) * (1.0 / HD_B)
    d = o - mu
    var = _dot_sel_r(d * d, ones) * (1.0 / HD_B)
    o = d * lax.rsqrt(var + GN_EPS) * gng_ref[...] + gnb_ref[...]
    o = o + _dot_sel_r(r * k * rk_ref[...], ones) * v
    y_ref[j] = (o * _silu(g_ref[j])).astype(BF16)


def _wkv(z3, prep, s0, cw, *, t_valid, nbatch, nseq, sub_len):
    nb, t_len, _ = z3.shape
    nc = -(-t_len // CHUNK)
    seq = pl.BlockSpec((nbatch, CHUNK, W_B), lambda b, c: (b, c, 0))
    wspecs = [pl.BlockSpec(w.shape, lambda b, c: (0,) * w.ndim) for w in cw]
    sspec = pl.BlockSpec((nbatch * nseq, W_B, W_B), lambda b, c: (b, 0, 0))
    return pl.pallas_call(
        functools.partial(_wkv_kernel, t_valid=t_valid, nbatch=nbatch, nseq=nseq, sub_len=sub_len),
        grid=(nb // nbatch, nc),
        in_specs=[seq] * 6 + [pl.BlockSpec((nbatch, CHUNK, W_B), lambda b, c: (b, c, 3)), sspec] + wspecs,
        out_specs=[seq, sspec],
        out_shape=[jax.ShapeDtypeStruct((nb, t_len, W_B), BF16),
                   jax.ShapeDtypeStruct(s0.shape, F32)],
        compiler_params=_cparams("parallel", "arbitrary"), name="wkv",
    )(*prep, z3, s0, *cw)


def _rwkv_weights(p, hmb, ones_bd):
    row2 = lambda v: v.reshape(1, -1)
    prep_w = (_pad_to(p['rk_mix'], (8, W_B)), p['rk_wr'].astype(BF16), p['rk_wk'].astype(BF16),
              p['rk_wv'].astype(BF16), row2(p['rk_w0']), _pad_to(p['rk_w1'], (W_B, LANES)).astype(BF16),
              _pad_to(p['rk_w2'], (LANES, W_B)).astype(BF16), row2(p['rk_a0']),
              _pad_to(p['rk_a1'], (W_B, LANES)).astype(BF16), _pad_to(p['rk_a2'], (LANES, W_B)).astype(BF16),
              row2(p['rk_kk']), row2(p['rk_ka']), ones_bd)
    wkv_w = (hmb, ones_bd, row2(p['rk_rk']), row2(p['rk_gn_g']), row2(p['rk_gn_b']))
    return prep_w, wkv_w


def _state_to_bd(s):
    rows = [jnp.pad(s[:, h], ((0, 0), (0, 0), (h * HD_B, W_B - (h + 1) * HD_B))) for h in range(H_B)]
    return jnp.concatenate(rows, axis=1)


def _state_from_bd(s):
    return jnp.stack([s[:, h * HD_B:(h + 1) * HD_B, h * HD_B:(h + 1) * HD_B] for h in range(H_B)], axis=1)


def _rwkv_branch_prompt(z3, prep_w, wkv_w):
    nb, t_len, _ = z3.shape
    prep = _rwkv_prep(z3.reshape(nb * t_len, Z_COLS), prep_w, seq_len=t_len, tm=t_len // 3)
    prep3 = tuple(a.reshape(nb, t_len, W_B) for a in prep)
    y, s_bd = _wkv(z3, prep3, jnp.zeros((nb, W_B, W_B), F32), wkv_w, t_valid=t_len,
                   nbatch=min(nb, WKV_BATCH), nseq=1, sub_len=CHUNK)
    return y, _state_from_bd(s_bd)


def _rwkv_branch_sample(z3, shift_prev, s_prev, prep_w, wkv_w):
    nb, ts, _ = z3.shape
    z = z3.reshape(nb * ts, Z_COLS)
    prep = _rwkv_prep(z, prep_w, seq_len=ts, tm=nb * ts, prev_ext=jnp.repeat(shift_prev, ts, axis=0))
    per_chunk = CHUNK // ts
    groups = nb // per_chunk
    prep3 = tuple(a.reshape(groups, CHUNK, W_B) for a in prep)
    y, s_bd = _wkv(z.reshape(groups, CHUNK, Z_COLS), prep3, _state_to_bd(s_prev), wkv_w,
                   t_valid=CHUNK, nbatch=1, nseq=per_chunk, sub_len=ts)
    return y.reshape(nb, ts, W_B), _state_from_bd(s_bd)


def _mla_prep_kernel(cq_ref, ckv_ref, za_ref, zb_ref, cos_ref, sin_ref, qn_ref, kvn_ref, wr1_ref, wr2_ref,
                     ckv_out, krt_out, kp_out, cqn_out, of_out, *, t_valid):
    tm = cq_ref.shape[0]
    row = lax.broadcasted_iota(jnp.int32, (tm, 1), 0)
    valid = (pl.program_id(1) * tm + row) < t_valid
    cq = cq_ref[...]
    cqn = cq * lax.rsqrt(jnp.mean(cq * cq, axis=-1, keepdims=True) + 1e-6) * qn_ref[...]
    cv = ckv_ref[...]
    ckv = cv * lax.rsqrt(jnp.mean(cv * cv, axis=-1, keepdims=True) + 1e-6) * kvn_ref[...]
    cos = cos_ref[...]
    sin = sin_ref[...]
    za = za_ref[...]
    zb = zb_ref[...]
    kr1 = za * cos - zb * sin
    kr2 = za * sin + zb * cos
    cqb = cqn.astype(BF16)
    q1 = _dot(cqb, wr1_ref[...])
    q2 = _dot(cqb, wr2_ref[...])
    o1 = q1 * cos - q2 * sin
    o2 = q1 * sin + q2 * cos
    ckv_out[...] = ckv
    krt_out[...] = jnp.concatenate([kr1[:, 0:ROPE_HALF], kr2[:, 0:ROPE_HALF]], axis=1)
    kp_out[:, 0:KV_RANK] = jnp.where(valid, ckv, 0.0).astype(BF16)
    kp_out[:, KV_RANK:KV_RANK + LANES] = jnp.where(valid, kr1, 0.0).astype(BF16)
    kp_out[:, KV_RANK + LANES:] = jnp.where(valid, kr2, 0.0).astype(BF16)
    cqn_out[...] = jnp.where(valid, cqn, 0.0).astype(BF16)
    of_out[:, 0:LANES] = jnp.where(valid, o1 * Q_SCALE, 0.0).astype(BF16)
    of_out[:, LANES:] = jnp.where(valid, o2 * Q_SCALE, 0.0).astype(BF16)


def _mla_prep(z3, cos, sin, mw, *, tm, tp):
    nb, t_len, _ = z3.shape
    nt = tp // tm
    last = -(-t_len // tm) - 1
    wspecs = [pl.BlockSpec(w.shape, lambda b, j: (0,) * w.ndim) for w in mw]
    real = lambda width, col: pl.BlockSpec((None, tm, width), lambda b, j: (b, jnp.minimum(j, last), col))
    blk = lambda width: pl.BlockSpec((None, tm, width), lambda b, j: (b, j, 0))
    tab = pl.BlockSpec((tm, LANES), lambda b, j: (jnp.minimum(j, last), 0))
    return pl.pallas_call(
        functools.partial(_mla_prep_kernel, t_valid=t_len),
        grid=(nb, nt),
        in_specs=[real(256, 4), real(256, 5), real(LANES, 16), real(LANES, 17), tab, tab] + wspecs,
        out_specs=[real(256, 0), real(ROPE_DIM, 0), blk(512), blk(256), blk(256)],
        out_shape=[jax.ShapeDtypeStruct((nb, t_len, KV_RANK), F32),
                   jax.ShapeDtypeStruct((nb, t_len, ROPE_DIM), F32),
                   jax.ShapeDtypeStruct((nb, tp, 512), BF16),
                   jax.ShapeDtypeStruct((nb, tp, Q_RANK), BF16),
                   jax.ShapeDtypeStruct((nb, tp, 256), BF16)],
        compiler_params=_cparams("parallel", "parallel"), name="mla_prep",
    )(z3, z3, z3, z3, cos, sin, *mw)


def _fold_kernel(wq_ref, wk_ref, o_ref):
    o_ref[...] = _dot_hi(wq_ref[...], wk_ref[...]).astype(BF16)


def _fold_q(wq_nope, wuk_t):
    return pl.pallas_call(
        _fold_kernel,
        grid=(H_C,),
        in_specs=[pl.BlockSpec((None, Q_RANK, NOPE_DIM), lambda h: (h, 0, 0)),
                  pl.BlockSpec((None, NOPE_DIM, KV_RANK), lambda h: (h, 0, 0))],
        out_specs=pl.BlockSpec((Q_RANK, KV_RANK), lambda h: (0, h)),
        out_shape=jax.ShapeDtypeStruct((Q_RANK, H_C * KV_RANK), BF16),
        compiler_params=_cparams("parallel"), name="fold_q",
    )(wq_nope, wuk_t)


def _mla_weights(p):
    row2 = lambda v: v.reshape(1, -1)
    wq = p['w_uq'].reshape(Q_RANK, H_C, NOPE_DIM + ROPE_DIM)
    wq_nope = jnp.transpose(wq[:, :, 0:NOPE_DIM], (1, 0, 2))
    wr1 = wq[:, :, NOPE_DIM:NOPE_DIM + ROPE_HALF].reshape(Q_RANK, LANES).astype(BF16)
    wr2 = wq[:, :, NOPE_DIM + ROPE_HALF:].reshape(Q_RANK, LANES).astype(BF16)
    wuk_t = jnp.transpose(p['w_uk'], (1, 2, 0))
    wf = _fold_q(wq_nope, wuk_t)
    mla_w = (row2(p['q_norm']), row2(p['kv_norm']), wr1, wr2)
    eye = jnp.eye(2, dtype=F32)[jnp.arange(H_C) % 2]
    wuv = jnp.einsum('rhv,hg->hrgv', p['w_uv'], eye).reshape(H_C, KV_RANK, 2 * V_DIM).astype(BF16)
    return mla_w, wf, wuv


def _build_q(cqn, of, wf, hm, tq):
    q_all = _dot(cqn, wf) * Q_SCALE
    lat = [q_all[:, h * KV_RANK:(h + 1) * KV_RANK].astype(BF16) for h in range(H_C)]
    rope = [of * hm[h:h + 1].astype(BF16) for h in range(H_C)]
    return jnp.concatenate(lat, axis=0), jnp.concatenate(rope, axis=0)


def _project_out(o, wuv_ref, tq):
    blocks = []
    for g in range(H_C // 2):
        h0, h1 = 2 * g, 2 * g + 1
        blocks.append(_dot(o[h0 * tq:(h0 + 1) * tq].astype(BF16), wuv_ref[h0])
                      + _dot(o[h1 * tq:(h1 + 1) * tq].astype(BF16), wuv_ref[h1]))
    return jnp.concatenate(blocks, axis=1)


def _attn_prompt_kernel(qt_ref, kt_ref, cqn_ref, of_ref, kp_ref, wf_ref, hm_ref, wuv_ref, *refs, has_prev):
    y_ref, ql_scr, qr_scr, m_scr, acc_scr = refs[1:] if has_prev else refs
    tq = cqn_ref.shape[0]
    tk = kp_ref.shape[0]
    t = pl.program_id(1)
    qi = qt_ref[t]
    ki = kt_ref[t]
    last = (qi * tq + tq - 1) // tk

    @pl.when(ki == 0)
    def _():
        lat, rope = _build_q(cqn_ref[...], of_ref[...], wf_ref[...], hm_ref[...], tq)
        ql_scr[...] = lat
        qr_scr[...] = rope
        m_scr[...] = jnp.full(m_scr.shape, NEG_INF, F32)
        acc_scr[...] = jnp.zeros(acc_scr.shape, F32)

    def step(masked, nk=tk):
        kp = kp_ref[0:nk, :]
        ckv = kp[:, 0:KV_RANK]
        krope = kp[:, KV_RANK:]
        gr = max(1, min(H_C, ATT_TQ // tq)) * tq
        n_groups = H_C * tq // gr

        def scores(g):
            rows = pl.ds(g * gr, gr)
            return _dot_nt(ql_scr[rows, :], ckv) + _dot_nt(qr_scr[rows, :], krope)

        skew = min(ATT_SKEW, n_groups)
        ahead = [scores(g) for g in range(skew)]
        for g in range(n_groups):
            rows = pl.ds(g * gr, gr)
            s = ahead.pop(0)
            if g + skew < n_groups:
                ahead.append(scores(g + skew))
            if masked:
                qpos = qi * tq + lax.broadcasted_iota(jnp.int32, s.shape, 0) % tq
                kpos = ki * tk + lax.broadcasted_iota(jnp.int32, s.shape, 1)
                s = jnp.where(kpos <= qpos, s, NEG_INF)
            m_old = m_scr[rows, :]
            m_new = jnp.maximum(m_old, jnp.max(s, axis=-1, keepdims=True))
            alpha = jnp.exp2(m_old - m_new)
            pt = [jnp.exp2(st - m_new) for st in _lane_tiles(s)]
            p = jnp.concatenate(pt, axis=1).astype(BF16)
            pv = jnp.concatenate([_dot(p, ckv), functools.reduce(lambda x, y: x + y, pt)], axis=1)
            acc = acc_scr[rows, :]
            acc_scr[rows, :] = jnp.concatenate([at * alpha for at in _lane_tiles(acc)], axis=1) + pv
            m_scr[rows, :] = m_new

    @pl.when(ki < last)
    def _():
        step(False)

    visible = qi * tq + tq - ki * tk

    for piece in range(1, tk // ATT_TQ + 1):
        lo, hi = (piece - 1) * ATT_TQ, piece * ATT_TQ

        @pl.when((ki == last) & (visible > lo) & (visible <= hi))
        def _(hi=hi):
            step(True, hi)

    @pl.when(ki == last)
    def _():
        acc = acc_scr[...]
        l = jnp.sum(acc[:, KV_RANK:KV_RANK + LANES], axis=-1, keepdims=True)
        y_ref[...] = _project_out(acc[:, 0:KV_RANK] / l, wuv_ref, tq)


def _attn_prompt_tiles(cqn, of, kp, wf, hm, wuv, t_len, *, tq, q_tiles, y_prev=None):
    nb = kp.shape[0]
    tk = ATT_TK
    pairs = [(i, j) for i in q_tiles for j in range((i * tq + tq - 1) // tk + 1)]
    q_tab = jnp.asarray([p[0] for p in pairs], jnp.int32)
    k_tab = jnp.asarray([p[1] for p in pairs], jnp.int32)
    rows = H_C * tq
    const = lambda w: pl.BlockSpec(w.shape, lambda b, t, qt, kt: (0,) * w.ndim)
    in_specs = [pl.BlockSpec((None, tq, Q_RANK), lambda b, t, qt, kt: (b, qt[t], 0)),
                pl.BlockSpec((None, tq, 256), lambda b, t, qt, kt: (b, qt[t], 0)),
                pl.BlockSpec((None, tk, 512), lambda b, t, qt, kt: (b, kt[t], 0)),
                const(wf), const(hm), const(wuv)]
    args = [q_tab, k_tab, cqn, of, kp, wf, hm, wuv]
    aliases = {}
    if y_prev is not None:
        in_specs.append(pl.BlockSpec(memory_space=pl.ANY))
        args.append(y_prev)
        aliases = {len(args) - 1: 0}
    grid_spec = pltpu.PrefetchScalarGridSpec(
        num_scalar_prefetch=2,
        grid=(nb, len(pairs)),
        in_specs=in_specs,
        out_specs=pl.BlockSpec((None, tq, W_C), lambda b, t, qt, kt: (b, qt[t], 0)),
        scratch_shapes=[pltpu.VMEM((rows, KV_RANK), BF16), pltpu.VMEM((rows, 256), BF16),
                        pltpu.VMEM((rows, LANES), F32), pltpu.VMEM((rows, KV_RANK + LANES), F32)],
    )
    return pl.pallas_call(
        functools.partial(_attn_prompt_kernel, has_prev=y_prev is not None),
        grid_spec=grid_spec,
        out_shape=jax.ShapeDtypeStruct((nb, t_len, W_C), F32),
        input_output_aliases=aliases,
        compiler_params=_cparams("parallel", "arbitrary"), name="attn_prompt",
    )(*args)


def _attn_prompt(cqn, of, kp, wf, hm, wuv, t_len):
    n_full = t_len // ATT_TQ
    rem = t_len - n_full * ATT_TQ
    y = jnp.zeros((kp.shape[0], t_len, W_C), F32)
    y = _attn_prompt_tiles(cqn, of, kp, wf, hm, wuv, t_len, tq=ATT_TQ, q_tiles=range(n_full), y_prev=y)
    if rem:
        assert rem % 16 == 0 and (n_full * ATT_TQ) % rem == 0
        y = _attn_prompt_tiles(cqn, of, kp, wf, hm, wuv, t_len, tq=rem, q_tiles=[n_full * ATT_TQ // rem],
                               y_prev=y)
    return y


def _mla_branch_prompt(z3, mla_w, wf, wuv, hm):
    nb, t_len, _ = z3.shape
    tp = -(-t_len // ATT_TK) * ATT_TK
    cos, sin = _rope_tables(jnp.arange(tp))
    ckv, kr, kp, cqn, of = _mla_prep(z3, cos, sin, mla_w, tm=ATT_TK, tp=tp)
    yc = _attn_prompt(cqn, of, kp, wf, hm, wuv, t_len)
    return yc, ckv, kr


def _page_copies(pt_ref, ckv_hbm, kr_hbm, kbuf, krbuf, sem, seq, slot, pages=None, *, layer, n_pages):
    out = []
    for pg in (range(n_pages) if pages is None else pages):
        pid = pt_ref[seq * n_pages + pg]
        out.append(pltpu.make_async_copy(ckv_hbm.at[layer, pid], kbuf.at[slot, pg], sem.at[slot, 0]))
        out.append(pltpu.make_async_copy(kr_hbm.at[layer, pid], krbuf.at[slot, pg], sem.at[slot, 1]))
    return out


def _attn_sample_kernel(pt_ref, ckv_hbm, kr_hbm, cqn_ref, of_ref, cnew_ref, knew_ref, wf_ref, hm_ref,
                        tile_ref, wuv_ref, y_ref, kbuf, krbuf, k_scr, kr_scr, sem, *, layer, n_pages):
    tq = cqn_ref.shape[1]
    page = kbuf.shape[2]
    span = SAMPLE_PAGES_PER_CHUNK * page
    n_chunks = n_pages // SAMPLE_PAGES_PER_CHUNK
    step = pl.program_id(0)
    last_step = pl.num_programs(0) - 1
    copies = functools.partial(_page_copies, pt_ref, ckv_hbm, kr_hbm, kbuf, krbuf, sem,
                               layer=layer, n_pages=n_pages)

    def attend(j, slot, seq_after):
        lat, rope = _build_q(cqn_ref[j], of_ref[j], wf_ref[...], hm_ref[...], tq)
        qr = _dot(rope, tile_ref[...]).astype(BF16)
        cnew = cnew_ref[j].astype(BF16)
        s_new = _dot_nt(lat, cnew) + _dot_nt(qr, knew_ref[j].astype(BF16))
        tpos = lax.broadcasted_iota(jnp.int32, s_new.shape, 0) % tq
        s_new = jnp.where(lax.broadcasted_iota(jnp.int32, s_new.shape, 1) <= tpos, s_new, NEG_INF)
        m = jnp.max(s_new, axis=-1, keepdims=True)
        p_new = jnp.exp2(s_new - m)
        l = jnp.sum(p_new, axis=-1, keepdims=True)
        acc = _dot(p_new.astype(BF16), cnew)

        def scores(c):
            pages = range(c * SAMPLE_PAGES_PER_CHUNK, (c + 1) * SAMPLE_PAGES_PER_CHUNK)
            for i in pages:
                k_scr[i * page:(i + 1) * page, :] = kbuf[slot, i].astype(BF16)
                kr_scr[:, i * page:(i + 1) * page] = krbuf[slot, i].astype(BF16)
            for cp in copies(seq_after, slot, pages):
                cp.start()
            cols = slice(c * span, (c + 1) * span)
            return _dot_nt(lat, k_scr[cols, :]) + _dot(qr, kr_scr[:, cols])

        ahead = [scores(c) for c in range(SAMPLE_SKEW)]
        for c in range(n_chunks):
            s = ahead.pop(0)
            if c + SAMPLE_SKEW < n_chunks:
                ahead.append(scores(c + SAMPLE_SKEW))
            m_new = jnp.maximum(m, jnp.max(s, axis=-1, keepdims=True))
            alpha = jnp.exp2(m - m_new)
            p = jnp.exp2(s - m_new)
            l = alpha * l + jnp.sum(p, axis=-1, keepdims=True)
            acc = alpha * acc + _dot(p.astype(BF16), k_scr[c * span:(c + 1) * span, :])
            m = m_new
        y_ref[j] = _project_out(acc / l, wuv_ref, tq)

    seq0 = 2 * step

    @pl.when(step == 0)
    def _():
        for slot in range(2):
            for cp in copies(seq0 + slot, slot):
                cp.start()

    after = [jnp.minimum(seq0 + 2 + slot, 2 * last_step + slot) for slot in range(2)]
    for slot in range(2):
        for cp in copies(seq0 + slot, slot):
            cp.wait()
        attend(slot, slot, after[slot])

    @pl.when(step == last_step)
    def _():
        for slot in range(2):
            for cp in copies(after[slot], slot):
                cp.wait()


def _attn_sample(page_table, cache_ckv, cache_krope_t, layer, cqn, of, cnew, knew, wf, hm, tile_t, wuv):
    nb, tq, _ = cqn.shape
    n_pages = page_table.shape[1]
    page = cache_ckv.shape[2]
    assert nb % 2 == 0
    per_step = lambda width: pl.BlockSpec((2, tq, width), lambda i, pt: (i, 0, 0))
    const = lambda w: pl.BlockSpec(w.shape, lambda i, pt: (0,) * w.ndim)
    hbm = pl.BlockSpec(memory_space=pl.ANY)
    grid_spec = pltpu.PrefetchScalarGridSpec(
        num_scalar_prefetch=1,
        grid=(nb // 2,),
        in_specs=[hbm, hbm, per_step(Q_RANK), per_step(256), per_step(KV_RANK), per_step(ROPE_DIM),
                  const(wf), const(hm), const(tile_t), const(wuv)],
        out_specs=per_step(W_C),
        scratch_shapes=[pltpu.VMEM((2, n_pages, page, KV_RANK), F32),
                        pltpu.VMEM((2, n_pages, ROPE_DIM, page), F32),
                        pltpu.VMEM((n_pages * page, KV_RANK), BF16),
                        pltpu.VMEM((ROPE_DIM, n_pages * page), BF16),
                        pltpu.SemaphoreType.DMA((2, 2))],
    )
    return pl.pallas_call(
        functools.partial(_attn_sample_kernel, layer=layer, n_pages=n_pages),
        grid_spec=grid_spec,
        out_shape=jax.ShapeDtypeStruct((nb, tq, W_C), F32),
        compiler_params=_cparams("arbitrary"), name="attn_sample",
    )(page_table.reshape(-1), cache_ckv, cache_krope_t, cqn, of, cnew, knew, wf, hm, tile_t, wuv)


def _mla_branch_sample(z3, page_table, cache_ckv, cache_krope_t, layer, n_past, mla_w, wf, wuv, hm, tile_t):
    nb, ts, _ = z3.shape
    cos, sin = _rope_tables(n_past + jnp.arange(nb * ts) % ts)
    ckv, kr, _, cqn, of = _mla_prep(z3.reshape(1, nb * ts, Z_COLS), cos, sin, mla_w, tm=ATT_TK, tp=nb * ts)
    ckv = ckv.reshape(nb, ts, KV_RANK)
    kr = kr.reshape(nb, ts, ROPE_DIM)
    yc = _attn_sample(page_table, cache_ckv, cache_krope_t, layer, cqn.reshape(nb, ts, Q_RANK),
                      of.reshape(nb, ts, 256), ckv, kr, wf, hm, tile_t, wuv)
    return yc, ckv, kr


def _out_kernel(ya_ref, yb_ref, yc_ref, gc_ref, x_ref, wa_ref, wb_ref, wc_ref, g_ref, b_ref, o_ref):
    yc = (yc_ref[...] * _silu(gc_ref[...])).astype(BF16)
    out = _dot(ya_ref[...], wa_ref[...]) + _dot(yb_ref[...], wb_ref[...]) + _dot(yc, wc_ref[...])
    h = DN_ALPHA * x_ref[...] + out
    mu = jnp.mean(h, axis=-1, keepdims=True)
    d = h - mu
    var = jnp.mean(d * d, axis=-1, keepdims=True)
    o_ref[...] = d * lax.rsqrt(var + 1e-5) * g_ref[...] + b_ref[...]


def _out_proj(ya, yb, yc, z, x, ow, tm):
    n = x.shape[0]
    wspecs = [pl.BlockSpec(w.shape, lambda i: (0,) * w.ndim) for w in ow]
    rowblk = lambda width, col=0: pl.BlockSpec((tm, width), lambda i: (i, col))
    return pl.pallas_call(
        _out_kernel,
        grid=(n // tm,),
        in_specs=[rowblk(W_A), rowblk(W_B), rowblk(W_C), rowblk(W_C, 3), rowblk(D_MODEL)] + wspecs,
        out_specs=rowblk(D_MODEL),
        out_shape=jax.ShapeDtypeStruct((n, D_MODEL), F32),
        compiler_params=_cparams("parallel"), name="out_proj",
    )(ya, yb, yc, z, x, *ow)


def _out_proj_tail(ya, yb, yc, z3, x, ow, *, skip, tm):
    nb, t_len, _ = z3.shape
    keep = t_len - skip
    seq3 = lambda a: a.reshape(nb, t_len, a.shape[-1])
    wspecs = [pl.BlockSpec(w.shape, lambda b, j: (0,) * w.ndim) for w in ow]
    rows = lambda width, col=0: pl.BlockSpec((pl.Element(1), pl.Element(tm), pl.Element(width)),
                                             lambda b, j: (b, pl.multiple_of(skip + j * tm, skip), col * width))

    def tail_kernel(ya_ref, yb_ref, yc_ref, gc_ref, x_ref, *rest):
        _out_kernel(ya_ref.at[0], yb_ref.at[0], yc_ref.at[0], gc_ref.at[0], x_ref.at[0], *rest)

    return pl.pallas_call(
        tail_kernel,
        grid=(nb, keep // tm),
        in_specs=[rows(W_A), rows(W_B), rows(W_C), rows(W_C, 3), rows(D_MODEL)] + wspecs,
        out_specs=pl.BlockSpec((None, tm, D_MODEL), lambda b, j: (b, j, 0)),
        out_shape=jax.ShapeDtypeStruct((nb, keep, D_MODEL), F32),
        compiler_params=_cparams("parallel", "parallel"), name="out_proj_tail",
    )(seq3(ya), seq3(yb), seq3(yc), z3, seq3(x), *ow)


def _block_diag(blocks):
    nblk, bw, _ = blocks.shape
    eye = jnp.eye(nblk, dtype=blocks.dtype)
    return jnp.einsum('hij,hg->higj', blocks, eye).reshape(nblk * bw, nblk * bw)


def _pad_to(x, shape):
    return jnp.pad(x, [(0, s - d) for d, s in zip(x.shape, shape)])


def _rope_tables(pos):
    inv = ROPE_THETA ** (-2.0 * jnp.arange(ROPE_HALF, dtype=F32) / ROPE_DIM)
    ang = pos.astype(F32)[:, None] * inv[None, :]
    return jnp.tile(jnp.cos(ang), (1, H_C)), jnp.tile(jnp.sin(ang), (1, H_C))


def _static_tables():
    lane = np.arange(256)
    hm = np.stack([((lane % LANES) // ROPE_HALF == h) for h in range(H_C)]).astype(np.float32)
    hmb = np.stack([(lane // HD_B == h) for h in range(H_B)]).astype(np.float32)
    ones_bd = (lane[:, None] // HD_B == lane[None, :] // HD_B).astype(np.float32)
    tile_t = np.zeros((256, ROPE_DIM), np.float32)
    for ln in lane:
        tile_t[ln, (ln // LANES) * ROPE_HALF + ln % ROPE_HALF] = 1.0
    return jnp.asarray(hm), jnp.asarray(hmb), jnp.asarray(ones_bd, BF16), jnp.asarray(tile_t, BF16)


_PER_LAYER = ('w_in', 'conv_w', 'conv_b', 'lru_wr', 'lru_br', 'lru_wi', 'lru_bi', 'lru_lambda', 'rk_mix',
              'rk_wr', 'rk_wk', 'rk_wv', 'rk_w0', 'rk_w1', 'rk_w2', 'rk_a0', 'rk_a1', 'rk_a2', 'rk_kk',
              'rk_ka', 'rk_rk', 'rk_gn_g', 'rk_gn_b', 'q_norm', 'kv_norm', 'w_uq', 'w_uk', 'w_uv', 'w_out',
              'ln_g', 'ln_b')


def kernel(x_prompt, x_sample, cache_ckv, cache_krope, state_conv, state_lru, state_shift, state_wkv,
           page_table, meta_tokens, w_in, conv_w, conv_b, lru_wr, lru_br, lru_wi, lru_bi, lru_lambda,
           rk_mix, rk_wr, rk_wk, rk_wv, rk_w0, rk_w1, rk_w2, rk_a0, rk_a1, rk_a2, rk_kk, rk_ka, rk_rk,
           rk_gn_g, rk_gn_b, q_norm, kv_norm, w_uq, w_uk, w_uv, w_out, ln_g, ln_b):
    stacked = dict(zip(_PER_LAYER, (w_in, conv_w, conv_b, lru_wr, lru_br, lru_wi, lru_bi, lru_lambda, rk_mix,
                                    rk_wr, rk_wk, rk_wv, rk_w0, rk_w1, rk_w2, rk_a0, rk_a1, rk_a2, rk_kk,
                                    rk_ka, rk_rk, rk_gn_g, rk_gn_b, q_norm, kv_norm, w_uq, w_uk, w_uv, w_out,
                                    ln_g, ln_b)))
    bp, seq, _ = x_prompt.shape
    bs, ts, _ = x_sample.shape
    tp_len = N_META + seq
    n_past = page_table.shape[1] * cache_ckv.shape[2]
    depth = w_in.shape[0]
    assert ts == 8 and CHUNK % ts == 0 and bs % (CHUNK // ts) == 0
    assert tp_len % 24 == 0 and seq % TAIL_TILE == 0 and N_META % 8 == 0

    hm, hmb, ones_bd, tile_t = _static_tables()
    row2 = lambda v: v.reshape(1, -1)
    cache_krope_t = jnp.swapaxes(cache_krope, 2, 3)

    xp = jnp.concatenate(
        [jnp.broadcast_to(meta_tokens[None].astype(x_prompt.dtype), (bp, N_META, D_MODEL)), x_prompt],
        axis=1).reshape(bp * tp_len, D_MODEL)
    xs = x_sample.reshape(bs * ts, D_MODEL)
    tm_p = tp_len // 3
    tm_s = bs * ts

    p_states, s_states = [], []
    for l in range(depth):
        p = {name: arr[l] for name, arr in stacked.items()}
        wi = p['w_in']
        kr_cols = wi[:, 1536:1568]
        w_in_r = jnp.concatenate(
            [wi[:, 0:1536], wi[:, 1568:2080],
             jnp.tile(kr_cols[:, 0:ROPE_HALF], (1, H_C)), jnp.tile(kr_cols[:, ROPE_HALF:], (1, H_C))],
            axis=1).astype(BF16)
        lru_w = _rglru_weights(p)
        prep_w, wkv_w = _rwkv_weights(p, hmb, ones_bd)
        mla_w, wf, wuv = _mla_weights(p)
        wo = p['w_out'].astype(BF16)
        out_w = (wo[0:W_A], wo[W_A:W_A + W_B], wo[W_A + W_B:], row2(p['ln_g']), row2(p['ln_b']))

        z = _in_proj(xp, w_in_r, tm_p)
        z3 = z.reshape(bp, tp_len, Z_COLS)
        ya, lru_new = _rglru_branch_prompt(z3, lru_w)
        yb, wkv_new = _rwkv_branch_prompt(z3, prep_w, wkv_w)
        yc, ckv, kr = _mla_branch_prompt(z3, mla_w, wf, wuv, hm)
        if l + 1 < depth:
            xp = _out_proj(ya, yb.reshape(bp * tp_len, W_B), yc.reshape(bp * tp_len, W_C), z, xp, out_w, tm_p)
        else:
            y_prompt = _out_proj_tail(ya, yb, yc, z3, xp, out_w, skip=N_META, tm=TAIL_TILE)
        p_states.append((ckv, kr, z3[:, tp_len - (CONV_W - 1):, 0:W_A], lru_new,
                         z3[:, tp_len - 1, 2 * W_A:2 * W_A + W_B], wkv_new))

        z = _in_proj(xs, w_in_r, tm_s)
        z3 = z.reshape(bs, ts, Z_COLS)
        ya, lru_new = _rglru_branch_sample(z3, state_conv[l], state_lru[l], n_past, lru_w)
        yb, wkv_new = _rwkv_branch_sample(z3, state_shift[l], state_wkv[l], prep_w, wkv_w)
        yc, ckv, kr = _mla_branch_sample(z3, page_table, cache_ckv, cache_krope_t, l, n_past,
                                         mla_w, wf, wuv, hm, tile_t)
        x_new = _out_proj(ya, yb.reshape(bs * ts, W_B), yc.reshape(bs * ts, W_C), z, xs, out_w, tm_s)
        s_states.append((ckv, kr, z3[:, ts - (CONV_W - 1):, 0:W_A], lru_new,
                         z3[:, ts - 1, 2 * W_A:2 * W_A + W_B], wkv_new))
        xs = x_new

    y_sample = xs.reshape(bs, ts, D_MODEL)
    p_out = [jnp.stack([st[i] for st in p_states]) for i in range(6)]
    s_out = [jnp.stack([st[i] for st in s_states]) for i in range(6)]
    return (y_prompt, y_sample, *p_out, *s_out)
```

```python
import functools
import itertools

import numpy as np
import jax
import jax.numpy as jnp
from jax import lax
from jax.experimental import pallas as pl
from jax.experimental.pallas import tpu as pltpu

F32 = jnp.float32
BF16 = jnp.bfloat16
HIGHEST = lax.Precision.HIGHEST

D_MODEL = 1024
N_META = 16
W_A = 256
W_B = 256
W_C = 512
CONV_W = 4
LRU_C = 8.0
HD_B = 64
H_B = 4
GN_EPS = 64e-5
H_C = 8
V_DIM = 64
NOPE_DIM = 64
ROPE_DIM = 32
ROPE_HALF = 16
Q_RANK = 256
KV_RANK = 256
ROPE_THETA = 10000.0
ATTN_SCALE = (NOPE_DIM + ROPE_DIM) ** -0.5
LOG2_E = 1.4426950408889634
Q_SCALE = ATTN_SCALE * LOG2_E
NEG_INF = -1e30
DEPTH = 2
DN_ALPHA = (2 * DEPTH) ** 0.25

LANES = 128
Z_COLS = 2304
CHUNK = 64
WKV_BATCH = 8
WKV_INTERLEAVE = 4
TAIL_TILE = 1024
ATT_TQ = 256
ATT_TK = 1024
ATT_SKEW = 3
SAMPLE_PAGES_PER_CHUNK = 8
SAMPLE_SKEW = 3

VMEM_LIMIT = 48 * 1024 * 1024


def _cparams(*sem):
    return pltpu.CompilerParams(dimension_semantics=sem, vmem_limit_bytes=VMEM_LIMIT)


def _dot(a, b):
    return jnp.dot(a, b, preferred_element_type=F32)


def _dot_nt(a, b):
    return lax.dot_general(a, b, (((1,), (1,)), ((), ())), preferred_element_type=F32)


def _dot_tn(a, b):
    return lax.dot_general(a, b, (((0,), (0,)), ((), ())), preferred_element_type=F32)


def _dot_hi(a, b):
    return jnp.dot(a, b, preferred_element_type=F32, precision=HIGHEST)


def _split(x):
    hi = x.astype(BF16)
    return hi, (x - hi.astype(F32)).astype(BF16)


def _split3(x):
    hi, mid = _split(x)
    return hi, mid, (x - hi.astype(F32) - mid.astype(F32)).astype(BF16)


def _dot_sel(sel, b):
    b_hi, b_mid, b_lo = _split3(b)
    return _dot(sel, b_hi) + (_dot(sel, b_mid) + _dot(sel, b_lo))


def _dot_sel_r(a, sel):
    a_hi, a_mid, a_lo = _split3(a)
    return _dot(a_hi, sel) + (_dot(a_mid, sel) + _dot(a_lo, sel))


def _softplus(x):
    return jnp.maximum(x, 0.0) + jnp.log1p(jnp.exp(-jnp.abs(x)))


def _sigmoid(x):
    return 0.5 * jnp.tanh(0.5 * x) + 0.5


def _silu(x):
    return x * _sigmoid(x)


def _lane_tiles(x):
    return [x[:, j * LANES:(j + 1) * LANES] for j in range(x.shape[1] // LANES)]


def _mm_kernel(x_ref, w_ref, o_ref):
    o_ref[...] = _dot(x_ref[...].astype(BF16), w_ref[...])


def _in_proj(x, w, tm):
    n, k = x.shape
    m = w.shape[1]
    return pl.pallas_call(
        _mm_kernel,
        grid=(n // tm,),
        in_specs=[pl.BlockSpec((tm, k), lambda i: (i, 0)), pl.BlockSpec((k, m), lambda i: (0, 0))],
        out_specs=pl.BlockSpec((tm, m), lambda i: (i, 0)),
        out_shape=jax.ShapeDtypeStruct((n, m), F32),
        compiler_params=_cparams("parallel"),
        name="in_proj",
    )(x, w)


def _rglru_kernel(*refs, seq_len, pos0, has_state):
    if has_state:
        (u_ref, g_ref, f_ref, hp_ref, cw_ref, cb_ref, wr_ref, br_ref, wi_ref, bi_ref, lam_ref,
         y_ref, h_ref) = refs
    else:
        (u_ref, g_ref, cw_ref, cb_ref, wr_ref, br_ref, wi_ref, bi_ref, lam_ref,
         y_ref, h_ref, a_scr, b_scr) = refs
    u = u_ref[...]
    rows = u.shape[0]
    row = lax.broadcasted_iota(jnp.int32, u.shape, 0)
    t = row if rows == seq_len else row & (seq_len - 1)

    cw = cw_ref[...]
    xc = cb_ref[...] + u * cw[CONV_W - 1:CONV_W]
    for k in range(1, CONV_W):
        sh = pltpu.roll(u, k, 0)
        if has_state:
            f = f_ref[...]
            prev = f if k == CONV_W - 1 else pltpu.roll(f, rows - (CONV_W - 1 - k), 0)
        else:
            prev = 0.0
        sh = jnp.where(t < k, prev, sh)
        xc = xc + sh * cw[CONV_W - 1 - k:CONV_W - k]

    xcb = xc.astype(BF16)
    gate_r = _sigmoid(_dot(xcb, wr_ref[...]) + br_ref[...])
    gate_i = _sigmoid(_dot(xcb, wi_ref[...]) + bi_ref[...])
    log_a = (-LRU_C) * gate_r * _softplus(-lam_ref[...])
    th = jnp.tanh(log_a)
    neg_expm1 = -2.0 * th / (1.0 - th)
    reset = (t + pos0) == 0
    a = jnp.where(reset, 0.0, jnp.exp(log_a))
    mult = jnp.where(reset, 1.0, jnp.sqrt(neg_expm1))
    bv = xc * gate_i * mult

    t8 = row & 7
    for s in (1, 2, 4):
        a_s = jnp.where(t8 >= s, pltpu.roll(a, s, 0), 1.0)
        b_s = jnp.where(t8 >= s, pltpu.roll(bv, s, 0), 0.0)
        bv = a * b_s + bv
        a = a * a_s

    if has_state:
        h = bv + a * hp_ref[...]
        h_ref[...] = h
    else:
        a_scr[...] = a
        b_scr[...] = bv

        def body(i, carry):
            r0 = pl.multiple_of(i * 8, 8)
            hh = b_scr[pl.ds(r0, 8), :] + a_scr[pl.ds(r0, 8), :] * carry
            b_scr[pl.ds(r0, 8), :] = hh
            return jnp.broadcast_to(hh[7:8, :], hh.shape)

        carry = lax.fori_loop(0, rows // 8, body, jnp.zeros((8, u.shape[1]), F32))
        h = b_scr[...]
        h_ref[...] = carry[0:1, :]
    y_ref[...] = (h * _silu(g_ref[...])).astype(BF16)


def _rglru(z, lw, *, nseq, seq_len, pos0, conv_ext=None, h_ext=None):
    n = z.shape[0]
    wspecs = [pl.BlockSpec(w.shape, lambda i: (0,) * w.ndim) for w in lw]
    if conv_ext is None:
        rows = seq_len
        in_specs = [pl.BlockSpec((rows, W_A), lambda i: (i, 0)),
                    pl.BlockSpec((rows, W_A), lambda i: (i, 1))] + wspecs
        out_specs = [pl.BlockSpec((rows, W_A), lambda i: (i, 0)),
                     pl.BlockSpec((None, 1, W_A), lambda i: (i, 0, 0))]
        out_shape = [jax.ShapeDtypeStruct((n, W_A), BF16), jax.ShapeDtypeStruct((nseq, 1, W_A), F32)]
        scratch = [pltpu.VMEM((rows, W_A), F32), pltpu.VMEM((rows, W_A), F32)]
        args = (z, z) + tuple(lw)
        grid = (nseq,)
    else:
        rows = n
        full = pl.BlockSpec((rows, W_A), lambda i: (0, 0))
        in_specs = [full, pl.BlockSpec((rows, W_A), lambda i: (0, 1)), full, full] + wspecs
        out_specs = [full, full]
        out_shape = [jax.ShapeDtypeStruct((n, W_A), BF16), jax.ShapeDtypeStruct((n, W_A), F32)]
        scratch = []
        args = (z, z, conv_ext, h_ext) + tuple(lw)
        grid = (1,)
    return pl.pallas_call(
        functools.partial(_rglru_kernel, seq_len=seq_len, pos0=pos0, has_state=conv_ext is not None),
        grid=grid, in_specs=in_specs, out_specs=out_specs, out_shape=out_shape,
        scratch_shapes=scratch, compiler_params=_cparams("parallel"), name="rglru",
    )(*args)


def _rglru_weights(p):
    row2 = lambda v: v.reshape(1, -1)
    return (p['conv_w'], row2(p['conv_b']), _block_diag(p['lru_wr']).astype(BF16), row2(p['lru_br']),
            _block_diag(p['lru_wi']).astype(BF16), row2(p['lru_bi']), row2(p['lru_lambda']))


def _rglru_branch_prompt(z3, lru_w):
    nb, t_len, _ = z3.shape
    y, h_last = _rglru(z3.reshape(nb * t_len, Z_COLS), lru_w, nseq=nb, seq_len=t_len, pos0=0)
    return y, h_last.reshape(nb, W_A)


def _rglru_branch_sample(z3, conv_prev, h_prev, pos0, lru_w):
    nb, ts, _ = z3.shape
    conv_ext = _pad_to(conv_prev, (nb, ts, W_A)).reshape(nb * ts, W_A)
    h_ext = jnp.repeat(h_prev, ts, axis=0)
    y, h_all = _rglru(z3.reshape(nb * ts, Z_COLS), lru_w, nseq=nb, seq_len=ts, pos0=pos0,
                      conv_ext=conv_ext, h_ext=h_ext)
    return y, h_all.reshape(nb, ts, W_A)[:, ts - 1]


def _rwkv_prep_kernel(u_ref, p_ref, mix_ref, wr_ref, wk_ref, wv_ref, w0_ref, w1_ref, w2_ref,
                      a0_ref, a1_ref, a2_ref, kkw_ref, ka_ref, ones_ref,
                      r_ref, lw_ref, k_ref, v_ref, kk_ref, a_ref, *, seq_len, tiles_per_seq):
    u = u_ref[...]
    row = lax.broadcasted_iota(jnp.int32, u.shape, 0)
    rolled = pltpu.roll(u, 1, 0)
    if tiles_per_seq is None:
        u_prev = jnp.where((row & (seq_len - 1)) == 0, p_ref[...], rolled)
    else:
        first = (pl.program_id(0) % tiles_per_seq) == 0
        prow = jnp.where(first, 0.0, p_ref[7:8, :])
        u_prev = jnp.where(row == 0, prow, rolled)
    du = u_prev - u
    mix = mix_ref[...]
    xr = (u + du * mix[0:1]).astype(BF16)
    xw = (u + du * mix[1:2]).astype(BF16)
    xk = (u + du * mix[2:3]).astype(BF16)
    xv = (u + du * mix[3:4]).astype(BF16)
    xa = (u + du * mix[4:5]).astype(BF16)
    r = _dot(xr, wr_ref[...])
    k = _dot(xk, wk_ref[...])
    v = _dot(xv, wv_ref[...])
    lora_w = _dot(jnp.tanh(_dot(xw, w1_ref[...])).astype(BF16), w2_ref[...])
    w_log = -_softplus(-(w0_ref[...] + lora_w)) - 0.5
    a = _sigmoid(a0_ref[...] + _dot(_dot(xa, a1_ref[...]).astype(BF16), a2_ref[...]))
    kk = k * kkw_ref[...]
    ssq = _dot_sel_r(kk * kk, ones_ref[...])
    kk = kk / jnp.maximum(jnp.sqrt(ssq), 1e-12)
    r_ref[...] = r.astype(BF16)
    lw_ref[...] = -jnp.exp(w_log)
    k_ref[...] = (k * (1.0 + (a - 1.0) * ka_ref[...])).astype(BF16)
    v_ref[...] = v.astype(BF16)
    kk_ref[...] = kk.astype(BF16)
    a_ref[...] = a.astype(BF16)


def _rwkv_prep(z, pw, *, seq_len, tm, prev_ext=None):
    n = z.shape[0]
    wspecs = [pl.BlockSpec(w.shape, lambda i: (0,) * w.ndim) for w in pw]
    if prev_ext is None:
        tiles = seq_len // tm
        p_arg = z
        p_spec = pl.BlockSpec((8, W_B), lambda i: (jnp.maximum(i * (tm // 8) - 1, 0), 2))
    else:
        tiles = None
        p_arg = prev_ext
        p_spec = pl.BlockSpec((tm, W_B), lambda i: (i, 0))
    out = [jax.ShapeDtypeStruct((n, W_B), F32 if i == 1 else BF16) for i in range(6)]
    ospec = pl.BlockSpec((tm, W_B), lambda i: (i, 0))
    return pl.pallas_call(
        functools.partial(_rwkv_prep_kernel, seq_len=seq_len, tiles_per_seq=tiles),
        grid=(n // tm,),
        in_specs=[pl.BlockSpec((tm, W_B), lambda i: (i, 2)), p_spec] + wspecs,
        out_specs=[ospec] * 6, out_shape=out,
        compiler_params=_cparams("parallel"), name="rwkv_prep",
    )(z, p_arg, *pw)


def _wkv_kernel(r_ref, lw_ref, k_ref, v_ref, kk_ref, a_ref, g_ref, s0_ref, hm_ref, ones_ref,
                rk_ref, gng_ref, gnb_ref, y_ref, s_ref, *, t_valid, nbatch, nseq, sub_len):
    c = pl.program_id(1)

    @pl.when(c == 0)
    def _():
        s_ref[...] = s0_ref[...]

    for first in range(0, nbatch, WKV_INTERLEAVE):
        chains = [_wkv_chunk(c, j, r_ref, lw_ref, k_ref, v_ref, kk_ref, a_ref, g_ref, hm_ref, ones_ref,
                             rk_ref, gng_ref, gnb_ref, y_ref, s_ref, t_valid=t_valid, nseq=nseq, sub_len=sub_len)
                  for j in range(first, min(first + WKV_INTERLEAVE, nbatch))]
        for _ in itertools.zip_longest(*chains):
            pass


def _wkv_chunk(c, j, r_ref, lw_ref, k_ref, v_ref, kk_ref, a_ref, g_ref, hm_ref, ones_ref,
               rk_ref, gng_ref, gnb_ref, y_ref, s_ref, *, t_valid, nseq, sub_len):
    width = W_B
    nh = H_B
    s_base = j * nseq
    row = lax.broadcasted_iota(jnp.int32, (CHUNK, width), 0)
    valid = (c * CHUNK + row) < t_valid
    zero = jnp.zeros((CHUNK, width), F32)
    r = jnp.where(valid, r_ref[j].astype(F32), zero)
    lw = jnp.where(valid, lw_ref[j], zero)
    k = jnp.where(valid, k_ref[j].astype(F32), zero)
    v = jnp.where(valid, v_ref[j].astype(F32), zero)
    kk = jnp.where(valid, kk_ref[j].astype(F32), zero)
    a = jnp.where(valid, a_ref[j].astype(F32), zero)

    qi = lax.broadcasted_iota(jnp.int32, (CHUNK, CHUNK), 0)
    si = lax.broadcasted_iota(jnp.int32, (CHUNK, CHUNK), 1)
    same = (qi // sub_len) == (si // sub_len)
    cum = _dot_sel(jnp.where(same & (qi >= si), 1.0, 0.0).astype(BF16), lw)
    tot = _dot_sel(jnp.where(same, 1.0, 0.0).astype(BF16), lw)
    yield
    e_neg = jnp.exp(-cum)
    e_tail = jnp.exp(tot - cum)
    rg = r * jnp.exp(cum)
    ag = -kk * jnp.exp(cum - lw)
    kka = kk * a
    bg = kka * e_neg
    kg = k * e_neg
    bt = kka * e_tail
    kt = k * e_tail

    hm = hm_ref[...]

    def stack(x):
        return jnp.concatenate([x * hm[h:h + 1] for h in range(nh)], axis=0)

    def unstack(x):
        out = x[0:CHUNK]
        for h in range(1, nh):
            out = out + x[h * CHUNK:(h + 1) * CHUNK]
        return out

    a_st = stack(ag).astype(BF16)
    r_st = stack(rg)
    b_st = stack(bg).astype(BF16)
    k_st = stack(kg).astype(BF16)
    v_st = stack(v)

    n_st = nh * CHUNK
    rq = lax.broadcasted_iota(jnp.int32, (n_st, n_st), 0) % CHUNK
    cs = lax.broadcasted_iota(jnp.int32, (n_st, n_st), 1) % CHUNK
    same_st = (rq // sub_len) == (cs // sub_len)
    strict = same_st & (rq > cs)
    incl = same_st & (rq >= cs)

    gram = _dot_nt(jnp.concatenate([a_st, r_st.astype(BF16)], axis=0), jnp.concatenate([b_st, k_st], axis=0))
    l_ab = jnp.where(strict, gram[0:n_st, 0:n_st], 0.0)
    l_ak = jnp.where(strict, gram[0:n_st, n_st:], 0.0)
    p_rb = jnp.where(incl, gram[n_st:, 0:n_st], 0.0)
    p_rk = jnp.where(incl, gram[n_st:, n_st:], 0.0)
    yield

    eye = jnp.where(lax.broadcasted_iota(jnp.int32, (n_st, n_st), 0)
                    == lax.broadcasted_iota(jnp.int32, (n_st, n_st), 1), 1.0, 0.0)
    levels = int(np.log2(sub_len))
    tinv = eye + l_ab
    lpb = l_ab.astype(BF16)
    lp = _dot(lpb, lpb)
    yield
    for level in range(1, levels):
        lpb = lp.astype(BF16)
        if level + 1 < levels:
            both = _dot(lpb, jnp.concatenate([lpb, tinv.astype(BF16)], axis=1))
            lp = both[:, 0:n_st]
            tinv = tinv + both[:, n_st:]
        else:
            tinv = tinv + _dot(lpb, tinv.astype(BF16))
        yield

    x0_parts, r0_parts = [], []
    for s in range(nseq):
        st = s_ref[s_base + s].astype(BF16)
        sl = slice(s * sub_len, (s + 1) * sub_len)
        rows_s = jnp.concatenate([ag[sl], rg[sl]], axis=0).astype(BF16)
        xs = _dot_nt(rows_s, st)
        x0_parts.append(xs[0:sub_len])
        r0_parts.append(xs[sub_len:2 * sub_len])
    x0 = x0_parts[0] if nseq == 1 else jnp.concatenate(x0_parts, axis=0)
    r0 = r0_parts[0] if nseq == 1 else jnp.concatenate(r0_parts, axis=0)
    yield

    x_st = stack(x0) + _dot(l_ak.astype(BF16), v_st.astype(BF16))
    yield
    u_st = _dot(tinv.astype(BF16), x_st.astype(BF16))
    yield
    o_st = stack(r0) + _dot(jnp.concatenate([p_rb, p_rk], axis=1).astype(BF16),
                            jnp.concatenate([u_st, v_st], axis=0).astype(BF16))
    u_all = unstack(u_st)
    o = unstack(o_st)
    yield

    bd = (lax.broadcasted_iota(jnp.int32, (width, width), 0) // HD_B
          == lax.broadcasted_iota(jnp.int32, (width, width), 1) // HD_B)
    for s in range(nseq):
        sl = slice(s * sub_len, (s + 1) * sub_len)
        upd = (_dot_tn(u_all[sl].astype(BF16), bt[sl].astype(BF16))
               + _dot_tn(v[sl].astype(BF16), kt[sl].astype(BF16)))
        gam = jnp.exp(tot[s * sub_len:s * sub_len + 1])
        s_ref[s_base + s] = s_ref[s_base + s] * gam + jnp.where(bd, upd, 0.0)

    ones = ones_ref[...]
    mu = _dot_sel_r(o, ones) * (1.0 / HD_B)
    d = o - mu
    var = _dot_sel_r(d * d, ones) * (1.0 / HD_B)
    o = d * lax.rsqrt(var + GN_EPS) * gng_ref[...] + gnb_ref[...]
    o = o + _dot_sel_r(r * k * rk_ref[...], ones) * v
    y_ref[j] = (o * _silu(g_ref[j])).astype(BF16)


def _wkv(z3, prep, s0, cw, *, t_valid, nbatch, nseq, sub_len):
    nb, t_len, _ = z3.shape
    nc = -(-t_len // CHUNK)
    seq = pl.BlockSpec((nbatch, CHUNK, W_B), lambda b, c: (b, c, 0))
    wspecs = [pl.BlockSpec(w.shape, lambda b, c: (0,) * w.ndim) for w in cw]
    sspec = pl.BlockSpec((nbatch * nseq, W_B, W_B), lambda b, c: (b, 0, 0))
    return pl.pallas_call(
        functools.partial(_wkv_kernel, t_valid=t_valid, nbatch=nbatch, nseq=nseq, sub_len=sub_len),
        grid=(nb // nbatch, nc),
        in_specs=[seq] * 6 + [pl.BlockSpec((nbatch, CHUNK, W_B), lambda b, c: (b, c, 3)), sspec] + wspecs,
        out_specs=[seq, sspec],
        out_shape=[jax.ShapeDtypeStruct((nb, t_len, W_B), BF16),
                   jax.ShapeDtypeStruct(s0.shape, F32)],
        compiler_params=_cparams("parallel", "arbitrary"), name="wkv",
    )(*prep, z3, s0, *cw)


def _rwkv_weights(p, hmb, ones_bd):
    row2 = lambda v: v.reshape(1, -1)
    prep_w = (_pad_to(p['rk_mix'], (8, W_B)), p['rk_wr'].astype(BF16), p['rk_wk'].astype(BF16),
              p['rk_wv'].astype(BF16), row2(p['rk_w0']), _pad_to(p['rk_w1'], (W_B, LANES)).astype(BF16),
              _pad_to(p['rk_w2'], (LANES, W_B)).astype(BF16), row2(p['rk_a0']),
              _pad_to(p['rk_a1'], (W_B, LANES)).astype(BF16), _pad_to(p['rk_a2'], (LANES, W_B)).astype(BF16),
              row2(p['rk_kk']), row2(p['rk_ka']), ones_bd)
    wkv_w = (hmb, ones_bd, row2(p['rk_rk']), row2(p['rk_gn_g']), row2(p['rk_gn_b']))
    return prep_w, wkv_w


def _state_to_bd(s):
    rows = [jnp.pad(s[:, h], ((0, 0), (0, 0), (h * HD_B, W_B - (h + 1) * HD_B))) for h in range(H_B)]
    return jnp.concatenate(rows, axis=1)


def _state_from_bd(s):
    return jnp.stack([s[:, h * HD_B:(h + 1) * HD_B, h * HD_B:(h + 1) * HD_B] for h in range(H_B)], axis=1)


def _rwkv_branch_prompt(z3, prep_w, wkv_w):
    nb, t_len, _ = z3.shape
    prep = _rwkv_prep(z3.reshape(nb * t_len, Z_COLS), prep_w, seq_len=t_len, tm=t_len // 3)
    prep3 = tuple(a.reshape(nb, t_len, W_B) for a in prep)
    y, s_bd = _wkv(z3, prep3, jnp.zeros((nb, W_B, W_B), F32), wkv_w, t_valid=t_len,
                   nbatch=min(nb, WKV_BATCH), nseq=1, sub_len=CHUNK)
    return y, _state_from_bd(s_bd)


def _rwkv_branch_sample(z3, shift_prev, s_prev, prep_w, wkv_w):
    nb, ts, _ = z3.shape
    z = z3.reshape(nb * ts, Z_COLS)
    prep = _rwkv_prep(z, prep_w, seq_len=ts, tm=nb * ts, prev_ext=jnp.repeat(shift_prev, ts, axis=0))
    per_chunk = CHUNK // ts
    groups = nb // per_chunk
    prep3 = tuple(a.reshape(groups, CHUNK, W_B) for a in prep)
    y, s_bd = _wkv(z.reshape(groups, CHUNK, Z_COLS), prep3, _state_to_bd(s_prev), wkv_w,
                   t_valid=CHUNK, nbatch=1, nseq=per_chunk, sub_len=ts)
    return y.reshape(nb, ts, W_B), _state_from_bd(s_bd)


def _mla_prep_kernel(cq_ref, ckv_ref, za_ref, zb_ref, cos_ref, sin_ref, qn_ref, kvn_ref, wr1_ref, wr2_ref,
                     ckv_out, krt_out, kp_out, cqn_out, of_out, *, t_valid):
    tm = cq_ref.shape[0]
    row = lax.broadcasted_iota(jnp.int32, (tm, 1), 0)
    valid = (pl.program_id(1) * tm + row) < t_valid
    cq = cq_ref[...]
    cqn = cq * lax.rsqrt(jnp.mean(cq * cq, axis=-1, keepdims=True) + 1e-6) * qn_ref[...]
    cv = ckv_ref[...]
    ckv = cv * lax.rsqrt(jnp.mean(cv * cv, axis=-1, keepdims=True) + 1e-6) * kvn_ref[...]
    cos = cos_ref[...]
    sin = sin_ref[...]
    za = za_ref[...]
    zb = zb_ref[...]
    kr1 = za * cos - zb * sin
    kr2 = za * sin + zb * cos
    cqb = cqn.astype(BF16)
    q1 = _dot(cqb, wr1_ref[...])
    q2 = _dot(cqb, wr2_ref[...])
    o1 = q1 * cos - q2 * sin
    o2 = q1 * sin + q2 * cos
    ckv_out[...] = ckv
    krt_out[...] = jnp.concatenate([kr1[:, 0:ROPE_HALF], kr2[:, 0:ROPE_HALF]], axis=1)
    kp_out[:, 0:KV_RANK] = jnp.where(valid, ckv, 0.0).astype(BF16)
    kp_out[:, KV_RANK:KV_RANK + LANES] = jnp.where(valid, kr1, 0.0).astype(BF16)
    kp_out[:, KV_RANK + LANES:] = jnp.where(valid, kr2, 0.0).astype(BF16)
    cqn_out[...] = jnp.where(valid, cqn, 0.0).astype(BF16)
    of_out[:, 0:LANES] = jnp.where(valid, o1 * Q_SCALE, 0.0).astype(BF16)
    of_out[:, LANES:] = jnp.where(valid, o2 * Q_SCALE, 0.0).astype(BF16)


def _mla_prep(z3, cos, sin, mw, *, tm, tp):
    nb, t_len, _ = z3.shape
    nt = tp // tm
    last = -(-t_len // tm) - 1
    wspecs = [pl.BlockSpec(w.shape, lambda b, j: (0,) * w.ndim) for w in mw]
    real = lambda width, col: pl.BlockSpec((None, tm, width), lambda b, j: (b, jnp.minimum(j, last), col))
    blk = lambda width: pl.BlockSpec((None, tm, width), lambda b, j: (b, j, 0))
    tab = pl.BlockSpec((tm, LANES), lambda b, j: (jnp.minimum(j, last), 0))
    return pl.pallas_call(
        functools.partial(_mla_prep_kernel, t_valid=t_len),
        grid=(nb, nt),
        in_specs=[real(256, 4), real(256, 5), real(LANES, 16), real(LANES, 17), tab, tab] + wspecs,
        out_specs=[real(256, 0), real(ROPE_DIM, 0), blk(512), blk(256), blk(256)],
        out_shape=[jax.ShapeDtypeStruct((nb, t_len, KV_RANK), F32),
                   jax.ShapeDtypeStruct((nb, t_len, ROPE_DIM), F32),
                   jax.ShapeDtypeStruct((nb, tp, 512), BF16),
                   jax.ShapeDtypeStruct((nb, tp, Q_RANK), BF16),
                   jax.ShapeDtypeStruct((nb, tp, 256), BF16)],
        compiler_params=_cparams("parallel", "parallel"), name="mla_prep",
    )(z3, z3, z3, z3, cos, sin, *mw)


def _fold_kernel(wq_ref, wk_ref, o_ref):
    o_ref[...] = _dot_hi(wq_ref[...], wk_ref[...]).astype(BF16)


def _fold_q(wq_nope, wuk_t):
    return pl.pallas_call(
        _fold_kernel,
        grid=(H_C,),
        in_specs=[pl.BlockSpec((None, Q_RANK, NOPE_DIM), lambda h: (h, 0, 0)),
                  pl.BlockSpec((None, NOPE_DIM, KV_RANK), lambda h: (h, 0, 0))],
        out_specs=pl.BlockSpec((Q_RANK, KV_RANK), lambda h: (0, h)),
        out_shape=jax.ShapeDtypeStruct((Q_RANK, H_C * KV_RANK), BF16),
        compiler_params=_cparams("parallel"), name="fold_q",
    )(wq_nope, wuk_t)


def _mla_weights(p):
    row2 = lambda v: v.reshape(1, -1)
    wq = p['w_uq'].reshape(Q_RANK, H_C, NOPE_DIM + ROPE_DIM)
    wq_nope = jnp.transpose(wq[:, :, 0:NOPE_DIM], (1, 0, 2))
    wr1 = wq[:, :, NOPE_DIM:NOPE_DIM + ROPE_HALF].reshape(Q_RANK, LANES).astype(BF16)
    wr2 = wq[:, :, NOPE_DIM + ROPE_HALF:].reshape(Q_RANK, LANES).astype(BF16)
    wuk_t = jnp.transpose(p['w_uk'], (1, 2, 0))
    wf = _fold_q(wq_nope, wuk_t)
    mla_w = (row2(p['q_norm']), row2(p['kv_norm']), wr1, wr2)
    eye = jnp.eye(2, dtype=F32)[jnp.arange(H_C) % 2]
    wuv = jnp.einsum('rhv,hg->hrgv', p['w_uv'], eye).reshape(H_C, KV_RANK, 2 * V_DIM).astype(BF16)
    return mla_w, wf, wuv


def _build_q(cqn, of, wf, hm, tq):
    q_all = _dot(cqn, wf) * Q_SCALE
    lat = [q_all[:, h * KV_RANK:(h + 1) * KV_RANK].astype(BF16) for h in range(H_C)]
    rope = [of * hm[h:h + 1].astype(BF16) for h in range(H_C)]
    return jnp.concatenate(lat, axis=0), jnp.concatenate(rope, axis=0)


def _project_out(o, wuv_ref, tq):
    blocks = []
    for g in range(H_C // 2):
        h0, h1 = 2 * g, 2 * g + 1
        blocks.append(_dot(o[h0 * tq:(h0 + 1) * tq].astype(BF16), wuv_ref[h0])
                      + _dot(o[h1 * tq:(h1 + 1) * tq].astype(BF16), wuv_ref[h1]))
    return jnp.concatenate(blocks, axis=1)


def _attn_prompt_kernel(qt_ref, kt_ref, cqn_ref, of_ref, kp_ref, wf_ref, hm_ref, wuv_ref, *refs, has_prev):
    y_ref, ql_scr, qr_scr, m_scr, acc_scr = refs[1:] if has_prev else refs
    tq = cqn_ref.shape[0]
    tk = kp_ref.shape[0]
    t = pl.program_id(1)
    qi = qt_ref[t]
    ki = kt_ref[t]
    last = (qi * tq + tq - 1) // tk

    @pl.when(ki == 0)
    def _():
        lat, rope = _build_q(cqn_ref[...], of_ref[...], wf_ref[...], hm_ref[...], tq)
        ql_scr[...] = lat
        qr_scr[...] = rope
        m_scr[...] = jnp.full(m_scr.shape, NEG_INF, F32)
        acc_scr[...] = jnp.zeros(acc_scr.shape, F32)

    def step(masked, nk=tk):
        kp = kp_ref[0:nk, :]
        ckv = kp[:, 0:KV_RANK]
        krope = kp[:, KV_RANK:]
        gr = max(1, min(H_C, ATT_TQ // tq)) * tq
        n_groups = H_C * tq // gr

        def scores(g):
            rows = pl.ds(g * gr, gr)
            return _dot_nt(ql_scr[rows, :], ckv) + _dot_nt(qr_scr[rows, :], krope)

        skew = min(ATT_SKEW, n_groups)
        ahead = [scores(g) for g in range(skew)]
        for g in range(n_groups):
            rows = pl.ds(g * gr, gr)
            s = ahead.pop(0)
            if g + skew < n_groups:
                ahead.append(scores(g + skew))
            if masked:
                qpos = qi * tq + lax.broadcasted_iota(jnp.int32, s.shape, 0) % tq
                kpos = ki * tk + lax.broadcasted_iota(jnp.int32, s.shape, 1)
                s = jnp.where(kpos <= qpos, s, NEG_INF)
            m_old = m_scr[rows, :]
            m_new = jnp.maximum(m_old, jnp.max(s, axis=-1, keepdims=True))
            alpha = jnp.exp2(m_old - m_new)
            pt = [jnp.exp2(st - m_new) for st in _lane_tiles(s)]
            p = jnp.concatenate(pt, axis=1).astype(BF16)
            pv = jnp.concatenate([_dot(p, ckv), functools.reduce(lambda x, y: x + y, pt)], axis=1)
            acc = acc_scr[rows, :]
            acc_scr[rows, :] = jnp.concatenate([at * alpha for at in _lane_tiles(acc)], axis=1) + pv
            m_scr[rows, :] = m_new

    @pl.when(ki < last)
    def _():
        step(False)

    visible = qi * tq + tq - ki * tk

    for piece in range(1, tk // ATT_TQ + 1):
        lo, hi = (piece - 1) * ATT_TQ, piece * ATT_TQ

        @pl.when((ki == last) & (visible > lo) & (visible <= hi))
        def _(hi=hi):
            step(True, hi)

    @pl.when(ki == last)
    def _():
        acc = acc_scr[...]
        l = jnp.sum(acc[:, KV_RANK:KV_RANK + LANES], axis=-1, keepdims=True)
        y_ref[...] = _project_out(acc[:, 0:KV_RANK] / l, wuv_ref, tq)


def _attn_prompt_tiles(cqn, of, kp, wf, hm, wuv, t_len, *, tq, q_tiles, y_prev=None):
    nb = kp.shape[0]
    tk = ATT_TK
    pairs = [(i, j) for i in q_tiles for j in range((i * tq + tq - 1) // tk + 1)]
    q_tab = jnp.asarray([p[0] for p in pairs], jnp.int32)
    k_tab = jnp.asarray([p[1] for p in pairs], jnp.int32)
    rows = H_C * tq
    const = lambda w: pl.BlockSpec(w.shape, lambda b, t, qt, kt: (0,) * w.ndim)
    in_specs = [pl.BlockSpec((None, tq, Q_RANK), lambda b, t, qt, kt: (b, qt[t], 0)),
                pl.BlockSpec((None, tq, 256), lambda b, t, qt, kt: (b, qt[t], 0)),
                pl.BlockSpec((None, tk, 512), lambda b, t, qt, kt: (b, kt[t], 0)),
                const(wf), const(hm), const(wuv)]
    args = [q_tab, k_tab, cqn, of, kp, wf, hm, wuv]
    aliases = {}
    if y_prev is not None:
        in_specs.append(pl.BlockSpec(memory_space=pl.ANY))
        args.append(y_prev)
        aliases = {len(args) - 1: 0}
    grid_spec = pltpu.PrefetchScalarGridSpec(
        num_scalar_prefetch=2,
        grid=(nb, len(pairs)),
        in_specs=in_specs,
        out_specs=pl.BlockSpec((None, tq, W_C), lambda b, t, qt, kt: (b, qt[t], 0)),
        scratch_shapes=[pltpu.VMEM((rows, KV_RANK), BF16), pltpu.VMEM((rows, 256), BF16),
                        pltpu.VMEM((rows, LANES), F32), pltpu.VMEM((rows, KV_RANK + LANES), F32)],
    )
    return pl.pallas_call(
        functools.partial(_attn_prompt_kernel, has_prev=y_prev is not None),
        grid_spec=grid_spec,
        out_shape=jax.ShapeDtypeStruct((nb, t_len, W_C), F32),
        input_output_aliases=aliases,
        compiler_params=_cparams("parallel", "arbitrary"), name="attn_prompt",
    )(*args)


def _attn_prompt(cqn, of, kp, wf, hm, wuv, t_len):
    n_full = t_len // ATT_TQ
    rem = t_len - n_full * ATT_TQ
    y = jnp.zeros((kp.shape[0], t_len, W_C), F32)
    y = _attn_prompt_tiles(cqn, of, kp, wf, hm, wuv, t_len, tq=ATT_TQ, q_tiles=range(n_full), y_prev=y)
    if rem:
        assert rem % 16 == 0 and (n_full * ATT_TQ) % rem == 0
        y = _attn_prompt_tiles(cqn, of, kp, wf, hm, wuv, t_len, tq=rem, q_tiles=[n_full * ATT_TQ // rem],
                               y_prev=y)
    return y


def _mla_branch_prompt(z3, mla_w, wf, wuv, hm):
    nb, t_len, _ = z3.shape
    tp = -(-t_len // ATT_TK) * ATT_TK
    cos, sin = _rope_tables(jnp.arange(tp))
    ckv, kr, kp, cqn, of = _mla_prep(z3, cos, sin, mla_w, tm=ATT_TK, tp=tp)
    yc = _attn_prompt(cqn, of, kp, wf, hm, wuv, t_len)
    return yc, ckv, kr


def _page_copies(pt_ref, ckv_hbm, kr_hbm, kbuf, krbuf, sem, seq, slot, pages=None, *, layer, n_pages):
    out = []
    for pg in (range(n_pages) if pages is None else pages):
        pid = pt_ref[seq * n_pages + pg]
        out.append(pltpu.make_async_copy(ckv_hbm.at[layer, pid], kbuf.at[slot, pg], sem.at[slot, 0]))
        out.append(pltpu.make_async_copy(kr_hbm.at[layer, pid], krbuf.at[slot, pg], sem.at[slot, 1]))
    return out


def _attn_sample_kernel(pt_ref, ckv_hbm, kr_hbm, cqn_ref, of_ref, cnew_ref, knew_ref, wf_ref, hm_ref,
                        tile_ref, wuv_ref, y_ref, kbuf, krbuf, k_scr, kr_scr, sem, *, layer, n_pages):
    tq = cqn_ref.shape[1]
    page = kbuf.shape[2]
    span = SAMPLE_PAGES_PER_CHUNK * page
    n_chunks = n_pages // SAMPLE_PAGES_PER_CHUNK
    step = pl.program_id(0)
    last_step = pl.num_programs(0) - 1
    copies = functools.partial(_page_copies, pt_ref, ckv_hbm, kr_hbm, kbuf, krbuf, sem,
                               layer=layer, n_pages=n_pages)

    def attend(j, slot, seq_after):
        lat, rope = _build_q(cqn_ref[j], of_ref[j], wf_ref[...], hm_ref[...], tq)
        qr = _dot(rope, tile_ref[...]).astype(BF16)
        cnew = cnew_ref[j].astype(BF16)
        s_new = _dot_nt(lat, cnew) + _dot_nt(qr, knew_ref[j].astype(BF16))
        tpos = lax.broadcasted_iota(jnp.int32, s_new.shape, 0) % tq
        s_new = jnp.where(lax.broadcasted_iota(jnp.int32, s_new.shape, 1) <= tpos, s_new, NEG_INF)
        m = jnp.max(s_new, axis=-1, keepdims=True)
        p_new = jnp.exp2(s_new - m)
        l = jnp.sum(p_new, axis=-1, keepdims=True)
        acc = _dot(p_new.astype(BF16), cnew)

        def scores(c):
            pages = range(c * SAMPLE_PAGES_PER_CHUNK, (c + 1) * SAMPLE_PAGES_PER_CHUNK)
            for i in pages:
                k_scr[i * page:(i + 1) * page, :] = kbuf[slot, i].astype(BF16)
                kr_scr[:, i * page:(i + 1) * page] = krbuf[slot, i].astype(BF16)
            for cp in copies(seq_after, slot, pages):
                cp.start()
            cols = slice(c * span, (c + 1) * span)
            return _dot_nt(lat, k_scr[cols, :]) + _dot(qr, kr_scr[:, cols])

        ahead = [scores(c) for c in range(SAMPLE_SKEW)]
        for c in range(n_chunks):
            s = ahead.pop(0)
            if c + SAMPLE_SKEW < n_chunks:
                ahead.append(scores(c + SAMPLE_SKEW))
            m_new = jnp.maximum(m, jnp.max(s, axis=-1, keepdims=True))
            alpha = jnp.exp2(m - m_new)
            p = jnp.exp2(s - m_new)
            l = alpha * l + jnp.sum(p, axis=-1, keepdims=True)
            acc = alpha * acc + _dot(p.astype(BF16), k_scr[c * span:(c + 1) * span, :])
            m = m_new
        y_ref[j] = _project_out(acc / l, wuv_ref, tq)

    seq0 = 2 * step

    @pl.when(step == 0)
    def _():
        for slot in range(2):
            for cp in copies(seq0 + slot, slot):
                cp.start()

    after = [jnp.minimum(seq0 + 2 + slot, 2 * last_step + slot) for slot in range(2)]
    for slot in range(2):
        for cp in copies(seq0 + slot, slot):
            cp.wait()
        attend(slot, slot, after[slot])

    @pl.when(step == last_step)
    def _():
        for slot in range(2):
            for cp in copies(after[slot], slot):
                cp.wait()


def _attn_sample(page_table, cache_ckv, cache_krope_t, layer, cqn, of, cnew, knew, wf, hm, tile_t, wuv):
    nb, tq, _ = cqn.shape
    n_pages = page_table.shape[1]
    page = cache_ckv.shape[2]
    assert nb % 2 == 0
    per_step = lambda width: pl.BlockSpec((2, tq, width), lambda i, pt: (i, 0, 0))
    const = lambda w: pl.BlockSpec(w.shape, lambda i, pt: (0,) * w.ndim)
    hbm = pl.BlockSpec(memory_space=pl.ANY)
    grid_spec = pltpu.PrefetchScalarGridSpec(
        num_scalar_prefetch=1,
        grid=(nb // 2,),
        in_specs=[hbm, hbm, per_step(Q_RANK), per_step(256), per_step(KV_RANK), per_step(ROPE_DIM),
                  const(wf), const(hm), const(tile_t), const(wuv)],
        out_specs=per_step(W_C),
        scratch_shapes=[pltpu.VMEM((2, n_pages, page, KV_RANK), F32),
                        pltpu.VMEM((2, n_pages, ROPE_DIM, page), F32),
                        pltpu.VMEM((n_pages * page, KV_RANK), BF16),
                        pltpu.VMEM((ROPE_DIM, n_pages * page), BF16),
                        pltpu.SemaphoreType.DMA((2, 2))],
    )
    return pl.pallas_call(
        functools.partial(_attn_sample_kernel, layer=layer, n_pages=n_pages),
        grid_spec=grid_spec,
        out_shape=jax.ShapeDtypeStruct((nb, tq, W_C), F32),
        compiler_params=_cparams("arbitrary"), name="attn_sample",
    )(page_table.reshape(-1), cache_ckv, cache_krope_t, cqn, of, cnew, knew, wf, hm, tile_t, wuv)


def _mla_branch_sample(z3, page_table, cache_ckv, cache_krope_t, layer, n_past, mla_w, wf, wuv, hm, tile_t):
    nb, ts, _ = z3.shape
    cos, sin = _rope_tables(n_past + jnp.arange(nb * ts) % ts)
    ckv, kr, _, cqn, of = _mla_prep(z3.reshape(1, nb * ts, Z_COLS), cos, sin, mla_w, tm=ATT_TK, tp=nb * ts)
    ckv = ckv.reshape(nb, ts, KV_RANK)
    kr = kr.reshape(nb, ts, ROPE_DIM)
    yc = _attn_sample(page_table, cache_ckv, cache_krope_t, layer, cqn.reshape(nb, ts, Q_RANK),
                      of.reshape(nb, ts, 256), ckv, kr, wf, hm, tile_t, wuv)
    return yc, ckv, kr


def _out_kernel(ya_ref, yb_ref, yc_ref, gc_ref, x_ref, wa_ref, wb_ref, wc_ref, g_ref, b_ref, o_ref):
    yc = (yc_ref[...] * _silu(gc_ref[...])).astype(BF16)
    out = _dot(ya_ref[...], wa_ref[...]) + _dot(yb_ref[...], wb_ref[...]) + _dot(yc, wc_ref[...])
    h = DN_ALPHA * x_ref[...] + out
    mu = jnp.mean(h, axis=-1, keepdims=True)
    d = h - mu
    var = jnp.mean(d * d, axis=-1, keepdims=True)
    o_ref[...] = d * lax.rsqrt(var + 1e-5) * g_ref[...] + b_ref[...]


def _out_proj(ya, yb, yc, z, x, ow, tm):
    n = x.shape[0]
    wspecs = [pl.BlockSpec(w.shape, lambda i: (0,) * w.ndim) for w in ow]
    rowblk = lambda width, col=0: pl.BlockSpec((tm, width), lambda i: (i, col))
    return pl.pallas_call(
        _out_kernel,
        grid=(n // tm,),
        in_specs=[rowblk(W_A), rowblk(W_B), rowblk(W_C), rowblk(W_C, 3), rowblk(D_MODEL)] + wspecs,
        out_specs=rowblk(D_MODEL),
        out_shape=jax.ShapeDtypeStruct((n, D_MODEL), F32),
        compiler_params=_cparams("parallel"), name="out_proj",
    )(ya, yb, yc, z, x, *ow)


def _out_proj_tail(ya, yb, yc, z3, x, ow, *, skip, tm):
    nb, t_len, _ = z3.shape
    keep = t_len - skip
    seq3 = lambda a: a.reshape(nb, t_len, a.shape[-1])
    wspecs = [pl.BlockSpec(w.shape, lambda b, j: (0,) * w.ndim) for w in ow]
    rows = lambda width, col=0: pl.BlockSpec((pl.Element(1), pl.Element(tm), pl.Element(width)),
                                             lambda b, j: (b, pl.multiple_of(skip + j * tm, skip), col * width))

    def tail_kernel(ya_ref, yb_ref, yc_ref, gc_ref, x_ref, *rest):
        _out_kernel(ya_ref.at[0], yb_ref.at[0], yc_ref.at[0], gc_ref.at[0], x_ref.at[0], *rest)

    return pl.pallas_call(
        tail_kernel,
        grid=(nb, keep // tm),
        in_specs=[rows(W_A), rows(W_B), rows(W_C), rows(W_C, 3), rows(D_MODEL)] + wspecs,
        out_specs=pl.BlockSpec((None, tm, D_MODEL), lambda b, j: (b, j, 0)),
        out_shape=jax.ShapeDtypeStruct((nb, keep, D_MODEL), F32),
        compiler_params=_cparams("parallel", "parallel"), name="out_proj_tail",
    )(seq3(ya), seq3(yb), seq3(yc), z3, seq3(x), *ow)


def _block_diag(blocks):
    nblk, bw, _ = blocks.shape
    eye = jnp.eye(nblk, dtype=blocks.dtype)
    return jnp.einsum('hij,hg->higj', blocks, eye).reshape(nblk * bw, nblk * bw)


def _pad_to(x, shape):
    return jnp.pad(x, [(0, s - d) for d, s in zip(x.shape, shape)])


def _rope_tables(pos):
    inv = ROPE_THETA ** (-2.0 * jnp.arange(ROPE_HALF, dtype=F32) / ROPE_DIM)
    ang = pos.astype(F32)[:, None] * inv[None, :]
    return jnp.tile(jnp.cos(ang), (1, H_C)), jnp.tile(jnp.sin(ang), (1, H_C))


def _static_tables():
    lane = np.arange(256)
    hm = np.stack([((lane % LANES) // ROPE_HALF == h) for h in range(H_C)]).astype(np.float32)
    hmb = np.stack([(lane // HD_B == h) for h in range(H_B)]).astype(np.float32)
    ones_bd = (lane[:, None] // HD_B == lane[None, :] // HD_B).astype(np.float32)
    tile_t = np.zeros((256, ROPE_DIM), np.float32)
    for ln in lane:
        tile_t[ln, (ln // LANES) * ROPE_HALF + ln % ROPE_HALF] = 1.0
    return jnp.asarray(hm), jnp.asarray(hmb), jnp.asarray(ones_bd, BF16), jnp.asarray(tile_t, BF16)


_PER_LAYER = ('w_in', 'conv_w', 'conv_b', 'lru_wr', 'lru_br', 'lru_wi', 'lru_bi', 'lru_lambda', 'rk_mix',
              'rk_wr', 'rk_wk', 'rk_wv', 'rk_w0', 'rk_w1', 'rk_w2', 'rk_a0', 'rk_a1', 'rk_a2', 'rk_kk',
              'rk_ka', 'rk_rk', 'rk_gn_g', 'rk_gn_b', 'q_norm', 'kv_norm', 'w_uq', 'w_uk', 'w_uv', 'w_out',
              'ln_g', 'ln_b')


def kernel(x_prompt, x_sample, cache_ckv, cache_krope, state_conv, state_lru, state_shift, state_wkv,
           page_table, meta_tokens, w_in, conv_w, conv_b, lru_wr, lru_br, lru_wi, lru_bi, lru_lambda,
           rk_mix, rk_wr, rk_wk, rk_wv, rk_w0, rk_w1, rk_w2, rk_a0, rk_a1, rk_a2, rk_kk, rk_ka, rk_rk,
           rk_gn_g, rk_gn_b, q_norm, kv_norm, w_uq, w_uk, w_uv, w_out, ln_g, ln_b):
    stacked = dict(zip(_PER_LAYER, (w_in, conv_w, conv_b, lru_wr, lru_br, lru_wi, lru_bi, lru_lambda, rk_mix,
                                    rk_wr, rk_wk, rk_wv, rk_w0, rk_w1, rk_w2, rk_a0, rk_a1, rk_a2, rk_kk,
                                    rk_ka, rk_rk, rk_gn_g, rk_gn_b, q_norm, kv_norm, w_uq, w_uk, w_uv, w_out,
                                    ln_g, ln_b)))
    bp, seq, _ = x_prompt.shape
    bs, ts, _ = x_sample.shape
    tp_len = N_META + seq
    n_past = page_table.shape[1] * cache_ckv.shape[2]
    depth = w_in.shape[0]
    assert ts == 8 and CHUNK % ts == 0 and bs % (CHUNK // ts) == 0
    assert tp_len % 24 == 0 and seq % TAIL_TILE == 0 and N_META % 8 == 0

    hm, hmb, ones_bd, tile_t = _static_tables()
    row2 = lambda v: v.reshape(1, -1)
    cache_krope_t = jnp.swapaxes(cache_krope, 2, 3)

    xp = jnp.concatenate(
        [jnp.broadcast_to(meta_tokens[None].astype(x_prompt.dtype), (bp, N_META, D_MODEL)), x_prompt],
        axis=1).reshape(bp * tp_len, D_MODEL)
    xs = x_sample.reshape(bs * ts, D_MODEL)
    tm_p = tp_len // 3
    tm_s = bs * ts

    p_states, s_states = [], []
    for l in range(depth):
        p = {name: arr[l] for name, arr in stacked.items()}
        wi = p['w_in']
        kr_cols = wi[:, 1536:1568]
        w_in_r = jnp.concatenate(
            [wi[:, 0:1536], wi[:, 1568:2080],
             jnp.tile(kr_cols[:, 0:ROPE_HALF], (1, H_C)), jnp.tile(kr_cols[:, ROPE_HALF:], (1, H_C))],
            axis=1).astype(BF16)
        lru_w = _rglru_weights(p)
        prep_w, wkv_w = _rwkv_weights(p, hmb, ones_bd)
        mla_w, wf, wuv = _mla_weights(p)
        wo = p['w_out'].astype(BF16)
        out_w = (wo[0:W_A], wo[W_A:W_A + W_B], wo[W_A + W_B:], row2(p['ln_g']), row2(p['ln_b']))

        z = _in_proj(xp, w_in_r, tm_p)
        z3 = z.reshape(bp, tp_len, Z_COLS)
        ya, lru_new = _rglru_branch_prompt(z3, lru_w)
        yb, wkv_new = _rwkv_branch_prompt(z3, prep_w, wkv_w)
        yc, ckv, kr = _mla_branch_prompt(z3, mla_w, wf, wuv, hm)
        if l + 1 < depth:
            xp = _out_proj(ya, yb.reshape(bp * tp_len, W_B), yc.reshape(bp * tp_len, W_C), z, xp, out_w, tm_p)
        else:
            y_prompt = _out_proj_tail(ya, yb, yc, z3, xp, out_w, skip=N_META, tm=TAIL_TILE)
        p_states.append((ckv, kr, z3[:, tp_len - (CONV_W - 1):, 0:W_A], lru_new,
                         z3[:, tp_len - 1, 2 * W_A:2 * W_A + W_B], wkv_new))

        z = _in_proj(xs, w_in_r, tm_s)
        z3 = z.reshape(bs, ts, Z_COLS)
        ya, lru_new = _rglru_branch_sample(z3, state_conv[l], state_lru[l], n_past, lru_w)
        yb, wkv_new = _rwkv_branch_sample(z3, state_shift[l], state_wkv[l], prep_w, wkv_w)
        yc, ckv, kr = _mla_branch_sample(z3, page_table, cache_ckv, cache_krope_t, l, n_past,
                                         mla_w, wf, wuv, hm, tile_t)
        x_new = _out_proj(ya, yb.reshape(bs * ts, W_B), yc.reshape(bs * ts, W_C), z, xs, out_w, tm_s)
        s_states.append((ckv, kr, z3[:, ts - (CONV_W - 1):, 0:W_A], lru_new,
                         z3[:, ts - 1, 2 * W_A:2 * W_A + W_B], wkv_new))
        xs = x_new

    y_sample = xs.reshape(bs, ts, D_MODEL)
    p_out = [jnp.stack([st[i] for st in p_states]) for i in range(6)]
    s_out = [jnp.stack([st[i] for st in s_states]) for i in range(6)]
    return (y_prompt, y_sample, *p_out, *s_out)
```
